```python
import jax, jax.numpy as jnp
from jax import lax
import numpy as np

D_MODEL = 1024
BATCH = 16
SEQ = 256
DEPTH = 2
DEC_BATCH = 8
DEC_SEQ = 4096
PAST_LEN = 512

GRID_W = 64
N_MIXERS = 2
N_RET = (DEPTH + 1) // 2
N_ATT = DEPTH // 2
RET_HEADS = 4
RET_DK = D_MODEL // RET_HEADS
RET_DV = 2 * RET_DK
RET_CHUNK = 128
RET_IN = 2 * RET_HEADS * RET_DK + 3 * RET_HEADS * RET_DV
ATT_HEADS = 8
ATT_KV_HEADS = 2
HEAD_DIM = D_MODEL // ATT_HEADS
ROPE_AXIS_DIM = HEAD_DIM // 2
ROPE_THETA = 10000.0
Q_BLOCK = 128
QKV_DIM = (ATT_HEADS + 2 * ATT_KV_HEADS) * HEAD_DIM
N_EXPERTS = 32
N_GROUPS = 8
EXPERTS_PER_GROUP = N_EXPERTS // N_GROUPS
TOP_K = 2
D_EXPERT = D_MODEL // 2
MOE_BLOCK = 128
EPS = 1e-6

kernel_name = 'hybrid_retention_gqa_moe_diffusion_step'


def rms_norm(x, w):
    xf = x.astype(jnp.float32)
    y = xf * lax.rsqrt(jnp.mean(xf * xf, axis=-1, keepdims=True) + EPS)
    return (y * w.astype(jnp.float32)).astype(x.dtype)


def modulate(x, shift, scale):
    return x * (1.0 + scale) + shift


def ada_params(cond, w, b):
    m = jax.nn.silu(cond) @ w + b
    return jnp.split(m, 6, axis=-1)


def axial_rope(n_tokens):
    rows = n_tokens // GRID_W
    row = jnp.repeat(jnp.arange(rows), GRID_W).astype(jnp.float32)
    col = jnp.tile(jnp.arange(GRID_W), rows).astype(jnp.float32)
    inv = ROPE_THETA ** (-jnp.arange(0, ROPE_AXIS_DIM, 2, dtype=jnp.float32) / ROPE_AXIS_DIM)
    ang = jnp.concatenate([row[:, None] * inv, col[:, None] * inv], axis=-1)
    return jnp.cos(ang), jnp.sin(ang)


def apply_axial_rope(x, cos, sin):
    L = x.shape[1]
    half = ROPE_AXIS_DIM // 2
    xf = x.astype(jnp.float32).reshape(*x.shape[:-1], 2, 2, half)
    x1, x2 = xf[..., 0, :], xf[..., 1, :]
    c = cos.reshape(L, 2, half)[None, :, None]
    s = sin.reshape(L, 2, half)[None, :, None]
    out = jnp.stack([x1 * c - x2 * s, x1 * s + x2 * c], axis=-2)
    return out.reshape(x.shape).astype(x.dtype)


def retention_scan(q, k, v, log_gamma, s0):
    Bn, L, H, _ = q.shape
    DV = v.shape[-1]
    C = RET_CHUNK
    n = L // C
    pos = jnp.arange(C, dtype=jnp.float32)
    diff = pos[:, None] - pos[None, :]
    decay = jnp.where(diff >= 0, jnp.exp(jnp.maximum(diff, 0.0)[None] * log_gamma[:, None, None]), 0.0)
    q_dec = jnp.exp((pos + 1.0)[:, None] * log_gamma[None, :])
    k_dec = jnp.exp((C - 1.0 - pos)[:, None] * log_gamma[None, :])
    c_dec = jnp.exp(C * log_gamma)

    def chunks(a):
        return jnp.moveaxis(a.astype(jnp.float32).reshape(Bn, n, C, H, a.shape[-1]), 1, 0)

    def step(s, qkv):
        qc, kc, vc = qkv
        scores = jnp.einsum('bihd,bjhd->bhij', qc, kc) * decay
        inner = jnp.einsum('bhij,bjhe->bihe', scores, vc)
        cross = jnp.einsum('bihd,bhde->bihe', qc * q_dec[None, :, :, None], s)
        s_new = s * c_dec[None, :, None, None] + jnp.einsum('bjhd,bjhe->bhde', kc * k_dec[None, :, :, None], vc)
        return s_new, inner + cross

    s_fin, o = lax.scan(step, s0.astype(jnp.float32), (chunks(q), chunks(k), chunks(v)))
    o = jnp.moveaxis(o, 0, 1).reshape(Bn, L, H, DV)
    return o, s_fin


def retention_mixer(h, s_f0, s_b0, w_in, w_out, gn_w, logit_f, logit_b):
    Bn, L, _ = h.shape
    hq = RET_HEADS * RET_DK
    hv = RET_HEADS * RET_DV
    q, k, v, g_f, g_b = jnp.split(h @ w_in, [hq, 2 * hq, 2 * hq + hv, 2 * hq + 2 * hv], axis=-1)
    q = q.reshape(Bn, L, RET_HEADS, RET_DK)
    k = k.reshape(Bn, L, RET_HEADS, RET_DK) * (RET_DK ** -0.5)
    v = v.reshape(Bn, L, RET_HEADS, RET_DV)
    lg_f = jax.nn.log_sigmoid(logit_f.astype(jnp.float32))
    lg_b = jax.nn.log_sigmoid(logit_b.astype(jnp.float32))
    o_f, s_f = retention_scan(q, k, v, lg_f, s_f0)
    o_b, s_b = retention_scan(q[:, ::-1], k[:, ::-1], v[:, ::-1], lg_b, s_b0)
    o_b = o_b[:, ::-1]
    gn = gn_w.astype(jnp.float32).reshape(RET_HEADS, RET_DV)

    def head_norm(o):
        mu = jnp.mean(o, axis=-1, keepdims=True)
        var = jnp.mean(jnp.square(o - mu), axis=-1, keepdims=True)
        return (o - mu) * lax.rsqrt(var + EPS) * gn

    gate_f = jax.nn.silu(g_f.astype(jnp.float32)).reshape(Bn, L, RET_HEADS, RET_DV)
    gate_b = jax.nn.silu(g_b.astype(jnp.float32)).reshape(Bn, L, RET_HEADS, RET_DV)
    y = head_norm(o_f) * gate_f + head_norm(o_b) * gate_b
    y = y.reshape(Bn, L, hv).astype(h.dtype) @ w_out
    return y, s_f.astype(h.dtype), s_b.astype(h.dtype)


def attn_project(h, w_qkv, q_norm_w, k_norm_w):
    Bn, L, _ = h.shape
    q, k, v = jnp.split(h @ w_qkv, [ATT_HEADS * HEAD_DIM, (ATT_HEADS + ATT_KV_HEADS) * HEAD_DIM], axis=-1)
    q = rms_norm(q.reshape(Bn, L, ATT_HEADS, HEAD_DIM), q_norm_w)
    k = rms_norm(k.reshape(Bn, L, ATT_KV_HEADS, HEAD_DIM), k_norm_w)
    v = v.reshape(Bn, L, ATT_KV_HEADS, HEAD_DIM)
    return q, k, v


def blocked_attention(q, k, v):
    Bn, Lq, H, Dh = q.shape
    G = H // ATT_KV_HEADS
    nb = Lq // Q_BLOCK
    qb = q.reshape(Bn, nb, Q_BLOCK, ATT_KV_HEADS, G, Dh).transpose(1, 0, 2, 3, 4, 5)
    scale = Dh ** -0.5

    def one_block(qi):
        s = jnp.einsum('bqkgd,bskd->bkgqs', qi, k).astype(jnp.float32) * scale
        p = jax.nn.softmax(s, axis=-1).astype(v.dtype)
        return jnp.einsum('bkgqs,bskd->bqkgd', p, v)

    o = lax.map(one_block, qb)
    return o.transpose(1, 0, 2, 3, 4, 5).reshape(Bn, Lq, H * Dh)


def moe_ffn(h, router_w, router_b, w_gate, w_up, w_down):
    Bn, L, D = h.shape
    x = h.reshape(-1, D)
    T = x.shape[0]
    scores = jax.nn.sigmoid((x @ router_w).astype(jnp.float32))
    sel = scores + router_b.astype(jnp.float32)
    grp_score = jnp.sum(lax.top_k(sel.reshape(T, N_GROUPS, EXPERTS_PER_GROUP), 2)[0], axis=-1)
    best = jnp.argmax(grp_score, axis=-1)
    in_group = (jnp.arange(N_EXPERTS) // EXPERTS_PER_GROUP)[None, :] == best[:, None]
    _, top_idx = lax.top_k(jnp.where(in_group, sel, -jnp.inf), TOP_K)
    top_w = jnp.take_along_axis(scores, top_idx, axis=1)
    top_w = top_w / jnp.sum(top_w, axis=-1, keepdims=True)

    A = T * TOP_K
    e_flat = top_idx.reshape(-1)
    order = jnp.argsort(e_flat)
    e_sorted = e_flat[order]
    tok_sorted = order // TOP_K
    counts = jnp.bincount(e_flat, length=N_EXPERTS)
    starts = jnp.cumsum(counts) - counts
    padded = (counts + MOE_BLOCK - 1) // MOE_BLOCK * MOE_BLOCK
    pends = jnp.cumsum(padded)
    pstarts = pends - padded
    dest = pstarts[e_sorted] + (jnp.arange(A) - starts[e_sorted])
    P = A + N_EXPERTS * MOE_BLOCK
    n_blk = P // MOE_BLOCK
    buf = jnp.zeros((P, D), x.dtype).at[dest].set(x[tok_sorted])
    blk_expert = jnp.clip(jnp.searchsorted(pends, jnp.arange(n_blk) * MOE_BLOCK, side='right'), 0, N_EXPERTS - 1)

    def expert_block(args):
        xb, e = args
        hid = jax.nn.silu(xb @ w_gate[e]) * (xb @ w_up[e])
        return hid @ w_down[e]

    out_buf = lax.map(expert_block, (buf.reshape(n_blk, MOE_BLOCK, D), blk_expert)).reshape(P, D)
    y_rows = out_buf[dest] * top_w.reshape(-1)[order][:, None].astype(x.dtype)
    y = jax.ops.segment_sum(y_rows, tok_sorted, num_segments=T)
    return y.reshape(Bn, L, D)


def setup_inputs(seed: int = 0) -> dict:
    key = jax.random.key(seed)
    ks = jax.random.split(key, 26)
    f32 = jnp.float32

    def nrm(k, shape, scale):
        return jax.random.normal(k, shape, f32) * scale

    gamma0 = 1.0 - 2.0 ** (-5.0 - np.arange(RET_HEADS))
    logit0 = jnp.asarray(np.log(gamma0 / (1.0 - gamma0)), f32)
    hv = RET_HEADS * RET_DV
    ret_state = (DEC_BATCH, N_RET, RET_HEADS, RET_DK, RET_DV)
    att_cache = (DEC_BATCH, N_ATT, PAST_LEN, ATT_KV_HEADS, HEAD_DIM)
    return {
        'x_prompt': nrm(ks[0], (BATCH, SEQ, D_MODEL), 1.0),
        'x_sample': nrm(ks[1], (DEC_BATCH, DEC_SEQ, D_MODEL), 1.0),
        'c': nrm(ks[2], (DEC_BATCH, D_MODEL), 1.0),
        'state_ret_fwd': nrm(ks[3], ret_state, 0.3),
        'state_ret_bwd': nrm(ks[4], ret_state, 0.3),
        'cache_attn_k': nrm(ks[5], att_cache, 1.0),
        'cache_attn_v': nrm(ks[6], att_cache, 1.0),
        'c_ctx': nrm(ks[7], (D_MODEL,), 1.0),
        'ada_w': nrm(ks[8], (DEPTH, D_MODEL, 6 * D_MODEL), 0.5 * D_MODEL ** -0.5),
        'ada_b': nrm(ks[9], (DEPTH, 6 * D_MODEL), 0.02),
        'norm1_w': 1.0 + nrm(ks[10], (DEPTH, D_MODEL), 0.02),
        'norm2_w': 1.0 + nrm(ks[11], (DEPTH, D_MODEL), 0.02),
        'ret_w_in': nrm(ks[12], (N_RET, D_MODEL, RET_IN), D_MODEL ** -0.5),
        'ret_w_out': nrm(ks[13], (N_RET, hv, D_MODEL), hv ** -0.5),
        'ret_gn_w': 1.0 + nrm(ks[14], (N_RET, hv), 0.02),
        'ret_decay_fwd': logit0[None, :] + nrm(ks[15], (N_RET, RET_HEADS), 0.1),
        'ret_decay_bwd': logit0[None, :] + nrm(ks[16], (N_RET, RET_HEADS), 0.1),
        'attn_w_qkv': nrm(ks[17], (N_ATT, D_MODEL, QKV_DIM), D_MODEL ** -0.5),
        'attn_w_o': nrm(ks[18], (N_ATT, ATT_HEADS * HEAD_DIM, D_MODEL), (ATT_HEADS * HEAD_DIM) ** -0.5),
        'attn_q_norm': 1.0 + nrm(ks[19], (N_ATT, HEAD_DIM), 0.02),
        'attn_k_norm': 1.0 + nrm(ks[20], (N_ATT, HEAD_DIM), 0.02),
        'router_w': nrm(ks[21], (D_MODEL, N_EXPERTS), D_MODEL ** -0.5),
        'router_b': nrm(ks[22], (N_EXPERTS,), 0.01),
        'moe_w_gate': nrm(ks[23], (DEPTH, N_EXPERTS, D_MODEL, D_EXPERT), D_MODEL ** -0.5),
        'moe_w_up': nrm(ks[24], (DEPTH, N_EXPERTS, D_MODEL, D_EXPERT), D_MODEL ** -0.5),
        'moe_w_down': nrm(ks[25], (DEPTH, N_EXPERTS, D_EXPERT, D_MODEL), D_EXPERT ** -0.5),
    }


def reference(x_prompt, x_sample, c, state_ret_fwd, state_ret_bwd, cache_attn_k, cache_attn_v, c_ctx,
              ada_w, ada_b, norm1_w, norm2_w, ret_w_in, ret_w_out, ret_gn_w, ret_decay_fwd, ret_decay_bwd,
              attn_w_qkv, attn_w_o, attn_q_norm, attn_k_norm, router_w, router_b,
              moe_w_gate, moe_w_up, moe_w_down):
    n_prompt = x_prompt.shape[0]
    cos, sin = axial_rope(x_sample.shape[1])
    y_p = x_prompt
    y_s = x_sample
    ret_f_list, ret_b_list, k_list, v_list = [], [], [], []
    for i in range(DEPTH):
        sh1_p, sc1_p, g1_p, sh2_p, sc2_p, g2_p = ada_params(c_ctx, ada_w[i], ada_b[i])
        sh1_s, sc1_s, g1_s, sh2_s, sc2_s, g2_s = [m[:, None, :] for m in ada_params(c, ada_w[i], ada_b[i])]
        h_p = modulate(rms_norm(y_p, norm1_w[i]), sh1_p, sc1_p)
        h_s = modulate(rms_norm(y_s, norm1_w[i]), sh1_s, sc1_s)
        if i % N_MIXERS == 0:
            r = i // N_MIXERS
            zero = jnp.zeros((n_prompt, RET_HEADS, RET_DK, RET_DV), x_prompt.dtype)
            o_p, s_f, s_b = retention_mixer(h_p, zero, zero, ret_w_in[r], ret_w_out[r], ret_gn_w[r],
                                            ret_decay_fwd[r], ret_decay_bwd[r])
            o_s, _, _ = retention_mixer(h_s, state_ret_fwd[:, r], state_ret_bwd[:, r], ret_w_in[r], ret_w_out[r],
                                        ret_gn_w[r], ret_decay_fwd[r], ret_decay_bwd[r])
            ret_f_list.append(s_f)
            ret_b_list.append(s_b)
        else:
            a = i // N_MIXERS
            q_p, k_p, v_p = attn_project(h_p, attn_w_qkv[a], attn_q_norm[a], attn_k_norm[a])
            o_p = blocked_attention(q_p, k_p, v_p) @ attn_w_o[a]
            q_s, k_s, v_s = attn_project(h_s, attn_w_qkv[a], attn_q_norm[a], attn_k_norm[a])
            q_s = apply_axial_rope(q_s, cos, sin)
            k_s = apply_axial_rope(k_s, cos, sin)
            k_all = jnp.concatenate([cache_attn_k[:, a].astype(k_s.dtype), k_s], axis=1)
            v_all = jnp.concatenate([cache_attn_v[:, a].astype(v_s.dtype), v_s], axis=1)
            o_s = blocked_attention(q_s, k_all, v_all) @ attn_w_o[a]
            k_list.append(k_p)
            v_list.append(v_p)
        y_p = y_p + g1_p * o_p
        y_s = y_s + g1_s * o_s
        y_p = y_p + g2_p * moe_ffn(modulate(rms_norm(y_p, norm2_w[i]), sh2_p, sc2_p), router_w, router_b,
                                   moe_w_gate[i], moe_w_up[i], moe_w_down[i])
        y_s = y_s + g2_s * moe_ffn(modulate(rms_norm(y_s, norm2_w[i]), sh2_s, sc2_s), router_w, router_b,
                                   moe_w_gate[i], moe_w_up[i], moe_w_down[i])
    new_state_ret_fwd = jnp.stack(ret_f_list, axis=1)
    new_state_ret_bwd = jnp.stack(ret_b_list, axis=1)
    new_cache_attn_k = jnp.stack(k_list, axis=1)
    new_cache_attn_v = jnp.stack(v_list, axis=1)
    return (y_p, y_s, new_state_ret_fwd, new_state_ret_bwd, new_cache_attn_k, new_cache_attn_v)
```

```python
import functools

import jax
import jax.numpy as jnp
from jax import lax
from jax.experimental import pallas as pl
from jax.experimental.pallas import tpu as pltpu

F32 = jnp.float32
BF16 = jnp.bfloat16
I32 = jnp.int32

EPS = 1e-6
GRID_W = 64
RET_HEADS = 4
ATT_HEADS = 8
ATT_KV_HEADS = 2
ROPE_THETA = 10000.0
N_EXPERTS = 32
N_GROUPS = 8
EXPERTS_PER_GROUP = N_EXPERTS // N_GROUPS
TOP_K = 2

V7X_VMEM_BYTES = 64 * 1024 * 1024
VMEM_LIMIT_BYTES = 56 * 1024 * 1024


def _params(sem):
    return pltpu.CompilerParams(dimension_semantics=sem, vmem_limit_bytes=VMEM_LIMIT_BYTES)


def _dot(a, b):
    return jnp.dot(a, b, preferred_element_type=F32)


def _dot_nt(a, b):
    return lax.dot_general(a, b, (((1,), (1,)), ((), ())), preferred_element_type=F32)


def _dot_tn(a, b):
    return lax.dot_general(a, b, (((0,), (0,)), ((), ())), preferred_element_type=F32)


def _pick(n, prefs):
    for p in prefs:
        if n % p == 0:
            return p
    return n


def _ada_kernel(c_ref, w_ref, b_ref, o_ref):
    c = c_ref[...]
    s = jax.nn.silu(c).astype(BF16)
    o_ref[...] = _dot(s, w_ref[...].astype(BF16)) + b_ref[...]


def _ada_params(cond, ada_w, ada_b):
    depth, d, n6 = ada_w.shape
    nbp = cond.shape[0]
    tn = _pick(n6, (1536, 1024, 512, 256, 128))
    out = pl.pallas_call(
        _ada_kernel,
        out_shape=jax.ShapeDtypeStruct((depth, nbp, n6), F32),
        grid=(depth, n6 // tn),
        in_specs=[
            pl.BlockSpec((nbp, d), lambda l, j: (0, 0)),
            pl.BlockSpec((None, d, tn), lambda l, j: (l, 0, j)),
            pl.BlockSpec((None, 1, tn), lambda l, j: (l, 0, j)),
        ],
        out_specs=pl.BlockSpec((None, nbp, tn), lambda l, j: (l, 0, j)),
        compiler_params=_params(("arbitrary", "arbitrary")),
        name="ada_params",
    )(cond, ada_w, ada_b.reshape(depth, 1, n6))
    return out.reshape(depth, nbp, 6, 1, d).transpose(0, 2, 1, 3, 4)


def _mod_spec(d, layer, j):
    return pl.BlockSpec((None, None, None, 1, d), lambda b, i: (layer, j, b, 0, 0))


def _norm_mod(x, nw, sh, sc):
    ms = jnp.mean(x * x, axis=-1, keepdims=True)
    h = x * lax.rsqrt(ms + EPS) * nw
    return h * (1.0 + sc) + sh


def _ret_inproj_kernel(x_ref, nw_ref, sh_ref, sc_ref, w_ref, o_ref, *, hq, hv, tn, k_scale):
    hb = _norm_mod(x_ref[...], nw_ref[...], sh_ref[...], sc_ref[...]).astype(BF16)
    n = w_ref.shape[1]
    for j in range(n // tn):
        lo = j * tn
        acc = _dot(hb, w_ref[:, lo:lo + tn])
        if hq <= lo < 2 * hq:
            acc = acc * k_scale
        elif lo >= 2 * hq + hv:
            acc = jax.nn.silu(acc)
        o_ref[:, lo:lo + tn] = acc.astype(BF16)


def _ret_inproj(y, mod, layer, norm_w, w_in_bf16, dk):
    nb, l, d = y.shape
    n = w_in_bf16.shape[1]
    hq = RET_HEADS * dk
    hv = 2 * hq
    tm = _pick(l, (512, 256, 128))
    tn = _pick(hq, (1024, 512, 256, 128))
    kern = functools.partial(_ret_inproj_kernel, hq=hq, hv=hv, tn=tn, k_scale=float(dk) ** -0.5)
    return pl.pallas_call(
        kern,
        out_shape=jax.ShapeDtypeStruct((nb, l, n), BF16),
        grid=(nb, l // tm),
        in_specs=[
            pl.BlockSpec((None, tm, d), lambda b, i: (b, i, 0)),
            pl.BlockSpec((1, d), lambda b, i: (0, 0)),
            _mod_spec(d, layer, 0),
            _mod_spec(d, layer, 1),
            pl.BlockSpec((d, n), lambda b, i: (0, 0), pipeline_mode=pl.Buffered(1)),
        ],
        out_specs=pl.BlockSpec((None, tm, n), lambda b, i: (b, i, 0)),
        compiler_params=_params(("arbitrary", "arbitrary")),
        name="ret_inproj",
    )(y, norm_w.reshape(1, d), mod, mod, w_in_bf16)


def _log_sigmoid(x):
    return jnp.minimum(x, 0.0) - jnp.log1p(jnp.exp(-jnp.abs(x)))


def _ret_kernel(*refs, ls, c, has_init, emit_final):
    it = iter(refs)
    q_ref, k_ref, v_ref, gf_ref, gb_ref, gn_ref, df_ref, db_ref = (next(it) for _ in range(8))
    if has_init:
        s0f_ref, s0b_ref = next(it), next(it)
    y_ref = next(it)
    if emit_final:
        sf_out, sb_out = next(it), next(it)
    s_scr, sb_scr = next(it), next(it)

    dk = q_ref.shape[1]
    nchunks = ls // c
    lg_f = _log_sigmoid(df_ref[...])[:, 0:1]
    lg_b = _log_sigmoid(db_ref[...])[:, 0:1]

    ri = lax.broadcasted_iota(I32, (c, c), 0)
    ci = lax.broadcasted_iota(I32, (c, c), 1)
    dif = (ri - ci).astype(F32)
    decay_f = jnp.where(dif >= 0, jnp.exp(jnp.maximum(dif, 0.0) * lg_f), 0.0)
    decay_b = jnp.where(dif <= 0, jnp.exp(jnp.maximum(-dif, 0.0) * lg_b), 0.0)
    pos = lax.broadcasted_iota(I32, (c, dk), 0).astype(F32)
    qdec_f = jnp.exp((pos + 1.0) * lg_f)
    kdec_f = jnp.exp((c - 1.0 - pos) * lg_f)
    qdec_b = jnp.exp((c - pos) * lg_b)
    kdec_b = jnp.exp(pos * lg_b)
    cdec_f = jnp.exp(c * lg_f)
    cdec_b = jnp.exp(c * lg_b)
    gn = gn_ref[...]

    if has_init:
        s_scr[...] = s0b_ref[...]
    else:
        s_scr[...] = jnp.zeros_like(s_scr)

    def bwd_body(t, carry):
        n = nchunks - 1 - t
        r0 = pl.multiple_of(n * c, c)
        s = s_scr[...]
        sb_scr[n] = s.astype(BF16)
        kc = k_ref[pl.ds(r0, c), :].astype(F32)
        vc = v_ref[pl.ds(r0, c), :]
        s_scr[...] = s * cdec_b + _dot_tn((kc * kdec_b).astype(BF16), vc)
        return carry

    lax.fori_loop(0, nchunks, bwd_body, 0)
    if emit_final:
        sb_out[...] = s_scr[...]

    if has_init:
        s_scr[...] = s0f_ref[...]
    else:
        s_scr[...] = jnp.zeros_like(s_scr)

    def head_norm(o):
        mu = jnp.mean(o, axis=-1, keepdims=True)
        dlt = o - mu
        var = jnp.mean(dlt * dlt, axis=-1, keepdims=True)
        return dlt * lax.rsqrt(var + EPS) * gn

    def fwd_body(n, carry):
        r0 = pl.multiple_of(n * c, c)
        rows = pl.ds(r0, c)
        qc = q_ref[rows, :]
        kc = k_ref[rows, :]
        vc = v_ref[rows, :]
        qf = qc.astype(F32)
        sc = _dot_nt(qc, kc)
        s = s_scr[...]
        o_f = _dot((sc * decay_f).astype(BF16), vc) + _dot((qf * qdec_f).astype(BF16), s.astype(BF16))
        o_b = _dot((sc * decay_b).astype(BF16), vc) + _dot((qf * qdec_b).astype(BF16), sb_scr[n])
        s_scr[...] = s * cdec_f + _dot_tn((kc.astype(F32) * kdec_f).astype(BF16), vc)
        y = head_norm(o_f) * gf_ref[rows, :].astype(F32) + head_norm(o_b) * gb_ref[rows, :].astype(F32)
        y_ref[rows, :] = y.astype(BF16)
        return carry

    lax.fori_loop(0, nchunks, fwd_body, 0)
    if emit_final:
        sf_out[...] = s_scr[...]


def _retention(qkvg, nseq, seq0, ls, dk, gn_w, decay_f, decay_b, s0f=None, s0b=None, emit_final=False):
    h = RET_HEADS
    dv = 2 * dk
    c = _pick(ls, (128,))
    has_init = s0f is not None
    kq, kk, kv, kgf, kgb = 0, h, h, 2 * h, 3 * h
    in_specs = [
        pl.BlockSpec((None, ls, dk), lambda s, hh: (s + seq0, 0, kq + hh)),
        pl.BlockSpec((None, ls, dk), lambda s, hh: (s + seq0, 0, kk + hh)),
        pl.BlockSpec((None, ls, dv), lambda s, hh: (s + seq0, 0, kv + hh)),
        pl.BlockSpec((None, ls, dv), lambda s, hh: (s + seq0, 0, kgf + hh)),
        pl.BlockSpec((None, ls, dv), lambda s, hh: (s + seq0, 0, kgb + hh)),
        pl.BlockSpec((None, 1, dv), lambda s, hh: (hh, 0, 0)),
        pl.BlockSpec((None, 1, 128), lambda s, hh: (hh, 0, 0)),
        pl.BlockSpec((None, 1, 128), lambda s, hh: (hh, 0, 0)),
    ]
    args = [qkvg, qkvg, qkvg, qkvg, qkvg, gn_w.reshape(h, 1, dv),
            jnp.broadcast_to(decay_f.reshape(h, 1, 1), (h, 1, 128)),
            jnp.broadcast_to(decay_b.reshape(h, 1, 1), (h, 1, 128))]
    if has_init:
        in_specs += [pl.BlockSpec((None, None, dk, dv), lambda s, hh: (s, hh, 0, 0))] * 2
        args += [s0f, s0b]
    out_shape = [jax.ShapeDtypeStruct((nseq, ls, h * dv), BF16)]
    out_specs = [pl.BlockSpec((None, ls, dv), lambda s, hh: (s, 0, hh))]
    if emit_final:
        out_shape += [jax.ShapeDtypeStruct((nseq, h, dk, dv), F32)] * 2
        out_specs += [pl.BlockSpec((None, None, dk, dv), lambda s, hh: (s, hh, 0, 0))] * 2
    kern = functools.partial(_ret_kernel, ls=ls, c=c, has_init=has_init, emit_final=emit_final)
    return pl.pallas_call(
        kern,
        out_shape=out_shape,
        grid=(nseq, h),
        in_specs=in_specs,
        out_specs=out_specs,
        scratch_shapes=[pltpu.VMEM((dk, dv), F32), pltpu.VMEM((ls // c, dk, dv), BF16)],
        compiler_params=_params(("arbitrary", "arbitrary")),
        name="retention_init" if has_init else "retention_zero",
    )(*args)


def _qkv_kernel(x_ref, nw_ref, sh_ref, sc_ref, w_ref, qn_ref, kn_ref, cos_ref, sa_ref, sb_ref,
                o_ref, kv_ref, *, dh):
    hb = _norm_mod(x_ref[...], nw_ref[...], sh_ref[...], sc_ref[...]).astype(BF16)
    nq = ATT_HEADS * dh
    nk = ATT_KV_HEADS * dh
    cos, sa, sb = cos_ref[...], sa_ref[...], sb_ref[...]

    def norm_rope(a, w):
        a = a * lax.rsqrt(jnp.mean(a * a, axis=-1, keepdims=True) + EPS) * w
        roped = a * cos + pltpu.roll(a, dh - dh // 4, 1) * sa + pltpu.roll(a, dh // 4, 1) * sb
        return a, roped

    q = _dot(hb, w_ref[:, :nq])
    for hh in range(ATT_HEADS):
        _, r = norm_rope(q[:, hh * dh:(hh + 1) * dh], qn_ref[...])
        o_ref[:, hh * dh:(hh + 1) * dh] = r.astype(BF16)
    kvv = _dot(hb, w_ref[:, nq:])
    for hh in range(ATT_KV_HEADS):
        a, r = norm_rope(kvv[:, hh * dh:(hh + 1) * dh], kn_ref[...])
        o_ref[:, nq + hh * dh:nq + (hh + 1) * dh] = r.astype(BF16)
        kv_ref[:, hh * dh:(hh + 1) * dh] = a
    o_ref[:, nq + nk:] = kvv[:, nk:].astype(BF16)
    kv_ref[:, nk:] = kvv[:, nk:]


def _rope_tables(l, dh, rotate):
    axis_dim = dh // 2
    half = axis_dim // 2
    t = jnp.arange(l)
    row = (t // GRID_W).astype(F32)
    col = (t % GRID_W).astype(F32)
    inv = ROPE_THETA ** (-jnp.arange(0, axis_dim, 2, dtype=F32) / axis_dim)
    d = jnp.arange(dh)
    pos = jnp.where((d // axis_dim)[None, :] == 0, row[:, None], col[:, None])
    ang = pos * inv[d % half][None, :]
    first = ((d % axis_dim) < half)[None, :]
    cos, sin = jnp.cos(ang), jnp.sin(ang)
    if not rotate:
        cos, sin = jnp.ones_like(cos), jnp.zeros_like(sin)
    return cos, jnp.where(first, -sin, 0.0), jnp.where(first, 0.0, sin)


def _qkv_proj(y, ns, mod, layer, norm_w, w_bf16, qn, kn, dh):
    nb, l, d = y.shape
    n = w_bf16.shape[1]
    nkv = 2 * ATT_KV_HEADS * dh
    tm = _pick(l, (512, 256, 128))
    tabs = [jnp.stack([a, b]) for a, b in zip(_rope_tables(l, dh, True), _rope_tables(l, dh, False))]
    tab_spec = pl.BlockSpec((None, tm, dh), lambda b, i: (jnp.where(b < ns, 0, 1), i, 0))
    return pl.pallas_call(
        functools.partial(_qkv_kernel, dh=dh),
        out_shape=[jax.ShapeDtypeStruct((nb, l, n), BF16), jax.ShapeDtypeStruct((nb, l, nkv), F32)],
        grid=(nb, l // tm),
        in_specs=[
            pl.BlockSpec((None, tm, d), lambda b, i: (b, i, 0)),
            pl.BlockSpec((1, d), lambda b, i: (0, 0)),
            _mod_spec(d, layer, 0),
            _mod_spec(d, layer, 1),
            pl.BlockSpec((d, n), lambda b, i: (0, 0), pipeline_mode=pl.Buffered(1)),
            pl.BlockSpec((1, dh), lambda b, i: (0, 0)),
            pl.BlockSpec((1, dh), lambda b, i: (0, 0)),
            tab_spec, tab_spec, tab_spec,
        ],
        out_specs=[pl.BlockSpec((None, tm, n), lambda b, i: (b, i, 0)),
                   pl.BlockSpec((None, tm, nkv), lambda b, i: (b, i, 0))],
        compiler_params=_params(("arbitrary", "arbitrary")),
        name="qkv_proj",
    )(y, norm_w.reshape(1, d), mod, mod, w_bf16, qn.reshape(1, dh), kn.reshape(1, dh), *tabs)


def _attn_kernel(*refs, dh, group, has_cache, scale):
    if has_cache:
        q_ref, k_ref, v_ref, kc_ref, vc_ref, o_ref = refs
        kc = kc_ref[...].astype(BF16)
        vc = vc_ref[...].astype(BF16)
    else:
        q_ref, k_ref, v_ref, o_ref = refs
    k = k_ref[...]
    v = v_ref[...]
    for g in range(group):
        q = q_ref[:, g * dh:(g + 1) * dh]
        s1 = _dot_nt(q, k)
        m = jnp.max(s1, axis=-1, keepdims=True)
        if has_cache:
            s0 = _dot_nt(q, kc)
            m = jnp.maximum(m, jnp.max(s0, axis=-1, keepdims=True))
        p1 = jnp.exp((s1 - m) * scale)
        den = jnp.sum(p1, axis=-1, keepdims=True)
        o = _dot(p1.astype(BF16), v)
        if has_cache:
            p0 = jnp.exp((s0 - m) * scale)
            den = den + jnp.sum(p0, axis=-1, keepdims=True)
            o = o + _dot(p0.astype(BF16), vc)
        o_ref[:, g * dh:(g + 1) * dh] = (o / den).astype(BF16)


def _attention(qkv, nseq, seq0, ls, dh, cache_k=None, cache_v=None):
    group = ATT_HEADS // ATT_KV_HEADS
    has_cache = cache_k is not None
    tq = _pick(ls, (256, 128))
    kcol = ATT_HEADS
    vcol = ATT_HEADS + ATT_KV_HEADS
    in_specs = [
        pl.BlockSpec((None, tq, group * dh), lambda s, kh, i: (s + seq0, i, kh)),
        pl.BlockSpec((None, ls, dh), lambda s, kh, i: (s + seq0, 0, kcol + kh)),
        pl.BlockSpec((None, ls, dh), lambda s, kh, i: (s + seq0, 0, vcol + kh)),
    ]
    args = [qkv, qkv, qkv]
    if has_cache:
        past = cache_k.shape[1]
        in_specs += [pl.BlockSpec((None, past, dh), lambda s, kh, i: (s, 0, kh))] * 2
        args += [cache_k, cache_v]
    kern = functools.partial(_attn_kernel, dh=dh, group=group, has_cache=has_cache, scale=float(dh) ** -0.5)
    return pl.pallas_call(
        kern,
        out_shape=jax.ShapeDtypeStruct((nseq, ls, ATT_HEADS * dh), BF16),
        grid=(nseq, ATT_KV_HEADS, ls // tq),
        in_specs=in_specs,
        out_specs=pl.BlockSpec((None, tq, group * dh), lambda s, kh, i: (s, i, kh)),
        compiler_params=_params(("arbitrary", "arbitrary", "arbitrary")),
        name="attention_cache" if has_cache else "attention_self",
    )(*args)


def _outproj_kernel(os_ref, op_ref, w_ref, y_ref, g1_ref, nw_ref, sh_ref, sc_ref, rwh_ref, rwl_ref,
                    yo_ref, h_ref, lg_ref, *, ns):
    is_prompt = pl.program_id(0) >= ns
    o = jnp.where(is_prompt, op_ref[...], os_ref[...])
    y = y_ref[...] + g1_ref[...] * _dot(o, w_ref[...])
    yo_ref[...] = y
    h = _norm_mod(y, nw_ref[...], sh_ref[...], sc_ref[...])
    h_ref[...] = h
    h_hi = h.astype(BF16)
    h_lo = (h - h_hi.astype(F32)).astype(BF16)
    rwh = rwh_ref[...]
    lg_ref[...] = _dot_nt(rwh, h_hi) + _dot_nt(rwl_ref[...], h_hi) + _dot_nt(rwh, h_lo)


def _outproj(o_s, o_p, w_bf16, y, mod, layer, norm_w, rw_hi, rw_lo):
    nb, l, d = y.shape
    ns = o_s.shape[0]
    kdim = o_s.shape[2]
    ne = rw_hi.shape[0]
    tm = _pick(l, (512, 256, 128))
    ni = l // tm
    return pl.pallas_call(
        functools.partial(_outproj_kernel, ns=ns),
        out_shape=[jax.ShapeDtypeStruct((nb, l, d), F32), jax.ShapeDtypeStruct((nb, l, d), F32),
                   jax.ShapeDtypeStruct((nb, ne, l), F32)],
        grid=(nb, ni),
        in_specs=[
            pl.BlockSpec((None, tm, kdim), lambda b, i: (jnp.minimum(b, ns - 1), jnp.where(b < ns, i, ni - 1), 0)),
            pl.BlockSpec((None, tm, kdim), lambda b, i: (0, jnp.where(b < ns, 0, i), 0)),
            pl.BlockSpec((kdim, d), lambda b, i: (0, 0), pipeline_mode=pl.Buffered(1)),
            pl.BlockSpec((None, tm, d), lambda b, i: (b, i, 0)),
            _mod_spec(d, layer, 2),
            pl.BlockSpec((1, d), lambda b, i: (0, 0)),
            _mod_spec(d, layer, 3),
            _mod_spec(d, layer, 4),
            pl.BlockSpec((ne, d), lambda b, i: (0, 0)),
            pl.BlockSpec((ne, d), lambda b, i: (0, 0)),
        ],
        out_specs=[pl.BlockSpec((None, tm, d), lambda b, i: (b, i, 0)),
                   pl.BlockSpec((None, tm, d), lambda b, i: (b, i, 0)),
                   pl.BlockSpec((None, ne, tm), lambda b, i: (b, 0, i))],
        compiler_params=_params(("arbitrary", "arbitrary")),
        name="outproj_router",
    )(o_s, o_p, w_bf16, y, mod, norm_w.reshape(1, d), mod, mod, rw_hi, rw_lo)


def _router_kernel(lg_ref, b_ref, idx_ref, w_ref):
    ng = N_GROUPS
    sc = jax.nn.sigmoid(lg_ref[...])
    sel = sc + b_ref[...]
    sj = [sel[j * ng:(j + 1) * ng] for j in range(EXPERTS_PER_GROUP)]
    pj = [sc[j * ng:(j + 1) * ng] for j in range(EXPERTS_PER_GROUP)]
    hi1, lo1 = jnp.maximum(sj[0], sj[1]), jnp.minimum(sj[0], sj[1])
    hi2, lo2 = jnp.maximum(sj[2], sj[3]), jnp.minimum(sj[2], sj[3])
    top1 = jnp.maximum(hi1, hi2)
    top2 = jnp.maximum(jnp.minimum(hi1, hi2), jnp.maximum(lo1, lo2))
    gs = top1 + top2
    gmax = jnp.max(gs, axis=0, keepdims=True)
    gi = lax.broadcasted_iota(I32, gs.shape, 0)
    best = jnp.min(jnp.where(gs == gmax, gi, ng), axis=0, keepdims=True)
    onehot = gi == best
    neg = -jnp.inf
    vj = [jnp.max(jnp.where(onehot, a, neg), axis=0, keepdims=True) for a in sj]
    wj = [jnp.max(jnp.where(onehot, a, neg), axis=0, keepdims=True) for a in pj]

    def argmax4(vals):
        m, i = vals[0], jnp.zeros(vals[0].shape, I32)
        for j in range(1, EXPERTS_PER_GROUP):
            gt = vals[j] > m
            m = jnp.where(gt, vals[j], m)
            i = jnp.where(gt, j, i)
        return i

    i1 = argmax4(vj)
    i2 = argmax4([jnp.where(i1 == j, neg, vj[j]) for j in range(EXPERTS_PER_GROUP)])

    def take(vals, i):
        out = vals[0]
        for j in range(1, EXPERTS_PER_GROUP):
            out = jnp.where(i == j, vals[j], out)
        return out

    w1, w2 = take(wj, i1), take(wj, i2)
    tot = w1 + w2
    idx_ref[0:1, :] = best * EXPERTS_PER_GROUP + i1
    idx_ref[1:2, :] = best * EXPERTS_PER_GROUP + i2
    w_ref[0:1, :] = w1 / tot
    w_ref[1:2, :] = w2 / tot


def _router(logits, bias_perm):
    nb, ne, l = logits.shape
    tl = _pick(l, (2048, 1024, 512, 256, 128))
    return pl.pallas_call(
        _router_kernel,
        out_shape=[jax.ShapeDtypeStruct((nb, TOP_K, l), I32), jax.ShapeDtypeStruct((nb, TOP_K, l), F32)],
        grid=(nb, l // tl),
        in_specs=[pl.BlockSpec((None, ne, tl), lambda b, i: (b, 0, i)),
                  pl.BlockSpec((ne, 1), lambda b, i: (0, 0))],
        out_specs=[pl.BlockSpec((None, TOP_K, tl), lambda b, i: (b, 0, i)),
                   pl.BlockSpec((None, TOP_K, tl), lambda b, i: (b, 0, i))],
        compiler_params=_params(("arbitrary", "arbitrary")),
        name="router_topk",
    )(logits, bias_perm.reshape(ne, 1))


def _moe_kernel(be_ref, idx_hbm, h_hbm, rw_ref, wg_ref, wu_ref, wd_ref, out_hbm,
                idx_smem, xbuf, obuf, idx_sem, in_sem, out_sem, *, bm, nblk):
    i = pl.program_id(0)
    slot = i % 2
    nxt = 1 - slot

    def idx_copy(blk, s):
        return pltpu.make_async_copy(idx_hbm.at[blk], idx_smem.at[s], idx_sem.at[s])

    def row_in(s, r, tok):
        return pltpu.make_async_copy(h_hbm.at[tok], xbuf.at[s, r], in_sem.at[s])

    def row_out(s, r, dst):
        return pltpu.make_async_copy(obuf.at[s, r], out_hbm.at[dst], out_sem.at[s])

    def start_gather(s):
        for r in range(bm):
            row_in(s, r, idx_smem[s, r]).start()

    def wait_gather(s):
        for r in range(bm):
            row_in(s, r, 0).wait()

    def start_scatter(s):
        for r in range(bm):
            row_out(s, r, idx_smem[s, bm + r]).start()

    def wait_scatter(s):
        for r in range(bm):
            row_out(s, r, 0).wait()

    @pl.when(i == 0)
    def _():
        idx_copy(0, 0).start()
        idx_copy(0, 0).wait()
        start_gather(0)

    @pl.when(i + 1 < nblk)
    def _():
        idx_copy(i + 1, nxt).start()
        idx_copy(i + 1, nxt).wait()

    @pl.when(jnp.logical_and(i + 1 < nblk, nxt == 0))
    def _():
        start_gather(0)

    @pl.when(jnp.logical_and(i + 1 < nblk, nxt == 1))
    def _():
        start_gather(1)

    def compute(s):
        wait_gather(s)
        x = xbuf[s].astype(BF16)
        hid = jax.nn.silu(_dot(x, wg_ref[...])) * _dot(x, wu_ref[...])
        out = _dot(hid.astype(BF16), wd_ref[...]) * rw_ref[...]

        @pl.when(i >= 2)
        def _():
            wait_scatter(s)

        obuf[s] = out
        start_scatter(s)

    @pl.when(slot == 0)
    def _():
        compute(0)

    @pl.when(slot == 1)
    def _():
        compute(1)

    @pl.when(i == nblk - 1)
    def _():
        if nblk > 1:
            @pl.when(slot == 0)
            def _():
                wait_scatter(1)

            @pl.when(slot == 1)
            def _():
                wait_scatter(0)

        @pl.when(slot == 0)
        def _():
            wait_scatter(0)

        @pl.when(slot == 1)
        def _():
            wait_scatter(1)


def _moe_experts(h, blk_expert, row_idx, row_w, wg, wu, wd, bm):
    t, d = h.shape
    nblk = blk_expert.shape[0]
    p = nblk * bm
    de = wg.shape[2]
    grid_spec = pltpu.PrefetchScalarGridSpec(
        num_scalar_prefetch=1,
        grid=(nblk,),
        in_specs=[
            pl.BlockSpec(memory_space=pl.ANY),
            pl.BlockSpec(memory_space=pl.ANY),
            pl.BlockSpec((bm, 1), lambda i, be: (i, 0)),
            pl.BlockSpec((None, d, de), lambda i, be: (be[i], 0, 0)),
            pl.BlockSpec((None, d, de), lambda i, be: (be[i], 0, 0)),
            pl.BlockSpec((None, de, d), lambda i, be: (be[i], 0, 0)),
        ],
        out_specs=pl.BlockSpec(memory_space=pl.ANY),
        scratch_shapes=[
            pltpu.SMEM((2, 2 * bm), I32),
            pltpu.VMEM((2, bm, d), F32),
            pltpu.VMEM((2, bm, d), F32),
            pltpu.SemaphoreType.DMA((2,)),
            pltpu.SemaphoreType.DMA((2,)),
            pltpu.SemaphoreType.DMA((2,)),
        ],
    )
    return pl.pallas_call(
        functools.partial(_moe_kernel, bm=bm, nblk=nblk),
        out_shape=jax.ShapeDtypeStruct((p, d), F32),
        grid_spec=grid_spec,
        compiler_params=_params(("arbitrary",)),
        name="moe_experts",
    )(blk_expert, row_idx, h, row_w, wg, wu, wd)


def _dispatch(top_idx, top_w, bm):
    nb, k, l = top_idx.shape
    t = nb * l
    a = t * k
    e_flat = top_idx.transpose(0, 2, 1).reshape(a)
    w_flat = top_w.transpose(0, 2, 1).reshape(a)
    order = jnp.argsort(e_flat).astype(I32)
    e_sorted = e_flat[order]
    counts = jnp.bincount(e_flat, length=N_EXPERTS).astype(I32)
    starts = jnp.cumsum(counts) - counts
    padded = (counts + bm - 1) // bm * bm
    pends = jnp.cumsum(padded)
    pstarts = pends - padded
    dest = pstarts[e_sorted] + (jnp.arange(a, dtype=I32) - starts[e_sorted])
    p = a + N_EXPERTS * bm
    nblk = p // bm
    is_pad = jnp.ones((p,), I32).at[dest].set(0)
    pad_rank = jnp.cumsum(is_pad) - 1
    row_tok = jnp.zeros((p,), I32).at[dest].set(order // k)
    row_dst = jnp.where(is_pad == 1, a + pad_rank, jnp.zeros((p,), I32).at[dest].set(order)).astype(I32)
    row_w = jnp.zeros((p,), F32).at[dest].set(w_flat[order])
    blk_expert = jnp.clip(jnp.searchsorted(pends, jnp.arange(nblk, dtype=I32) * bm, side='right'),
                          0, N_EXPERTS - 1).astype(I32)
    row_idx = jnp.concatenate([row_tok.reshape(nblk, bm), row_dst.reshape(nblk, bm)], axis=1)
    return blk_expert, row_idx, row_w.reshape(p, 1)


def _combine_kernel(y_ref, g_ref, m_ref, o_ref):
    d = y_ref.shape[1]
    o_ref[...] = y_ref[...] + g_ref[...] * (m_ref[:, :d] + m_ref[:, d:])


def _combine(y, mod, layer, moe_rows):
    nb, l, d = y.shape
    tm = _pick(l, (512, 256, 128))
    ni = l // tm
    m2 = moe_rows.reshape(moe_rows.shape[0] // TOP_K, TOP_K * d)
    return pl.pallas_call(
        _combine_kernel,
        out_shape=jax.ShapeDtypeStruct((nb, l, d), F32),
        grid=(nb, ni),
        in_specs=[
            pl.BlockSpec((None, tm, d), lambda b, i: (b, i, 0)),
            _mod_spec(d, layer, 5),
            pl.BlockSpec((tm, TOP_K * d), lambda b, i: (b * ni + i, 0)),
        ],
        out_specs=pl.BlockSpec((None, tm, d), lambda b, i: (b, i, 0)),
        compiler_params=_params(("arbitrary", "arbitrary")),
        name="moe_combine",
    )(y, mod, m2)


def kernel(x_prompt, x_sample, c, state_ret_fwd, state_ret_bwd, cache_attn_k, cache_attn_v, c_ctx,
           ada_w, ada_b, norm1_w, norm2_w, ret_w_in, ret_w_out, ret_gn_w, ret_decay_fwd, ret_decay_bwd,
           attn_w_qkv, attn_w_o, attn_q_norm, attn_k_norm, router_w, router_b,
           moe_w_gate, moe_w_up, moe_w_down):
    n_prompt, seq, d = x_prompt.shape
    ns, l, _ = x_sample.shape
    assert n_prompt * seq == l, "prompt tokens must fill exactly one sample-length row"
    depth = ada_w.shape[0]
    nb = ns + 1
    dk = d // RET_HEADS
    dv = 2 * dk
    dh = d // ATT_HEADS
    bm = 256 if (nb * l * TOP_K) % 256 == 0 and d >= 1024 else 128

    y = jnp.concatenate([x_sample, x_prompt.reshape(1, l, d)], axis=0)
    nbp = (nb + 7) // 8 * 8
    cond = jnp.zeros((nbp, d), F32).at[:ns].set(c).at[ns].set(c_ctx)
    mod = _ada_params(cond, ada_w, ada_b)

    perm = (jnp.arange(N_EXPERTS) % N_GROUPS) * EXPERTS_PER_GROUP + jnp.arange(N_EXPERTS) // N_GROUPS
    rw_t = router_w.T[perm]
    rw_hi = rw_t.astype(BF16)
    rw_lo = (rw_t - rw_hi.astype(F32)).astype(BF16)
    rb_perm = router_b[perm]

    ret_f, ret_b, k_new, v_new = [], [], [], []
    for i in range(depth):
        if i % 2 == 0:
            r = i // 2
            qkvg = _ret_inproj(y, mod, i, norm1_w[i], ret_w_in[r].astype(BF16), dk)
            o_s = _retention(qkvg, ns, 0, l, dk, ret_gn_w[r], ret_decay_fwd[r], ret_decay_bwd[r],
                             s0f=state_ret_fwd[:, r], s0b=state_ret_bwd[:, r])
            qkvg_p = qkvg.reshape(nb * n_prompt, seq, qkvg.shape[2])
            o_p, s_f, s_b = _retention(qkvg_p, n_prompt, ns * n_prompt, seq, dk, ret_gn_w[r],
                                       ret_decay_fwd[r], ret_decay_bwd[r], emit_final=True)
            o_s = o_s[0] if isinstance(o_s, (list, tuple)) else o_s
            ret_f.append(s_f)
            ret_b.append(s_b)
            w_o = ret_w_out[r].astype(BF16)
        else:
            a = i // 2
            qkv, kvf = _qkv_proj(y, ns, mod, i, norm1_w[i], attn_w_qkv[a].astype(BF16),
                                 attn_q_norm[a], attn_k_norm[a], dh)
            past = cache_attn_k.shape[2]
            o_s = _attention(qkv, ns, 0, l, dh,
                             cache_k=cache_attn_k[:, a].reshape(ns, past, ATT_KV_HEADS * dh),
                             cache_v=cache_attn_v[:, a].reshape(ns, past, ATT_KV_HEADS * dh))
            o_p = _attention(qkv.reshape(nb * n_prompt, seq, qkv.shape[2]), n_prompt, ns * n_prompt, seq, dh)
            nk = ATT_KV_HEADS * dh
            k_new.append(kvf[ns, :, :nk].reshape(n_prompt, seq, ATT_KV_HEADS, dh))
            v_new.append(kvf[ns, :, nk:].reshape(n_prompt, seq, ATT_KV_HEADS, dh))
            w_o = attn_w_o[a].astype(BF16)
        o_p = o_p.reshape(1, l, o_p.shape[-1])
        y, h2, logits = _outproj(o_s, o_p, w_o, y, mod, i, norm2_w[i], rw_hi, rw_lo)
        top_idx, top_w = _router(logits, rb_perm)
        blk_expert, row_idx, row_w = _dispatch(top_idx, top_w, bm)
        moe_rows = _moe_experts(h2.reshape(nb * l, d), blk_expert, row_idx, row_w,
                                moe_w_gate[i].astype(BF16), moe_w_up[i].astype(BF16),
                                moe_w_down[i].astype(BF16), bm)
        y = _combine(y, mod, i, moe_rows)

    y_p = y[ns].reshape(n_prompt, seq, d)
    y_s = y[:ns]
    return (y_p, y_s, jnp.stack(ret_f, axis=1), jnp.stack(ret_b, axis=1),
            jnp.stack(k_new, axis=1), jnp.stack(v_new, axis=1))
```

```python
import functools

import jax
import jax.numpy as jnp
from jax import lax
from jax.experimental import pallas as pl
from jax.experimental.pallas import tpu as pltpu

F32 = jnp.float32
BF16 = jnp.bfloat16
I32 = jnp.int32
U32 = jnp.uint32

EPS = 1e-6
GRID_W = 64
RET_HEADS = 4
ATT_HEADS = 8
ATT_KV_HEADS = 2
ROPE_THETA = 10000.0
N_EXPERTS = 32
N_GROUPS = 8
EXPERTS_PER_GROUP = N_EXPERTS // N_GROUPS
TOP_K = 2
_PAIR_A = (0, 0, 0, 1, 1, 3)
_PAIR_B = (1, 2, 3, 3, 2, 2)
N_PAIRS = len(_PAIR_A)
MOE_ROW_BLOCK = 256

V7X_VMEM_BYTES = 64 * 1024 * 1024
VMEM_LIMIT_BYTES = 56 * 1024 * 1024


def _params(sem):
    return pltpu.CompilerParams(dimension_semantics=sem, vmem_limit_bytes=VMEM_LIMIT_BYTES)


def _dot(a, b):
    return jnp.dot(a, b, preferred_element_type=F32)


def _dot_nt(a, b):
    return lax.dot_general(a, b, (((1,), (1,)), ((), ())), preferred_element_type=F32)


def _dot_tn(a, b):
    return lax.dot_general(a, b, (((0,), (0,)), ((), ())), preferred_element_type=F32)


def _pick(n, prefs):
    for p in prefs:
        if n % p == 0:
            return p
    return n


def _ada_kernel(c_ref, w_ref, b_ref, o_ref):
    c = c_ref[...]
    s = jax.nn.silu(c).astype(BF16)
    o_ref[...] = _dot(s, w_ref[...].astype(BF16)) + b_ref[...]


def _ada_params(cond, ada_w, ada_b):
    depth, d, n6 = ada_w.shape
    nbp = cond.shape[0]
    tn = _pick(n6, (1536, 1024, 512, 256, 128))
    out = pl.pallas_call(
        _ada_kernel,
        out_shape=jax.ShapeDtypeStruct((depth, nbp, n6), F32),
        grid=(depth, n6 // tn),
        in_specs=[
            pl.BlockSpec((nbp, d), lambda l, j: (0, 0)),
            pl.BlockSpec((None, d, tn), lambda l, j: (l, 0, j)),
            pl.BlockSpec((None, 1, tn), lambda l, j: (l, 0, j)),
        ],
        out_specs=pl.BlockSpec((None, nbp, tn), lambda l, j: (l, 0, j)),
        compiler_params=_params(("arbitrary", "arbitrary")),
        name="ada_params",
    )(cond, ada_w, ada_b.reshape(depth, 1, n6))
    return out.reshape(depth, nbp, 6, 1, d).transpose(0, 2, 1, 3, 4)


def _mod_spec(d, layer, j):
    return pl.BlockSpec((None, None, None, 1, d), lambda b, i: (layer, j, b, 0, 0))


def _norm_mod(x, nw, sh, sc):
    ms = jnp.mean(x * x, axis=-1, keepdims=True)
    h = x * lax.rsqrt(ms + EPS) * nw
    return h * (1.0 + sc) + sh


def _ret_inproj_kernel(x_ref, nw_ref, sh_ref, sc_ref, w_ref, o_ref, *, hq, hv, tn, k_scale):
    hb = _norm_mod(x_ref[...], nw_ref[...], sh_ref[...], sc_ref[...]).astype(BF16)
    n = w_ref.shape[1]
    for j in range(n // tn):
        lo = j * tn
        acc = _dot(hb, w_ref[:, lo:lo + tn])
        if hq <= lo < 2 * hq:
            acc = acc * k_scale
        elif lo >= 2 * hq + hv:
            acc = jax.nn.silu(acc)
        o_ref[:, lo:lo + tn] = acc.astype(BF16)


def _ret_inproj(y, mod, layer, norm_w, w_in_bf16, dk):
    nb, l, d = y.shape
    n = w_in_bf16.shape[1]
    hq = RET_HEADS * dk
    hv = 2 * hq
    tm = _pick(l, (512, 256, 128))
    tn = _pick(hq, (1024, 512, 256, 128))
    kern = functools.partial(_ret_inproj_kernel, hq=hq, hv=hv, tn=tn, k_scale=float(dk) ** -0.5)
    return pl.pallas_call(
        kern,
        out_shape=jax.ShapeDtypeStruct((nb, l, n), BF16),
        grid=(nb, l // tm),
        in_specs=[
            pl.BlockSpec((None, tm, d), lambda b, i: (b, i, 0)),
            pl.BlockSpec((1, d), lambda b, i: (0, 0)),
            _mod_spec(d, layer, 0),
            _mod_spec(d, layer, 1),
            pl.BlockSpec((d, n), lambda b, i: (0, 0), pipeline_mode=pl.Buffered(1)),
        ],
        out_specs=pl.BlockSpec((None, tm, n), lambda b, i: (b, i, 0)),
        compiler_params=_params(("arbitrary", "arbitrary")),
        name="ret_inproj",
    )(y, norm_w.reshape(1, d), mod, mod, w_in_bf16)


def _log_sigmoid(x):
    return jnp.minimum(x, 0.0) - jnp.log1p(jnp.exp(-jnp.abs(x)))


def _ret_kernel(*refs, ls, c, has_init, emit_final):
    it = iter(refs)
    q_ref, k_ref, v_ref, gf_ref, gb_ref, gn_ref, df_ref, db_ref = (next(it) for _ in range(8))
    if has_init:
        s0f_ref, s0b_ref = next(it), next(it)
    y_ref = next(it)
    if emit_final:
        sf_out, sb_out = next(it), next(it)
    s_scr, sb_scr = next(it), next(it)

    dk = q_ref.shape[1]
    nchunks = ls // c
    lg_f = _log_sigmoid(df_ref[...])[:, 0:1]
    lg_b = _log_sigmoid(db_ref[...])[:, 0:1]

    ri = lax.broadcasted_iota(I32, (c, c), 0)
    ci = lax.broadcasted_iota(I32, (c, c), 1)
    dif = (ri - ci).astype(F32)
    decay_f = jnp.where(dif >= 0, jnp.exp(jnp.maximum(dif, 0.0) * lg_f), 0.0)
    decay_b = jnp.where(dif <= 0, jnp.exp(jnp.maximum(-dif, 0.0) * lg_b), 0.0)
    pos = lax.broadcasted_iota(I32, (c, dk), 0).astype(F32)
    qdec_f = jnp.exp((pos + 1.0) * lg_f)
    kdec_f = jnp.exp((c - 1.0 - pos) * lg_f)
    qdec_b = jnp.exp((c - pos) * lg_b)
    kdec_b = jnp.exp(pos * lg_b)
    cdec_f = jnp.exp(c * lg_f)
    cdec_b = jnp.exp(c * lg_b)
    gn = gn_ref[...]

    if has_init:
        s_scr[...] = s0b_ref[...]
    else:
        s_scr[...] = jnp.zeros_like(s_scr)

    def bwd_body(t, carry):
        n = nchunks - 1 - t
        r0 = pl.multiple_of(n * c, c)
        s = s_scr[...]
        sb_scr[n] = s.astype(BF16)
        kc = k_ref[pl.ds(r0, c), :].astype(F32)
        vc = v_ref[pl.ds(r0, c), :]
        s_scr[...] = s * cdec_b + _dot_tn((kc * kdec_b).astype(BF16), vc)
        return carry

    lax.fori_loop(0, nchunks, bwd_body, 0)
    if emit_final:
        sb_out[...] = s_scr[...]

    if has_init:
        s_scr[...] = s0f_ref[...]
    else:
        s_scr[...] = jnp.zeros_like(s_scr)

    def head_norm(o):
        mu = jnp.mean(o, axis=-1, keepdims=True)
        dlt = o - mu
        var = jnp.mean(dlt * dlt, axis=-1, keepdims=True)
        return dlt * lax.rsqrt(var + EPS) * gn

    def fwd_body(n, carry):
        r0 = pl.multiple_of(n * c, c)
        rows = pl.ds(r0, c)
        qc = q_ref[rows, :]
        kc = k_ref[rows, :]
        vc = v_ref[rows, :]
        qf = qc.astype(F32)
        sc = _dot_nt(qc, kc)
        s = s_scr[...]
        o_f = _dot((sc * decay_f).astype(BF16), vc) + _dot((qf * qdec_f).astype(BF16), s.astype(BF16))
        o_b = _dot((sc * decay_b).astype(BF16), vc) + _dot((qf * qdec_b).astype(BF16), sb_scr[n])
        s_scr[...] = s * cdec_f + _dot_tn((kc.astype(F32) * kdec_f).astype(BF16), vc)
        y = head_norm(o_f) * gf_ref[rows, :].astype(F32) + head_norm(o_b) * gb_ref[rows, :].astype(F32)
        y_ref[rows, :] = y.astype(BF16)
        return carry

    lax.fori_loop(0, nchunks, fwd_body, 0)
    if emit_final:
        sf_out[...] = s_scr[...]


def _retention(qkvg, nseq, seq0, ls, dk, gn_w, decay_f, decay_b, s0f=None, s0b=None, emit_final=False):
    h = RET_HEADS
    dv = 2 * dk
    c = _pick(ls, (128,))
    has_init = s0f is not None
    kq, kk, kv, kgf, kgb = 0, h, h, 2 * h, 3 * h
    in_specs = [
        pl.BlockSpec((None, ls, dk), lambda s, hh: (s + seq0, 0, kq + hh)),
        pl.BlockSpec((None, ls, dk), lambda s, hh: (s + seq0, 0, kk + hh)),
        pl.BlockSpec((None, ls, dv), lambda s, hh: (s + seq0, 0, kv + hh)),
        pl.BlockSpec((None, ls, dv), lambda s, hh: (s + seq0, 0, kgf + hh)),
        pl.BlockSpec((None, ls, dv), lambda s, hh: (s + seq0, 0, kgb + hh)),
        pl.BlockSpec((None, 1, dv), lambda s, hh: (hh, 0, 0)),
        pl.BlockSpec((None, 1, 128), lambda s, hh: (hh, 0, 0)),
        pl.BlockSpec((None, 1, 128), lambda s, hh: (hh, 0, 0)),
    ]
    args = [qkvg, qkvg, qkvg, qkvg, qkvg, gn_w.reshape(h, 1, dv),
            jnp.broadcast_to(decay_f.reshape(h, 1, 1), (h, 1, 128)),
            jnp.broadcast_to(decay_b.reshape(h, 1, 1), (h, 1, 128))]
    if has_init:
        in_specs += [pl.BlockSpec((None, None, dk, dv), lambda s, hh: (s, hh, 0, 0))] * 2
        args += [s0f, s0b]
    out_shape = [jax.ShapeDtypeStruct((nseq, ls, h * dv), BF16)]
    out_specs = [pl.BlockSpec((None, ls, dv), lambda s, hh: (s, 0, hh))]
    if emit_final:
        out_shape += [jax.ShapeDtypeStruct((nseq, h, dk, dv), F32)] * 2
        out_specs += [pl.BlockSpec((None, None, dk, dv), lambda s, hh: (s, hh, 0, 0))] * 2
    kern = functools.partial(_ret_kernel, ls=ls, c=c, has_init=has_init, emit_final=emit_final)
    return pl.pallas_call(
        kern,
        out_shape=out_shape,
        grid=(nseq, h),
        in_specs=in_specs,
        out_specs=out_specs,
        scratch_shapes=[pltpu.VMEM((dk, dv), F32), pltpu.VMEM((ls // c, dk, dv), BF16)],
        compiler_params=_params(("arbitrary", "arbitrary")),
        name="retention_init" if has_init else "retention_zero",
    )(*args)


def _qkv_kernel(x_ref, nw_ref, sh_ref, sc_ref, w_ref, qn_ref, kn_ref, cos_ref, sa_ref, sb_ref,
                o_ref, kv_ref, *, dh):
    hb = _norm_mod(x_ref[...], nw_ref[...], sh_ref[...], sc_ref[...]).astype(BF16)
    nq = ATT_HEADS * dh
    nk = ATT_KV_HEADS * dh
    cos, sa, sb = cos_ref[...], sa_ref[...], sb_ref[...]

    def norm_rope(a, w):
        a = a * lax.rsqrt(jnp.mean(a * a, axis=-1, keepdims=True) + EPS) * w
        roped = a * cos + pltpu.roll(a, dh - dh // 4, 1) * sa + pltpu.roll(a, dh // 4, 1) * sb
        return a, roped

    q = _dot(hb, w_ref[:, :nq])
    for hh in range(ATT_HEADS):
        _, r = norm_rope(q[:, hh * dh:(hh + 1) * dh], qn_ref[...])
        o_ref[:, hh * dh:(hh + 1) * dh] = r.astype(BF16)
    kvv = _dot(hb, w_ref[:, nq:])
    for hh in range(ATT_KV_HEADS):
        a, r = norm_rope(kvv[:, hh * dh:(hh + 1) * dh], kn_ref[...])
        o_ref[:, nq + hh * dh:nq + (hh + 1) * dh] = r.astype(BF16)
        kv_ref[:, hh * dh:(hh + 1) * dh] = a
    o_ref[:, nq + nk:] = kvv[:, nk:].astype(BF16)
    kv_ref[:, nk:] = kvv[:, nk:]


def _rope_tables(l, dh, rotate):
    axis_dim = dh // 2
    half = axis_dim // 2
    t = jnp.arange(l)
    row = (t // GRID_W).astype(F32)
    col = (t % GRID_W).astype(F32)
    inv = ROPE_THETA ** (-jnp.arange(0, axis_dim, 2, dtype=F32) / axis_dim)
    d = jnp.arange(dh)
    pos = jnp.where((d // axis_dim)[None, :] == 0, row[:, None], col[:, None])
    ang = pos * inv[d % half][None, :]
    first = ((d % axis_dim) < half)[None, :]
    cos, sin = jnp.cos(ang), jnp.sin(ang)
    if not rotate:
        cos, sin = jnp.ones_like(cos), jnp.zeros_like(sin)
    return cos, jnp.where(first, -sin, 0.0), jnp.where(first, 0.0, sin)


def _qkv_proj(y, ns, mod, layer, norm_w, w_bf16, qn, kn, dh):
    nb, l, d = y.shape
    n = w_bf16.shape[1]
    nkv = 2 * ATT_KV_HEADS * dh
    tm = _pick(l, (512, 256, 128))
    tabs = [jnp.stack([a, b]) for a, b in zip(_rope_tables(l, dh, True), _rope_tables(l, dh, False))]
    tab_spec = pl.BlockSpec((None, tm, dh), lambda b, i: (jnp.where(b < ns, 0, 1), i, 0))
    return pl.pallas_call(
        functools.partial(_qkv_kernel, dh=dh),
        out_shape=[jax.ShapeDtypeStruct((nb, l, n), BF16), jax.ShapeDtypeStruct((nb, l, nkv), F32)],
        grid=(nb, l // tm),
        in_specs=[
            pl.BlockSpec((None, tm, d), lambda b, i: (b, i, 0)),
            pl.BlockSpec((1, d), lambda b, i: (0, 0)),
            _mod_spec(d, layer, 0),
            _mod_spec(d, layer, 1),
            pl.BlockSpec((d, n), lambda b, i: (0, 0), pipeline_mode=pl.Buffered(1)),
            pl.BlockSpec((1, dh), lambda b, i: (0, 0)),
            pl.BlockSpec((1, dh), lambda b, i: (0, 0)),
            tab_spec, tab_spec, tab_spec,
        ],
        out_specs=[pl.BlockSpec((None, tm, n), lambda b, i: (b, i, 0)),
                   pl.BlockSpec((None, tm, nkv), lambda b, i: (b, i, 0))],
        compiler_params=_params(("arbitrary", "arbitrary")),
        name="qkv_proj",
    )(y, norm_w.reshape(1, d), mod, mod, w_bf16, qn.reshape(1, dh), kn.reshape(1, dh), *tabs)


def _attn_kernel(*refs, dh, group, has_cache, scale):
    if has_cache:
        q_ref, k_ref, v_ref, kc_ref, vc_ref, o_ref = refs
        kc = kc_ref[...].astype(BF16)
        vc = vc_ref[...].astype(BF16)
    else:
        q_ref, k_ref, v_ref, o_ref = refs
    k = k_ref[...]
    v = v_ref[...]
    for g in range(group):
        q = q_ref[:, g * dh:(g + 1) * dh]
        s1 = _dot_nt(q, k)
        m = jnp.max(s1, axis=-1, keepdims=True)
        if has_cache:
            s0 = _dot_nt(q, kc)
            m = jnp.maximum(m, jnp.max(s0, axis=-1, keepdims=True))
        p1 = jnp.exp((s1 - m) * scale)
        den = jnp.sum(p1, axis=-1, keepdims=True)
        o = _dot(p1.astype(BF16), v)
        if has_cache:
            p0 = jnp.exp((s0 - m) * scale)
            den = den + jnp.sum(p0, axis=-1, keepdims=True)
            o = o + _dot(p0.astype(BF16), vc)
        o_ref[:, g * dh:(g + 1) * dh] = (o / den).astype(BF16)


def _attention(qkv, nseq, seq0, ls, dh, cache_k=None, cache_v=None):
    group = ATT_HEADS // ATT_KV_HEADS
    has_cache = cache_k is not None
    tq = _pick(ls, (256, 128))
    kcol = ATT_HEADS
    vcol = ATT_HEADS + ATT_KV_HEADS
    in_specs = [
        pl.BlockSpec((None, tq, group * dh), lambda s, kh, i: (s + seq0, i, kh)),
        pl.BlockSpec((None, ls, dh), lambda s, kh, i: (s + seq0, 0, kcol + kh)),
        pl.BlockSpec((None, ls, dh), lambda s, kh, i: (s + seq0, 0, vcol + kh)),
    ]
    args = [qkv, qkv, qkv]
    if has_cache:
        past = cache_k.shape[1]
        in_specs += [pl.BlockSpec((None, past, dh), lambda s, kh, i: (s, 0, kh))] * 2
        args += [cache_k, cache_v]
    kern = functools.partial(_attn_kernel, dh=dh, group=group, has_cache=has_cache, scale=float(dh) ** -0.5)
    return pl.pallas_call(
        kern,
        out_shape=jax.ShapeDtypeStruct((nseq, ls, ATT_HEADS * dh), BF16),
        grid=(nseq, ATT_KV_HEADS, ls // tq),
        in_specs=in_specs,
        out_specs=pl.BlockSpec((None, tq, group * dh), lambda s, kh, i: (s, i, kh)),
        compiler_params=_params(("arbitrary", "arbitrary", "arbitrary")),
        name="attention_cache" if has_cache else "attention_self",
    )(*args)


def _pack_rows(h):
    half = h.shape[1] // 2
    u = lax.bitcast_convert_type(h.astype(BF16).astype(F32), U32)
    return (u[:, :half] >> 16) | (u[:, half:] & jnp.uint32(0xFFFF0000))


def _unpack_rows(p):
    lo = lax.bitcast_convert_type(p << 16, F32)
    hi = lax.bitcast_convert_type(p & jnp.uint32(0xFFFF0000), F32)
    return jnp.concatenate([lo, hi], axis=1).astype(BF16)


def _outproj_kernel(os_ref, op_ref, w_ref, y_ref, g1_ref, nw_ref, sh_ref, sc_ref, rwh_ref, rwl_ref,
                    yo_ref, h_ref, lg_ref, *, ns):
    is_prompt = pl.program_id(0) >= ns
    o = jnp.where(is_prompt, op_ref[...], os_ref[...])
    y = y_ref[...] + g1_ref[...] * _dot(o, w_ref[...])
    yo_ref[...] = y
    h = _norm_mod(y, nw_ref[...], sh_ref[...], sc_ref[...])
    h_ref[...] = _pack_rows(h)
    h_hi = h.astype(BF16)
    h_lo = (h - h_hi.astype(F32)).astype(BF16)
    rwh = rwh_ref[...]
    lg_ref[...] = _dot_nt(rwh, h_hi) + _dot_nt(rwl_ref[...], h_hi) + _dot_nt(rwh, h_lo)


def _outproj(o_s, o_p, w_bf16, y, mod, layer, norm_w, rw_hi, rw_lo):
    nb, l, d = y.shape
    ns = o_s.shape[0]
    kdim = o_s.shape[2]
    ne = rw_hi.shape[0]
    tm = _pick(l, (512, 256, 128))
    ni = l // tm
    dp = jax.eval_shape(_pack_rows, jax.ShapeDtypeStruct((8, d), F32)).shape[1]
    return pl.pallas_call(
        functools.partial(_outproj_kernel, ns=ns),
        out_shape=[jax.ShapeDtypeStruct((nb, l, d), F32), jax.ShapeDtypeStruct((nb, l, dp), U32),
                   jax.ShapeDtypeStruct((nb, ne, l), F32)],
        grid=(nb, ni),
        in_specs=[
            pl.BlockSpec((None, tm, kdim), lambda b, i: (jnp.minimum(b, ns - 1), jnp.where(b < ns, i, ni - 1), 0)),
            pl.BlockSpec((None, tm, kdim), lambda b, i: (0, jnp.where(b < ns, 0, i), 0)),
            pl.BlockSpec((kdim, d), lambda b, i: (0, 0), pipeline_mode=pl.Buffered(1)),
            pl.BlockSpec((None, tm, d), lambda b, i: (b, i, 0)),
            _mod_spec(d, layer, 2),
            pl.BlockSpec((1, d), lambda b, i: (0, 0)),
            _mod_spec(d, layer, 3),
            _mod_spec(d, layer, 4),
            pl.BlockSpec((ne, d), lambda b, i: (0, 0)),
            pl.BlockSpec((ne, d), lambda b, i: (0, 0)),
        ],
        out_specs=[pl.BlockSpec((None, tm, d), lambda b, i: (b, i, 0)),
                   pl.BlockSpec((None, tm, dp), lambda b, i: (b, i, 0)),
                   pl.BlockSpec((None, ne, tm), lambda b, i: (b, 0, i))],
        compiler_params=_params(("arbitrary", "arbitrary")),
        name="outproj_router",
    )(o_s, o_p, w_bf16, y, mod, norm_w.reshape(1, d), mod, mod, rw_hi, rw_lo)


def _router_kernel(lg_ref, b_ref, bkt_ref, rank_ref, wt_ref, cnt_ref, carry_ref):
    ng = N_GROUPS
    sc = jax.nn.sigmoid(lg_ref[...])
    sel = sc + b_ref[...]
    sj = [sel[j * ng:(j + 1) * ng] for j in range(EXPERTS_PER_GROUP)]
    pj = [sc[j * ng:(j + 1) * ng] for j in range(EXPERTS_PER_GROUP)]
    hi1, lo1 = jnp.maximum(sj[0], sj[1]), jnp.minimum(sj[0], sj[1])
    hi2, lo2 = jnp.maximum(sj[2], sj[3]), jnp.minimum(sj[2], sj[3])
    top1 = jnp.maximum(hi1, hi2)
    top2 = jnp.maximum(jnp.minimum(hi1, hi2), jnp.maximum(lo1, lo2))
    gs = top1 + top2
    gmax = jnp.max(gs, axis=0, keepdims=True)
    gi = lax.broadcasted_iota(I32, gs.shape, 0)
    best = jnp.min(jnp.where(gs == gmax, gi, ng), axis=0, keepdims=True)
    onehot = gi == best
    neg = -jnp.inf
    vj = [jnp.max(jnp.where(onehot, a, neg), axis=0, keepdims=True) for a in sj]
    wj = [jnp.max(jnp.where(onehot, a, neg), axis=0, keepdims=True) for a in pj]

    def argmax4(vals):
        m, i = vals[0], jnp.zeros(vals[0].shape, I32)
        for j in range(1, EXPERTS_PER_GROUP):
            gt = vals[j] > m
            m = jnp.where(gt, vals[j], m)
            i = jnp.where(gt, j, i)
        return i

    i1 = argmax4(vj)
    i2 = argmax4([jnp.where(i1 == j, neg, vj[j]) for j in range(EXPERTS_PER_GROUP)])

    def take(vals, i):
        out = vals[0]
        for j in range(1, EXPERTS_PER_GROUP):
            out = jnp.where(i == j, vals[j], out)
        return out

    w1, w2 = take(wj, i1), take(wj, i2)
    tot = w1 + w2
    w1, w2 = w1 / tot, w2 / tot

    lo, hi = jnp.minimum(i1, i2), jnp.maximum(i1, i2)
    pair = jnp.where(lo == 0, hi - 1, jnp.where(lo == 1, jnp.where(hi == 3, 3, 4), 5))
    slot_a = jnp.where(pair < 3, 0, jnp.where(pair < 5, 1, 3))
    bucket = best * N_PAIRS + pair
    a_first = i1 == slot_a
    w_a = jnp.where(a_first, w1, w2)
    w_b = jnp.where(a_first, w2, w1)

    tl = bucket.shape[1]
    nbk = carry_ref.shape[0]

    @pl.when(jnp.logical_and(pl.program_id(0) == 0, pl.program_id(1) == 0))
    def _():
        carry_ref[...] = jnp.zeros_like(carry_ref)

    onehot_b = lax.broadcasted_iota(I32, (nbk, tl), 0) == bucket
    upper = (lax.broadcasted_iota(I32, (tl, tl), 0) <= lax.broadcasted_iota(I32, (tl, tl), 1))
    prefix = _dot(onehot_b.astype(BF16), upper.astype(BF16))
    carry = carry_ref[...]
    rank = jnp.sum(jnp.where(onehot_b, prefix - 1.0 + carry, 0.0), axis=0, keepdims=True)
    carry = carry + prefix[:, tl - 1:tl]
    carry_ref[...] = carry
    bkt_ref[...] = bucket
    rank_ref[...] = rank.astype(I32)
    cnt_ref[...] = jnp.broadcast_to(carry, cnt_ref.shape)
    rows = lax.broadcasted_iota(I32, (wt_ref.shape[1], tl), 0)
    wt_ref[...] = jnp.where(rows == 0, w_a, jnp.where(rows == 1, w_b, 0.0)).T


def _router(logits, bias_perm):
    nb, ne, l = logits.shape
    tl = _pick(l, (512, 256, 128))
    nbk = N_GROUPS * N_PAIRS
    return pl.pallas_call(
        _router_kernel,
        out_shape=[jax.ShapeDtypeStruct((nb, 1, l), I32), jax.ShapeDtypeStruct((nb, 1, l), I32),
                   jax.ShapeDtypeStruct((nb, l, 128), F32), jax.ShapeDtypeStruct((nbk, 128), F32)],
        grid=(nb, l // tl),
        in_specs=[pl.BlockSpec((None, ne, tl), lambda b, i: (b, 0, i)),
                  pl.BlockSpec((ne, 1), lambda b, i: (0, 0))],
        out_specs=[pl.BlockSpec((None, 1, tl), lambda b, i: (b, 0, i)),
                   pl.BlockSpec((None, 1, tl), lambda b, i: (b, 0, i)),
                   pl.BlockSpec((None, tl, 128), lambda b, i: (b, i, 0)),
                   pl.BlockSpec((nbk, 128), lambda b, i: (0, 0))],
        scratch_shapes=[pltpu.VMEM((nbk, 1), F32)],
        compiler_params=_params(("arbitrary", "arbitrary")),
        name="router_rank",
    )(logits, bias_perm.reshape(ne, 1))


def _moe_kernel(ea_ref, eb_ref, nu_ref, x_ref, wga_ref, wua_ref, wda_ref, wgb_ref, wub_ref, wdb_ref, o_ref):
    del ea_ref, eb_ref
    i = pl.program_id(0)
    d = o_ref.shape[1] // 2

    @pl.when(i < nu_ref[0])
    def _():
        x = _unpack_rows(x_ref[...])

        def expert(wg, wu, wd):
            hid = jax.nn.silu(_dot(x, wg[...])) * _dot(x, wu[...])
            return _dot(hid.astype(BF16), wd[...])

        o_ref[:, :d] = expert(wga_ref, wua_ref, wda_ref)
        o_ref[:, d:] = expert(wgb_ref, wub_ref, wdb_ref)

    @pl.when(i >= nu_ref[0])
    def _():
        o_ref[...] = jnp.zeros_like(o_ref)


def _moe_experts(xs, blk_a, blk_b, n_used, wg, wu, wd, bm):
    p, dp = xs.shape
    _, d, de = wg.shape
    nblk = p // bm
    w_in = lambda sel: pl.BlockSpec((None, d, de), lambda i, ea, eb, nu: ((ea, eb)[sel][i], 0, 0))
    w_out = lambda sel: pl.BlockSpec((None, de, d), lambda i, ea, eb, nu: ((ea, eb)[sel][i], 0, 0))
    grid_spec = pltpu.PrefetchScalarGridSpec(
        num_scalar_prefetch=3,
        grid=(nblk,),
        in_specs=[pl.BlockSpec((bm, dp), lambda i, ea, eb, nu: (i, 0)),
                  w_in(0), w_in(0), w_out(0), w_in(1), w_in(1), w_out(1)],
        out_specs=pl.BlockSpec((bm, 2 * d), lambda i, ea, eb, nu: (i, 0)),
    )
    return pl.pallas_call(
        _moe_kernel,
        out_shape=jax.ShapeDtypeStruct((p, 2 * d), F32),
        grid_spec=grid_spec,
        compiler_params=_params(("arbitrary",)),
        name="moe_experts",
    )(blk_a, blk_b, n_used, xs, wg, wu, wd, wg, wu, wd)


def _row_plan(bucket, rank, counts, bm):
    t = bucket.size
    nbk = N_GROUPS * N_PAIRS
    cnt = counts[:, 0].astype(I32)
    padded = (cnt + bm - 1) // bm * bm
    pends = jnp.cumsum(padded)
    pstarts = pends - padded
    onehot = bucket.reshape(t, 1) == jnp.arange(nbk, dtype=I32)[None, :]
    dest = jnp.sum(jnp.where(onehot, pstarts[None, :], 0), axis=1) + rank.reshape(t)
    p = (t + bm - 1) // bm * bm + nbk * bm
    nblk = p // bm
    blk_bucket = jnp.clip(jnp.searchsorted(pends, jnp.arange(nblk, dtype=I32) * bm, side='right'), 0, nbk - 1)
    grp, pair = blk_bucket // N_PAIRS, blk_bucket % N_PAIRS
    blk_a = grp * EXPERTS_PER_GROUP + jnp.asarray(_PAIR_A, I32)[pair]
    blk_b = grp * EXPERTS_PER_GROUP + jnp.asarray(_PAIR_B, I32)[pair]
    n_used = (pends[-1:] // bm).astype(I32)
    return dest.astype(I32), blk_a.astype(I32), blk_b.astype(I32), n_used, p


def _scatter_kernel(dest_ref, src_hbm, init_hbm, out_hbm, sem, *, rows):
    del init_hbm
    base = pl.program_id(0) * rows

    def issue(r, carry):
        t = base + r
        pltpu.make_async_copy(src_hbm.at[t], out_hbm.at[dest_ref[t]], sem).start()
        return carry

    lax.fori_loop(0, rows, issue, 0, unroll=8)
    pltpu.make_async_copy(src_hbm.at[pl.ds(0, rows)], out_hbm.at[pl.ds(0, rows)], sem).wait()


def _scatter_rows(h_rows, dest, p):
    t, dp = h_rows.shape
    rows = _pick(t, (4096, 2048, 1024, 512))
    grid_spec = pltpu.PrefetchScalarGridSpec(
        num_scalar_prefetch=1,
        grid=(t // rows,),
        in_specs=[pl.BlockSpec(memory_space=pl.ANY), pl.BlockSpec(memory_space=pl.ANY)],
        out_specs=pl.BlockSpec(memory_space=pl.ANY),
        scratch_shapes=[pltpu.SemaphoreType.DMA],
    )
    return pl.pallas_call(
        functools.partial(_scatter_kernel, rows=rows),
        out_shape=jax.ShapeDtypeStruct((p, dp), h_rows.dtype),
        grid_spec=grid_spec,
        input_output_aliases={2: 0},
        compiler_params=_params(("arbitrary",)),
        name="moe_scatter_rows",
    )(dest, h_rows, jnp.zeros((p, dp), h_rows.dtype))


def _combine_kernel(dest_ref, y_ref, g_ref, wt_ref, m_hbm, o_ref, mbuf, sem, *, tm, nsteps):
    step = pl.program_id(0) * pl.num_programs(1) + pl.program_id(1)
    slot = step % 2
    d = y_ref.shape[1]

    def gather(s, sl):
        def issue(r, carry):
            pltpu.make_async_copy(m_hbm.at[dest_ref[s * tm + r]], mbuf.at[sl, r], sem.at[sl]).start()
            return carry
        lax.fori_loop(0, tm, issue, 0, unroll=8)

    @pl.when(step == 0)
    def _():
        gather(0, 0)

    @pl.when(step + 1 < nsteps)
    def _():
        gather(step + 1, 1 - slot)

    pltpu.make_async_copy(m_hbm.at[pl.ds(0, tm)], mbuf.at[slot], sem.at[slot]).wait()
    m = mbuf[slot]
    w = wt_ref[...]
    o_ref[...] = y_ref[...] + g_ref[...] * (w[:, 0:1] * m[:, :d] + w[:, 1:2] * m[:, d:])


def _combine(y, mod, layer, moe_rows, dest, wt):
    nb, l, d = y.shape
    tm = _pick(l, (512, 256, 128))
    ni = l // tm
    grid_spec = pltpu.PrefetchScalarGridSpec(
        num_scalar_prefetch=1,
        grid=(nb, ni),
        in_specs=[
            pl.BlockSpec((None, tm, d), lambda b, i, dst: (b, i, 0)),
            pl.BlockSpec((None, None, None, 1, d), lambda b, i, dst: (layer, 5, b, 0, 0)),
            pl.BlockSpec((None, tm, 128), lambda b, i, dst: (b, i, 0)),
            pl.BlockSpec(memory_space=pl.ANY),
        ],
        out_specs=pl.BlockSpec((None, tm, d), lambda b, i, dst: (b, i, 0)),
        scratch_shapes=[pltpu.VMEM((2, tm, 2 * d), F32), pltpu.SemaphoreType.DMA((2,))],
    )
    return pl.pallas_call(
        functools.partial(_combine_kernel, tm=tm, nsteps=nb * ni),
        out_shape=jax.ShapeDtypeStruct((nb, l, d), F32),
        grid_spec=grid_spec,
        compiler_params=_params(("arbitrary", "arbitrary")),
        name="moe_combine",
    )(dest, y, mod, wt, moe_rows)


def kernel(x_prompt, x_sample, c, state_ret_fwd, state_ret_bwd, cache_attn_k, cache_attn_v, c_ctx,
           ada_w, ada_b, norm1_w, norm2_w, ret_w_in, ret_w_out, ret_gn_w, ret_decay_fwd, ret_decay_bwd,
           attn_w_qkv, attn_w_o, attn_q_norm, attn_k_norm, router_w, router_b,
           moe_w_gate, moe_w_up, moe_w_down):
    n_prompt, seq, d = x_prompt.shape
    ns, l, _ = x_sample.shape
    assert n_prompt * seq == l, "prompt tokens must fill exactly one sample-length row"
    depth = ada_w.shape[0]
    nb = ns + 1
    dk = d // RET_HEADS
    dv = 2 * dk
    dh = d // ATT_HEADS
    bm = MOE_ROW_BLOCK

    y = jnp.concatenate([x_sample, x_prompt.reshape(1, l, d)], axis=0)
    nbp = (nb + 7) // 8 * 8
    cond = jnp.zeros((nbp, d), F32).at[:ns].set(c).at[ns].set(c_ctx)
    mod = _ada_params(cond, ada_w, ada_b)

    perm = (jnp.arange(N_EXPERTS) % N_GROUPS) * EXPERTS_PER_GROUP + jnp.arange(N_EXPERTS) // N_GROUPS
    rw_t = router_w.T[perm]
    rw_hi = rw_t.astype(BF16)
    rw_lo = (rw_t - rw_hi.astype(F32)).astype(BF16)
    rb_perm = router_b[perm]

    ret_f, ret_b, k_new, v_new = [], [], [], []
    for i in range(depth):
        if i % 2 == 0:
            r = i // 2
            qkvg = _ret_inproj(y, mod, i, norm1_w[i], ret_w_in[r].astype(BF16), dk)
            o_s = _retention(qkvg, ns, 0, l, dk, ret_gn_w[r], ret_decay_fwd[r], ret_decay_bwd[r],
                             s0f=state_ret_fwd[:, r], s0b=state_ret_bwd[:, r])
            qkvg_p = qkvg.reshape(nb * n_prompt, seq, qkvg.shape[2])
            o_p, s_f, s_b = _retention(qkvg_p, n_prompt, ns * n_prompt, seq, dk, ret_gn_w[r],
                                       ret_decay_fwd[r], ret_decay_bwd[r], emit_final=True)
            o_s = o_s[0] if isinstance(o_s, (list, tuple)) else o_s
            ret_f.append(s_f)
            ret_b.append(s_b)
            w_o = ret_w_out[r].astype(BF16)
        else:
            a = i // 2
            qkv, kvf = _qkv_proj(y, ns, mod, i, norm1_w[i], attn_w_qkv[a].astype(BF16),
                                 attn_q_norm[a], attn_k_norm[a], dh)
            past = cache_attn_k.shape[2]
            o_s = _attention(qkv, ns, 0, l, dh,
                             cache_k=cache_attn_k[:, a].reshape(ns, past, ATT_KV_HEADS * dh),
                             cache_v=cache_attn_v[:, a].reshape(ns, past, ATT_KV_HEADS * dh))
            o_p = _attention(qkv.reshape(nb * n_prompt, seq, qkv.shape[2]), n_prompt, ns * n_prompt, seq, dh)
            nk = ATT_KV_HEADS * dh
            k_new.append(kvf[ns, :, :nk].reshape(n_prompt, seq, ATT_KV_HEADS, dh))
            v_new.append(kvf[ns, :, nk:].reshape(n_prompt, seq, ATT_KV_HEADS, dh))
            w_o = attn_w_o[a].astype(BF16)
        o_p = o_p.reshape(1, l, o_p.shape[-1])
        y, h2, logits = _outproj(o_s, o_p, w_o, y, mod, i, norm2_w[i], rw_hi, rw_lo)
        bucket, rank, wt, counts = _router(logits, rb_perm)
        dest, blk_a, blk_b, n_used, p = _row_plan(bucket, rank, counts, bm)
        xs = _scatter_rows(h2.reshape(nb * l, h2.shape[2]), dest, p)
        moe_rows = _moe_experts(xs, blk_a, blk_b, n_used, moe_w_gate[i].astype(BF16),
                                moe_w_up[i].astype(BF16), moe_w_down[i].astype(BF16), bm)
        y = _combine(y, mod, i, moe_rows, dest, wt)

    y_p = y[ns].reshape(n_prompt, seq, d)
    y_s = y[:ns]
    return (y_p, y_s, jnp.stack(ret_f, axis=1), jnp.stack(ret_b, axis=1),
            jnp.stack(k_new, axis=1), jnp.stack(v_new, axis=1))
```

```python
import functools

import jax
import jax.numpy as jnp
from jax import lax
from jax.experimental import pallas as pl
from jax.experimental.pallas import tpu as pltpu

F32 = jnp.float32
BF16 = jnp.bfloat16
I32 = jnp.int32
U32 = jnp.uint32

EPS = 1e-6
GRID_W = 64
RET_HEADS = 4
ATT_HEADS = 8
ATT_KV_HEADS = 2
ROPE_THETA = 10000.0
N_EXPERTS = 32
N_GROUPS = 8
EXPERTS_PER_GROUP = N_EXPERTS // N_GROUPS
TOP_K = 2
_PAIR_A = (0, 0, 0, 1, 1, 3)
_PAIR_B = (1, 2, 3, 3, 2, 2)
N_PAIRS = len(_PAIR_A)
MOE_ROW_BLOCK = 256

V7X_VMEM_BYTES = 64 * 1024 * 1024
VMEM_LIMIT_BYTES = 56 * 1024 * 1024


def _params(sem):
    return pltpu.CompilerParams(dimension_semantics=sem, vmem_limit_bytes=VMEM_LIMIT_BYTES)


def _dot(a, b):
    return jnp.dot(a, b, preferred_element_type=F32)


def _dot_nt(a, b):
    return lax.dot_general(a, b, (((1,), (1,)), ((), ())), preferred_element_type=F32)


def _dot_tn(a, b):
    return lax.dot_general(a, b, (((0,), (0,)), ((), ())), preferred_element_type=F32)


def _pick(n, prefs):
    for p in prefs:
        if n % p == 0:
            return p
    return n


def _ada_kernel(c_ref, w_ref, b_ref, o_ref):
    c = c_ref[...]
    s = jax.nn.silu(c).astype(BF16)
    o_ref[...] = _dot(s, w_ref[...].astype(BF16)) + b_ref[...]


def _ada_params(cond, ada_w, ada_b):
    depth, d, n6 = ada_w.shape
    nbp = cond.shape[0]
    tn = _pick(n6, (1536, 1024, 512, 256, 128))
    out = pl.pallas_call(
        _ada_kernel,
        out_shape=jax.ShapeDtypeStruct((depth, nbp, n6), F32),
        grid=(depth, n6 // tn),
        in_specs=[
            pl.BlockSpec((nbp, d), lambda l, j: (0, 0)),
            pl.BlockSpec((None, d, tn), lambda l, j: (l, 0, j)),
            pl.BlockSpec((None, 1, tn), lambda l, j: (l, 0, j)),
        ],
        out_specs=pl.BlockSpec((None, nbp, tn), lambda l, j: (l, 0, j)),
        compiler_params=_params(("arbitrary", "arbitrary")),
        name="ada_params",
    )(cond, ada_w, ada_b.reshape(depth, 1, n6))
    return out.reshape(depth, nbp, 6, 1, d).transpose(0, 2, 1, 3, 4)


def _mod_spec(d, layer, j):
    return pl.BlockSpec((None, None, None, 1, d), lambda b, i: (layer, j, b, 0, 0))


def _norm_mod(x, nw, sh, sc):
    ms = jnp.mean(x * x, axis=-1, keepdims=True)
    h = x * lax.rsqrt(ms + EPS) * nw
    return h * (1.0 + sc) + sh


def _ret_inproj_kernel(x_ref, nw_ref, sh_ref, sc_ref, w_ref, o_ref, *, hq, hv, tn, k_scale):
    hb = _norm_mod(x_ref[...], nw_ref[...], sh_ref[...], sc_ref[...]).astype(BF16)
    n = w_ref.shape[1]
    for j in range(n // tn):
        lo = j * tn
        acc = _dot(hb, w_ref[:, lo:lo + tn])
        if hq <= lo < 2 * hq:
            acc = acc * k_scale
        elif lo >= 2 * hq + hv:
            acc = jax.nn.silu(acc)
        o_ref[:, lo:lo + tn] = acc.astype(BF16)


def _ret_inproj(y, mod, layer, norm_w, w_in_bf16, dk):
    nb, l, d = y.shape
    n = w_in_bf16.shape[1]
    hq = RET_HEADS * dk
    hv = 2 * hq
    tm = _pick(l, (512, 256, 128))
    tn = _pick(hq, (1024, 512, 256, 128))
    kern = functools.partial(_ret_inproj_kernel, hq=hq, hv=hv, tn=tn, k_scale=float(dk) ** -0.5)
    return pl.pallas_call(
        kern,
        out_shape=jax.ShapeDtypeStruct((nb, l, n), BF16),
        grid=(nb, l // tm),
        in_specs=[
            pl.BlockSpec((None, tm, d), lambda b, i: (b, i, 0)),
            pl.BlockSpec((1, d), lambda b, i: (0, 0)),
            _mod_spec(d, layer, 0),
            _mod_spec(d, layer, 1),
            pl.BlockSpec((d, n), lambda b, i: (0, 0), pipeline_mode=pl.Buffered(1)),
        ],
        out_specs=pl.BlockSpec((None, tm, n), lambda b, i: (b, i, 0)),
        compiler_params=_params(("arbitrary", "arbitrary")),
        name="ret_inproj",
    )(y, norm_w.reshape(1, d), mod, mod, w_in_bf16)


def _log_sigmoid(x):
    return jnp.minimum(x, 0.0) - jnp.log1p(jnp.exp(-jnp.abs(x)))


def _ret_kernel(*refs, ls, c, has_init, emit_final):
    it = iter(refs)
    q_ref, k_ref, v_ref, gf_ref, gb_ref, gn_ref, df_ref, db_ref = (next(it) for _ in range(8))
    if has_init:
        s0f_ref, s0b_ref = next(it), next(it)
    y_ref = next(it)
    if emit_final:
        sf_out, sb_out = next(it), next(it)
    s_scr, sb_scr = next(it), next(it)

    dk = q_ref.shape[1]
    nchunks = ls // c
    lg_f = _log_sigmoid(df_ref[...])[:, 0:1]
    lg_b = _log_sigmoid(db_ref[...])[:, 0:1]

    ri = lax.broadcasted_iota(I32, (c, c), 0)
    ci = lax.broadcasted_iota(I32, (c, c), 1)
    dif = (ri - ci).astype(F32)
    decay_f = jnp.where(dif >= 0, jnp.exp(jnp.maximum(dif, 0.0) * lg_f), 0.0)
    decay_b = jnp.where(dif <= 0, jnp.exp(jnp.maximum(-dif, 0.0) * lg_b), 0.0)
    pos = lax.broadcasted_iota(I32, (c, dk), 0).astype(F32)
    qdec_f = jnp.exp((pos + 1.0) * lg_f)
    kdec_f = jnp.exp((c - 1.0 - pos) * lg_f)
    qdec_b = jnp.exp((c - pos) * lg_b)
    kdec_b = jnp.exp(pos * lg_b)
    cdec_f = jnp.exp(c * lg_f)
    cdec_b = jnp.exp(c * lg_b)
    gn = gn_ref[...]

    if has_init:
        s_scr[...] = s0b_ref[...]
    else:
        s_scr[...] = jnp.zeros_like(s_scr)

    def bwd_body(t, carry):
        n = nchunks - 1 - t
        r0 = pl.multiple_of(n * c, c)
        s = s_scr[...]
        sb_scr[n] = s.astype(BF16)
        kc = k_ref[pl.ds(r0, c), :].astype(F32)
        vc = v_ref[pl.ds(r0, c), :]
        s_scr[...] = s * cdec_b + _dot_tn((kc * kdec_b).astype(BF16), vc)
        return carry

    lax.fori_loop(0, nchunks, bwd_body, 0)
    if emit_final:
        sb_out[...] = s_scr[...]

    if has_init:
        s_scr[...] = s0f_ref[...]
    else:
        s_scr[...] = jnp.zeros_like(s_scr)

    def head_norm(o):
        mu = jnp.mean(o, axis=-1, keepdims=True)
        dlt = o - mu
        var = jnp.mean(dlt * dlt, axis=-1, keepdims=True)
        return dlt * lax.rsqrt(var + EPS) * gn

    def fwd_body(n, carry):
        r0 = pl.multiple_of(n * c, c)
        rows = pl.ds(r0, c)
        qc = q_ref[rows, :]
        kc = k_ref[rows, :]
        vc = v_ref[rows, :]
        qf = qc.astype(F32)
        sc = _dot_nt(qc, kc)
        s = s_scr[...]
        o_f = _dot((sc * decay_f).astype(BF16), vc) + _dot((qf * qdec_f).astype(BF16), s.astype(BF16))
        o_b = _dot((sc * decay_b).astype(BF16), vc) + _dot((qf * qdec_b).astype(BF16), sb_scr[n])
        s_scr[...] = s * cdec_f + _dot_tn((kc.astype(F32) * kdec_f).astype(BF16), vc)
        y = head_norm(o_f) * gf_ref[rows, :].astype(F32) + head_norm(o_b) * gb_ref[rows, :].astype(F32)
        y_ref[rows, :] = y.astype(BF16)
        return carry

    lax.fori_loop(0, nchunks, fwd_body, 0)
    if emit_final:
        sf_out[...] = s_scr[...]


def _retention(qkvg, nseq, seq0, ls, dk, gn_w, decay_f, decay_b, s0f=None, s0b=None, emit_final=False):
    h = RET_HEADS
    dv = 2 * dk
    c = _pick(ls, (128,))
    has_init = s0f is not None
    kq, kk, kv, kgf, kgb = 0, h, h, 2 * h, 3 * h
    in_specs = [
        pl.BlockSpec((None, ls, dk), lambda s, hh: (s + seq0, 0, kq + hh)),
        pl.BlockSpec((None, ls, dk), lambda s, hh: (s + seq0, 0, kk + hh)),
        pl.BlockSpec((None, ls, dv), lambda s, hh: (s + seq0, 0, kv + hh)),
        pl.BlockSpec((None, ls, dv), lambda s, hh: (s + seq0, 0, kgf + hh)),
        pl.BlockSpec((None, ls, dv), lambda s, hh: (s + seq0, 0, kgb + hh)),
        pl.BlockSpec((None, 1, dv), lambda s, hh: (hh, 0, 0)),
        pl.BlockSpec((None, 1, 128), lambda s, hh: (hh, 0, 0)),
        pl.BlockSpec((None, 1, 128), lambda s, hh: (hh, 0, 0)),
    ]
    args = [qkvg, qkvg, qkvg, qkvg, qkvg, gn_w.reshape(h, 1, dv),
            jnp.broadcast_to(decay_f.reshape(h, 1, 1), (h, 1, 128)),
            jnp.broadcast_to(decay_b.reshape(h, 1, 1), (h, 1, 128))]
    if has_init:
        in_specs += [pl.BlockSpec((None, None, dk, dv), lambda s, hh: (s, hh, 0, 0))] * 2
        args += [s0f, s0b]
    out_shape = [jax.ShapeDtypeStruct((nseq, ls, h * dv), BF16)]
    out_specs = [pl.BlockSpec((None, ls, dv), lambda s, hh: (s, 0, hh))]
    if emit_final:
        out_shape += [jax.ShapeDtypeStruct((nseq, h, dk, dv), F32)] * 2
        out_specs += [pl.BlockSpec((None, None, dk, dv), lambda s, hh: (s, hh, 0, 0))] * 2
    kern = functools.partial(_ret_kernel, ls=ls, c=c, has_init=has_init, emit_final=emit_final)
    return pl.pallas_call(
        kern,
        out_shape=out_shape,
        grid=(nseq, h),
        in_specs=in_specs,
        out_specs=out_specs,
        scratch_shapes=[pltpu.VMEM((dk, dv), F32), pltpu.VMEM((ls // c, dk, dv), BF16)],
        compiler_params=_params(("arbitrary", "arbitrary")),
        name="retention_init" if has_init else "retention_zero",
    )(*args)


def _qkv_kernel(x_ref, nw_ref, sh_ref, sc_ref, w_ref, qn_ref, kn_ref, cos_ref, sa_ref, sb_ref,
                o_ref, kv_ref, *, dh):
    hb = _norm_mod(x_ref[...], nw_ref[...], sh_ref[...], sc_ref[...]).astype(BF16)
    nq = ATT_HEADS * dh
    nk = ATT_KV_HEADS * dh
    cos, sa, sb = cos_ref[...], sa_ref[...], sb_ref[...]

    def norm_rope(a, w):
        a = a * lax.rsqrt(jnp.mean(a * a, axis=-1, keepdims=True) + EPS) * w
        roped = a * cos + pltpu.roll(a, dh - dh // 4, 1) * sa + pltpu.roll(a, dh // 4, 1) * sb
        return a, roped

    q = _dot(hb, w_ref[:, :nq])
    for hh in range(ATT_HEADS):
        _, r = norm_rope(q[:, hh * dh:(hh + 1) * dh], qn_ref[...])
        o_ref[:, hh * dh:(hh + 1) * dh] = r.astype(BF16)
    kvv = _dot(hb, w_ref[:, nq:])
    for hh in range(ATT_KV_HEADS):
        a, r = norm_rope(kvv[:, hh * dh:(hh + 1) * dh], kn_ref[...])
        o_ref[:, nq + hh * dh:nq + (hh + 1) * dh] = r.astype(BF16)
        kv_ref[:, hh * dh:(hh + 1) * dh] = a
    o_ref[:, nq + nk:] = kvv[:, nk:].astype(BF16)
    kv_ref[:, nk:] = kvv[:, nk:]


def _rope_tables(l, dh, rotate):
    axis_dim = dh // 2
    half = axis_dim // 2
    t = jnp.arange(l)
    row = (t // GRID_W).astype(F32)
    col = (t % GRID_W).astype(F32)
    inv = ROPE_THETA ** (-jnp.arange(0, axis_dim, 2, dtype=F32) / axis_dim)
    d = jnp.arange(dh)
    pos = jnp.where((d // axis_dim)[None, :] == 0, row[:, None], col[:, None])
    ang = pos * inv[d % half][None, :]
    first = ((d % axis_dim) < half)[None, :]
    cos, sin = jnp.cos(ang), jnp.sin(ang)
    if not rotate:
        cos, sin = jnp.ones_like(cos), jnp.zeros_like(sin)
    return cos, jnp.where(first, -sin, 0.0), jnp.where(first, 0.0, sin)


def _qkv_proj(y, ns, mod, layer, norm_w, w_bf16, qn, kn, dh):
    nb, l, d = y.shape
    n = w_bf16.shape[1]
    nkv = 2 * ATT_KV_HEADS * dh
    tm = _pick(l, (512, 256, 128))
    tabs = [jnp.stack([a, b]) for a, b in zip(_rope_tables(l, dh, True), _rope_tables(l, dh, False))]
    tab_spec = pl.BlockSpec((None, tm, dh), lambda b, i: (jnp.where(b < ns, 0, 1), i, 0))
    return pl.pallas_call(
        functools.partial(_qkv_kernel, dh=dh),
        out_shape=[jax.ShapeDtypeStruct((nb, l, n), BF16), jax.ShapeDtypeStruct((nb, l, nkv), F32)],
        grid=(nb, l // tm),
        in_specs=[
            pl.BlockSpec((None, tm, d), lambda b, i: (b, i, 0)),
            pl.BlockSpec((1, d), lambda b, i: (0, 0)),
            _mod_spec(d, layer, 0),
            _mod_spec(d, layer, 1),
            pl.BlockSpec((d, n), lambda b, i: (0, 0), pipeline_mode=pl.Buffered(1)),
            pl.BlockSpec((1, dh), lambda b, i: (0, 0)),
            pl.BlockSpec((1, dh), lambda b, i: (0, 0)),
            tab_spec, tab_spec, tab_spec,
        ],
        out_specs=[pl.BlockSpec((None, tm, n), lambda b, i: (b, i, 0)),
                   pl.BlockSpec((None, tm, nkv), lambda b, i: (b, i, 0))],
        compiler_params=_params(("arbitrary", "arbitrary")),
        name="qkv_proj",
    )(y, norm_w.reshape(1, d), mod, mod, w_bf16, qn.reshape(1, dh), kn.reshape(1, dh), *tabs)


def _attn_kernel(*refs, dh, group, has_cache, scale):
    if has_cache:
        q_ref, k_ref, v_ref, kc_ref, vc_ref, o_ref = refs
        kc = kc_ref[...].astype(BF16)
        vc = vc_ref[...].astype(BF16)
    else:
        q_ref, k_ref, v_ref, o_ref = refs
    k = k_ref[...]
    v = v_ref[...]
    for g in range(group):
        q = q_ref[:, g * dh:(g + 1) * dh]
        s1 = _dot_nt(q, k)
        m = jnp.max(s1, axis=-1, keepdims=True)
        if has_cache:
            s0 = _dot_nt(q, kc)
            m = jnp.maximum(m, jnp.max(s0, axis=-1, keepdims=True))
        p1 = jnp.exp((s1 - m) * scale)
        den = jnp.sum(p1, axis=-1, keepdims=True)
        o = _dot(p1.astype(BF16), v)
        if has_cache:
            p0 = jnp.exp((s0 - m) * scale)
            den = den + jnp.sum(p0, axis=-1, keepdims=True)
            o = o + _dot(p0.astype(BF16), vc)
        o_ref[:, g * dh:(g + 1) * dh] = (o / den).astype(BF16)


def _attention(qkv, nseq, seq0, ls, dh, cache_k=None, cache_v=None):
    group = ATT_HEADS // ATT_KV_HEADS
    has_cache = cache_k is not None
    tq = _pick(ls, (256, 128))
    kcol = ATT_HEADS
    vcol = ATT_HEADS + ATT_KV_HEADS
    in_specs = [
        pl.BlockSpec((None, tq, group * dh), lambda s, kh, i: (s + seq0, i, kh)),
        pl.BlockSpec((None, ls, dh), lambda s, kh, i: (s + seq0, 0, kcol + kh)),
        pl.BlockSpec((None, ls, dh), lambda s, kh, i: (s + seq0, 0, vcol + kh)),
    ]
    args = [qkv, qkv, qkv]
    if has_cache:
        past = cache_k.shape[1]
        in_specs += [pl.BlockSpec((None, past, dh), lambda s, kh, i: (s, 0, kh))] * 2
        args += [cache_k, cache_v]
    kern = functools.partial(_attn_kernel, dh=dh, group=group, has_cache=has_cache, scale=float(dh) ** -0.5)
    return pl.pallas_call(
        kern,
        out_shape=jax.ShapeDtypeStruct((nseq, ls, ATT_HEADS * dh), BF16),
        grid=(nseq, ATT_KV_HEADS, ls // tq),
        in_specs=in_specs,
        out_specs=pl.BlockSpec((None, tq, group * dh), lambda s, kh, i: (s, i, kh)),
        compiler_params=_params(("arbitrary", "arbitrary", "arbitrary")),
        name="attention_cache" if has_cache else "attention_self",
    )(*args)


def _pack_rows(h):
    half = h.shape[1] // 2
    u = lax.bitcast_convert_type(h.astype(BF16).astype(F32), U32)
    return (u[:, :half] >> 16) | (u[:, half:] & jnp.uint32(0xFFFF0000))


def _unpack_rows(p):
    lo = lax.bitcast_convert_type(p << 16, F32)
    hi = lax.bitcast_convert_type(p & jnp.uint32(0xFFFF0000), F32)
    return jnp.concatenate([lo, hi], axis=1).astype(BF16)


def _outproj_kernel(os_ref, op_ref, w_ref, y_ref, g1_ref, nw_ref, sh_ref, sc_ref, rwh_ref, rwl_ref,
                    yo_ref, h_ref, lg_ref, *, ns):
    is_prompt = pl.program_id(0) >= ns
    o = jnp.where(is_prompt, op_ref[...], os_ref[...])
    y = y_ref[...] + g1_ref[...] * _dot(o, w_ref[...])
    yo_ref[...] = y
    h = _norm_mod(y, nw_ref[...], sh_ref[...], sc_ref[...])
    h_ref[...] = _pack_rows(h)
    h_hi = h.astype(BF16)
    h_lo = (h - h_hi.astype(F32)).astype(BF16)
    rwh = rwh_ref[...]
    lg_ref[...] = _dot_nt(rwh, h_hi) + _dot_nt(rwl_ref[...], h_hi) + _dot_nt(rwh, h_lo)


def _outproj(o_s, o_p, w_bf16, y, mod, layer, norm_w, rw_hi, rw_lo):
    nb, l, d = y.shape
    ns = o_s.shape[0]
    kdim = o_s.shape[2]
    ne = rw_hi.shape[0]
    tm = _pick(l, (512, 256, 128))
    ni = l // tm
    dp = jax.eval_shape(_pack_rows, jax.ShapeDtypeStruct((8, d), F32)).shape[1]
    return pl.pallas_call(
        functools.partial(_outproj_kernel, ns=ns),
        out_shape=[jax.ShapeDtypeStruct((nb, l, d), F32), jax.ShapeDtypeStruct((nb, l, dp), U32),
                   jax.ShapeDtypeStruct((nb, ne, l), F32)],
        grid=(nb, ni),
        in_specs=[
            pl.BlockSpec((None, tm, kdim), lambda b, i: (jnp.minimum(b, ns - 1), jnp.where(b < ns, i, ni - 1), 0)),
            pl.BlockSpec((None, tm, kdim), lambda b, i: (0, jnp.where(b < ns, 0, i), 0)),
            pl.BlockSpec((kdim, d), lambda b, i: (0, 0), pipeline_mode=pl.Buffered(1)),
            pl.BlockSpec((None, tm, d), lambda b, i: (b, i, 0)),
            _mod_spec(d, layer, 2),
            pl.BlockSpec((1, d), lambda b, i: (0, 0)),
            _mod_spec(d, layer, 3),
            _mod_spec(d, layer, 4),
            pl.BlockSpec((ne, d), lambda b, i: (0, 0)),
            pl.BlockSpec((ne, d), lambda b, i: (0, 0)),
        ],
        out_specs=[pl.BlockSpec((None, tm, d), lambda b, i: (b, i, 0)),
                   pl.BlockSpec((None, tm, dp), lambda b, i: (b, i, 0)),
                   pl.BlockSpec((None, ne, tm), lambda b, i: (b, 0, i))],
        compiler_params=_params(("arbitrary", "arbitrary")),
        name="outproj_router",
    )(o_s, o_p, w_bf16, y, mod, norm_w.reshape(1, d), mod, mod, rw_hi, rw_lo)


def _router_kernel(lg_ref, b_ref, bkt_ref, rank_ref, wt_ref, cnt_ref, carry_ref):
    ng = N_GROUPS
    sc = jax.nn.sigmoid(lg_ref[...])
    sel = sc + b_ref[...]
    sj = [sel[j * ng:(j + 1) * ng] for j in range(EXPERTS_PER_GROUP)]
    pj = [sc[j * ng:(j + 1) * ng] for j in range(EXPERTS_PER_GROUP)]
    hi1, lo1 = jnp.maximum(sj[0], sj[1]), jnp.minimum(sj[0], sj[1])
    hi2, lo2 = jnp.maximum(sj[2], sj[3]), jnp.minimum(sj[2], sj[3])
    top1 = jnp.maximum(hi1, hi2)
    top2 = jnp.maximum(jnp.minimum(hi1, hi2), jnp.maximum(lo1, lo2))
    gs = top1 + top2
    gmax = jnp.max(gs, axis=0, keepdims=True)
    gi = lax.broadcasted_iota(I32, gs.shape, 0)
    best = jnp.min(jnp.where(gs == gmax, gi, ng), axis=0, keepdims=True)
    onehot = gi == best
    neg = -jnp.inf
    vj = [jnp.max(jnp.where(onehot, a, neg), axis=0, keepdims=True) for a in sj]
    wj = [jnp.max(jnp.where(onehot, a, neg), axis=0, keepdims=True) for a in pj]

    def argmax4(vals):
        m, i = vals[0], jnp.zeros(vals[0].shape, I32)
        for j in range(1, EXPERTS_PER_GROUP):
            gt = vals[j] > m
            m = jnp.where(gt, vals[j], m)
            i = jnp.where(gt, j, i)
        return i

    i1 = argmax4(vj)
    i2 = argmax4([jnp.where(i1 == j, neg, vj[j]) for j in range(EXPERTS_PER_GROUP)])

    def take(vals, i):
        out = vals[0]
        for j in range(1, EXPERTS_PER_GROUP):
            out = jnp.where(i == j, vals[j], out)
        return out

    w1, w2 = take(wj, i1), take(wj, i2)
    tot = w1 + w2
    w1, w2 = w1 / tot, w2 / tot

    lo, hi = jnp.minimum(i1, i2), jnp.maximum(i1, i2)
    pair = jnp.where(lo == 0, hi - 1, jnp.where(lo == 1, jnp.where(hi == 3, 3, 4), 5))
    slot_a = jnp.where(pair < 3, 0, jnp.where(pair < 5, 1, 3))
    bucket = best * N_PAIRS + pair
    a_first = i1 == slot_a
    w_a = jnp.where(a_first, w1, w2)
    w_b = jnp.where(a_first, w2, w1)

    tl = bucket.shape[1]
    nbk = carry_ref.shape[0]

    @pl.when(jnp.logical_and(pl.program_id(0) == 0, pl.program_id(1) == 0))
    def _():
        carry_ref[...] = jnp.zeros_like(carry_ref)

    onehot_b = lax.broadcasted_iota(I32, (nbk, tl), 0) == bucket
    upper = (lax.broadcasted_iota(I32, (tl, tl), 0) <= lax.broadcasted_iota(I32, (tl, tl), 1))
    prefix = _dot(onehot_b.astype(BF16), upper.astype(BF16))
    carry = carry_ref[...]
    rank = jnp.sum(jnp.where(onehot_b, prefix - 1.0 + carry, 0.0), axis=0, keepdims=True)
    carry = carry + prefix[:, tl - 1:tl]
    carry_ref[...] = carry
    bkt_ref[...] = bucket
    rank_ref[...] = rank.astype(I32)
    cnt_ref[...] = jnp.broadcast_to(carry, cnt_ref.shape)
    rows = lax.broadcasted_iota(I32, (wt_ref.shape[1], tl), 0)
    wt_ref[...] = jnp.where(rows == 0, w_a, jnp.where(rows == 1, w_b, 0.0)).T


def _router(logits, bias_perm):
    nb, ne, l = logits.shape
    tl = _pick(l, (512, 256, 128))
    nbk = N_GROUPS * N_PAIRS
    return pl.pallas_call(
        _router_kernel,
        out_shape=[jax.ShapeDtypeStruct((nb, 1, l), I32), jax.ShapeDtypeStruct((nb, 1, l), I32),
                   jax.ShapeDtypeStruct((nb, l, 128), F32), jax.ShapeDtypeStruct((nbk, 128), F32)],
        grid=(nb, l // tl),
        in_specs=[pl.BlockSpec((None, ne, tl), lambda b, i: (b, 0, i)),
                  pl.BlockSpec((ne, 1), lambda b, i: (0, 0))],
        out_specs=[pl.BlockSpec((None, 1, tl), lambda b, i: (b, 0, i)),
                   pl.BlockSpec((None, 1, tl), lambda b, i: (b, 0, i)),
                   pl.BlockSpec((None, tl, 128), lambda b, i: (b, i, 0)),
                   pl.BlockSpec((nbk, 128), lambda b, i: (0, 0))],
        scratch_shapes=[pltpu.VMEM((nbk, 1), F32)],
        compiler_params=_params(("arbitrary", "arbitrary")),
        name="router_rank",
    )(logits, bias_perm.reshape(ne, 1))


def _moe_kernel(ea_ref, eb_ref, nu_ref, x_ref, wga_ref, wua_ref, wda_ref, wgb_ref, wub_ref, wdb_ref, o_ref):
    del ea_ref, eb_ref
    i = pl.program_id(0)
    d = o_ref.shape[1] // 2

    @pl.when(i < nu_ref[0])
    def _():
        x = _unpack_rows(x_ref[...])

        def expert(wg, wu, wd):
            hid = jax.nn.silu(_dot(x, wg[...])) * _dot(x, wu[...])
            return _dot(hid.astype(BF16), wd[...])

        o_ref[:, :d] = expert(wga_ref, wua_ref, wda_ref)
        o_ref[:, d:] = expert(wgb_ref, wub_ref, wdb_ref)

    @pl.when(i >= nu_ref[0])
    def _():
        o_ref[...] = jnp.zeros_like(o_ref)


def _moe_experts(xs, blk_a, blk_b, n_used, wg, wu, wd, bm):
    p, dp = xs.shape
    _, d, de = wg.shape
    nblk = p // bm
    w_in = lambda sel: pl.BlockSpec((None, d, de), lambda i, ea, eb, nu: ((ea, eb)[sel][i], 0, 0))
    w_out = lambda sel: pl.BlockSpec((None, de, d), lambda i, ea, eb, nu: ((ea, eb)[sel][i], 0, 0))
    grid_spec = pltpu.PrefetchScalarGridSpec(
        num_scalar_prefetch=3,
        grid=(nblk,),
        in_specs=[pl.BlockSpec((bm, dp), lambda i, ea, eb, nu: (i, 0)),
                  w_in(0), w_in(0), w_out(0), w_in(1), w_in(1), w_out(1)],
        out_specs=pl.BlockSpec((bm, 2 * d), lambda i, ea, eb, nu: (i, 0)),
    )
    return pl.pallas_call(
        _moe_kernel,
        out_shape=jax.ShapeDtypeStruct((p, 2 * d), F32),
        grid_spec=grid_spec,
        compiler_params=_params(("arbitrary",)),
        name="moe_experts",
    )(blk_a, blk_b, n_used, xs, wg, wu, wd, wg, wu, wd)


def _row_plan(bucket, rank, counts, bm):
    t = bucket.size
    nbk = N_GROUPS * N_PAIRS
    cnt = counts[:, 0].astype(I32)
    padded = (cnt + bm - 1) // bm * bm
    pends = jnp.cumsum(padded)
    pstarts = pends - padded
    onehot = bucket.reshape(t, 1) == jnp.arange(nbk, dtype=I32)[None, :]
    dest = jnp.sum(jnp.where(onehot, pstarts[None, :], 0), axis=1) + rank.reshape(t)
    p = (t + bm - 1) // bm * bm + nbk * bm
    nblk = p // bm
    blk_bucket = jnp.clip(jnp.searchsorted(pends, jnp.arange(nblk, dtype=I32) * bm, side='right'), 0, nbk - 1)
    grp, pair = blk_bucket // N_PAIRS, blk_bucket % N_PAIRS
    blk_a = grp * EXPERTS_PER_GROUP + jnp.asarray(_PAIR_A, I32)[pair]
    blk_b = grp * EXPERTS_PER_GROUP + jnp.asarray(_PAIR_B, I32)[pair]
    n_used = (pends[-1:] // bm).astype(I32)
    return dest.astype(I32), blk_a.astype(I32), blk_b.astype(I32), n_used, p


def _scatter_kernel(dest_ref, src_ref, init_hbm, out_hbm, sem, *, rows):
    del init_hbm
    base = pl.program_id(0) * rows

    def issue(r, carry):
        pltpu.make_async_copy(src_ref.at[r], out_hbm.at[dest_ref[base + r]], sem).start()
        return carry

    lax.fori_loop(0, rows, issue, 0, unroll=8)
    pltpu.make_async_copy(src_ref, out_hbm.at[pl.ds(0, rows)], sem).wait()


def _scatter_rows(h_rows, dest, p):
    t, dp = h_rows.shape
    rows = _pick(t, (2048, 1024, 512))
    grid_spec = pltpu.PrefetchScalarGridSpec(
        num_scalar_prefetch=1,
        grid=(t // rows,),
        in_specs=[pl.BlockSpec((rows, dp), lambda i, dst: (i, 0)), pl.BlockSpec(memory_space=pl.ANY)],
        out_specs=pl.BlockSpec(memory_space=pl.ANY),
        scratch_shapes=[pltpu.SemaphoreType.DMA],
    )
    return pl.pallas_call(
        functools.partial(_scatter_kernel, rows=rows),
        out_shape=jax.ShapeDtypeStruct((p, dp), h_rows.dtype),
        grid_spec=grid_spec,
        input_output_aliases={2: 0},
        compiler_params=_params(("arbitrary",)),
        name="moe_scatter_rows",
    )(dest, h_rows, jnp.zeros((p, dp), h_rows.dtype))


def _combine_kernel(dest_ref, y_ref, g_ref, wt_ref, m_hbm, o_ref, mbuf, sem, *, tm, nsteps):
    step = pl.program_id(0) * pl.num_programs(1) + pl.program_id(1)
    slot = step % 2
    d = y_ref.shape[1]

    def gather(s, sl):
        def issue(r, carry):
            pltpu.make_async_copy(m_hbm.at[dest_ref[s * tm + r]], mbuf.at[sl, r], sem.at[sl]).start()
            return carry
        lax.fori_loop(0, tm, issue, 0, unroll=8)

    @pl.when(step == 0)
    def _():
        gather(0, 0)

    @pl.when(step + 1 < nsteps)
    def _():
        gather(step + 1, 1 - slot)

    pltpu.make_async_copy(m_hbm.at[pl.ds(0, tm)], mbuf.at[slot], sem.at[slot]).wait()
    m = mbuf[slot]
    w = wt_ref[...]
    o_ref[...] = y_ref[...] + g_ref[...] * (w[:, 0:1] * m[:, :d] + w[:, 1:2] * m[:, d:])


def _combine(y, mod, layer, moe_rows, dest, wt):
    nb, l, d = y.shape
    tm = _pick(l, (512, 256, 128))
    ni = l // tm
    grid_spec = pltpu.PrefetchScalarGridSpec(
        num_scalar_prefetch=1,
        grid=(nb, ni),
        in_specs=[
            pl.BlockSpec((None, tm, d), lambda b, i, dst: (b, i, 0)),
            pl.BlockSpec((None, None, None, 1, d), lambda b, i, dst: (layer, 5, b, 0, 0)),
            pl.BlockSpec((None, tm, 128), lambda b, i, dst: (b, i, 0)),
            pl.BlockSpec(memory_space=pl.ANY),
        ],
        out_specs=pl.BlockSpec((None, tm, d), lambda b, i, dst: (b, i, 0)),
        scratch_shapes=[pltpu.VMEM((2, tm, 2 * d), F32), pltpu.SemaphoreType.DMA((2,))],
    )
    return pl.pallas_call(
        functools.partial(_combine_kernel, tm=tm, nsteps=nb * ni),
        out_shape=jax.ShapeDtypeStruct((nb, l, d), F32),
        grid_spec=grid_spec,
        compiler_params=_params(("arbitrary", "arbitrary")),
        name="moe_combine",
    )(dest, y, mod, wt, moe_rows)


def kernel(x_prompt, x_sample, c, state_ret_fwd, state_ret_bwd, cache_attn_k, cache_attn_v, c_ctx,
           ada_w, ada_b, norm1_w, norm2_w, ret_w_in, ret_w_out, ret_gn_w, ret_decay_fwd, ret_decay_bwd,
           attn_w_qkv, attn_w_o, attn_q_norm, attn_k_norm, router_w, router_b,
           moe_w_gate, moe_w_up, moe_w_down):
    n_prompt, seq, d = x_prompt.shape
    ns, l, _ = x_sample.shape
    assert n_prompt * seq == l, "prompt tokens must fill exactly one sample-length row"
    depth = ada_w.shape[0]
    nb = ns + 1
    dk = d // RET_HEADS
    dv = 2 * dk
    dh = d // ATT_HEADS
    bm = MOE_ROW_BLOCK

    y = jnp.concatenate([x_sample, x_prompt.reshape(1, l, d)], axis=0)
    nbp = (nb + 7) // 8 * 8
    cond = jnp.zeros((nbp, d), F32).at[:ns].set(c).at[ns].set(c_ctx)
    mod = _ada_params(cond, ada_w, ada_b)

    perm = (jnp.arange(N_EXPERTS) % N_GROUPS) * EXPERTS_PER_GROUP + jnp.arange(N_EXPERTS) // N_GROUPS
    rw_t = router_w.T[perm]
    rw_hi = rw_t.astype(BF16)
    rw_lo = (rw_t - rw_hi.astype(F32)).astype(BF16)
    rb_perm = router_b[perm]

    ret_f, ret_b, k_new, v_new = [], [], [], []
    for i in range(depth):
        if i % 2 == 0:
            r = i // 2
            qkvg = _ret_inproj(y, mod, i, norm1_w[i], ret_w_in[r].astype(BF16), dk)
            o_s = _retention(qkvg, ns, 0, l, dk, ret_gn_w[r], ret_decay_fwd[r], ret_decay_bwd[r],
                             s0f=state_ret_fwd[:, r], s0b=state_ret_bwd[:, r])
            qkvg_p = qkvg.reshape(nb * n_prompt, seq, qkvg.shape[2])
            o_p, s_f, s_b = _retention(qkvg_p, n_prompt, ns * n_prompt, seq, dk, ret_gn_w[r],
                                       ret_decay_fwd[r], ret_decay_bwd[r], emit_final=True)
            o_s = o_s[0] if isinstance(o_s, (list, tuple)) else o_s
            ret_f.append(s_f)
            ret_b.append(s_b)
            w_o = ret_w_out[r].astype(BF16)
        else:
            a = i // 2
            qkv, kvf = _qkv_proj(y, ns, mod, i, norm1_w[i], attn_w_qkv[a].astype(BF16),
                                 attn_q_norm[a], attn_k_norm[a], dh)
            past = cache_attn_k.shape[2]
            o_s = _attention(qkv, ns, 0, l, dh,
                             cache_k=cache_attn_k[:, a].reshape(ns, past, ATT_KV_HEADS * dh),
                             cache_v=cache_attn_v[:, a].reshape(ns, past, ATT_KV_HEADS * dh))
            o_p = _attention(qkv.reshape(nb * n_prompt, seq, qkv.shape[2]), n_prompt, ns * n_prompt, seq, dh)
            nk = ATT_KV_HEADS * dh
            k_new.append(kvf[ns, :, :nk].reshape(n_prompt, seq, ATT_KV_HEADS, dh))
            v_new.append(kvf[ns, :, nk:].reshape(n_prompt, seq, ATT_KV_HEADS, dh))
            w_o = attn_w_o[a].astype(BF16)
        o_p = o_p.reshape(1, l, o_p.shape[-1])
        y, h2, logits = _outproj(o_s, o_p, w_o, y, mod, i, norm2_w[i], rw_hi, rw_lo)
        bucket, rank, wt, counts = _router(logits, rb_perm)
        dest, blk_a, blk_b, n_used, p = _row_plan(bucket, rank, counts, bm)
        xs = _scatter_rows(h2.reshape(nb * l, h2.shape[2]), dest, p)
        moe_rows = _moe_experts(xs, blk_a, blk_b, n_used, moe_w_gate[i].astype(BF16),
                                moe_w_up[i].astype(BF16), moe_w_down[i].astype(BF16), bm)
        y = _combine(y, mod, i, moe_rows, dest, wt)

    y_p = y[ns].reshape(n_prompt, seq, d)
    y_s = y[:ns]
    return (y_p, y_s, jnp.stack(ret_f, axis=1), jnp.stack(ret_b, axis=1),
            jnp.stack(k_new, axis=1), jnp.stack(v_new, axis=1))
```

```python
import functools

import jax
import jax.numpy as jnp
from jax import lax
from jax.experimental import pallas as pl
from jax.experimental.pallas import tpu as pltpu

F32 = jnp.float32
BF16 = jnp.bfloat16
I32 = jnp.int32
U32 = jnp.uint32

EPS = 1e-6
GRID_W = 64
RET_HEADS = 4
ATT_HEADS = 8
ATT_KV_HEADS = 2
ROPE_THETA = 10000.0
N_EXPERTS = 32
N_GROUPS = 8
EXPERTS_PER_GROUP = N_EXPERTS // N_GROUPS
TOP_K = 2
_PAIR_A = (0, 0, 0, 1, 1, 3)
_PAIR_B = (1, 2, 3, 3, 2, 2)
N_PAIRS = len(_PAIR_A)
MOE_ROW_BLOCK = 256

V7X_VMEM_BYTES = 64 * 1024 * 1024
VMEM_LIMIT_BYTES = 56 * 1024 * 1024


def _params(sem):
    return pltpu.CompilerParams(dimension_semantics=sem, vmem_limit_bytes=VMEM_LIMIT_BYTES)


def _dot(a, b):
    return jnp.dot(a, b, preferred_element_type=F32)


def _dot_nt(a, b):
    return lax.dot_general(a, b, (((1,), (1,)), ((), ())), preferred_element_type=F32)


def _dot_tn(a, b):
    return lax.dot_general(a, b, (((0,), (0,)), ((), ())), preferred_element_type=F32)


def _pick(n, prefs):
    for p in prefs:
        if n % p == 0:
            return p
    return n


def _ada_kernel(c_ref, w_ref, b_ref, o_ref):
    c = c_ref[...]
    s = jax.nn.silu(c).astype(BF16)
    o_ref[...] = _dot(s, w_ref[...].astype(BF16)) + b_ref[...]


def _ada_params(cond, ada_w, ada_b):
    depth, d, n6 = ada_w.shape
    nbp = cond.shape[0]
    tn = _pick(n6, (1536, 1024, 512, 256, 128))
    out = pl.pallas_call(
        _ada_kernel,
        out_shape=jax.ShapeDtypeStruct((depth, nbp, n6), F32),
        grid=(depth, n6 // tn),
        in_specs=[
            pl.BlockSpec((nbp, d), lambda l, j: (0, 0)),
            pl.BlockSpec((None, d, tn), lambda l, j: (l, 0, j)),
            pl.BlockSpec((None, 1, tn), lambda l, j: (l, 0, j)),
        ],
        out_specs=pl.BlockSpec((None, nbp, tn), lambda l, j: (l, 0, j)),
        compiler_params=_params(("arbitrary", "arbitrary")),
        name="ada_params",
    )(cond, ada_w, ada_b.reshape(depth, 1, n6))
    return out.reshape(depth, nbp, 6, 1, d).transpose(0, 2, 1, 3, 4)


def _mod_spec(d, layer, j):
    return pl.BlockSpec((None, None, None, 1, d), lambda b, i: (layer, j, b, 0, 0))


def _norm_mod(x, nw, sh, sc):
    ms = jnp.mean(x * x, axis=-1, keepdims=True)
    h = x * lax.rsqrt(ms + EPS) * nw
    return h * (1.0 + sc) + sh


def _ret_inproj_kernel(x_ref, nw_ref, sh_ref, sc_ref, w_ref, o_ref, *, hq, hv, tn, k_scale):
    hb = _norm_mod(x_ref[...], nw_ref[...], sh_ref[...], sc_ref[...]).astype(BF16)
    n = w_ref.shape[1]
    for j in range(n // tn):
        lo = j * tn
        acc = _dot(hb, w_ref[:, lo:lo + tn])
        if hq <= lo < 2 * hq:
            acc = acc * k_scale
        elif lo >= 2 * hq + hv:
            acc = jax.nn.silu(acc)
        o_ref[:, lo:lo + tn] = acc.astype(BF16)


def _ret_inproj(y, mod, layer, norm_w, w_in_bf16, dk):
    nb, l, d = y.shape
    n = w_in_bf16.shape[1]
    hq = RET_HEADS * dk
    hv = 2 * hq
    tm = _pick(l, (512, 256, 128))
    tn = _pick(hq, (1024, 512, 256, 128))
    kern = functools.partial(_ret_inproj_kernel, hq=hq, hv=hv, tn=tn, k_scale=float(dk) ** -0.5)
    return pl.pallas_call(
        kern,
        out_shape=jax.ShapeDtypeStruct((nb, l, n), BF16),
        grid=(nb, l // tm),
        in_specs=[
            pl.BlockSpec((None, tm, d), lambda b, i: (b, i, 0)),
            pl.BlockSpec((1, d), lambda b, i: (0, 0)),
            _mod_spec(d, layer, 0),
            _mod_spec(d, layer, 1),
            pl.BlockSpec((d, n), lambda b, i: (0, 0), pipeline_mode=pl.Buffered(1)),
        ],
        out_specs=pl.BlockSpec((None, tm, n), lambda b, i: (b, i, 0)),
        compiler_params=_params(("arbitrary", "arbitrary")),
        name="ret_inproj",
    )(y, norm_w.reshape(1, d), mod, mod, w_in_bf16)


def _log_sigmoid(x):
    return jnp.minimum(x, 0.0) - jnp.log1p(jnp.exp(-jnp.abs(x)))


def _ret_kernel(*refs, ls, c, has_init, emit_final):
    it = iter(refs)
    q_ref, k_ref, v_ref, gf_ref, gb_ref, gn_ref, df_ref, db_ref = (next(it) for _ in range(8))
    if has_init:
        s0f_ref, s0b_ref = next(it), next(it)
    y_ref = next(it)
    if emit_final:
        sf_out, sb_out = next(it), next(it)
    s_scr, sb_scr = next(it), next(it)

    dk = q_ref.shape[1]
    nchunks = ls // c
    lg_f = _log_sigmoid(df_ref[...])[:, 0:1]
    lg_b = _log_sigmoid(db_ref[...])[:, 0:1]

    ri = lax.broadcasted_iota(I32, (c, c), 0)
    ci = lax.broadcasted_iota(I32, (c, c), 1)
    dif = (ri - ci).astype(F32)
    decay_f = jnp.where(dif >= 0, jnp.exp(jnp.maximum(dif, 0.0) * lg_f), 0.0)
    decay_b = jnp.where(dif <= 0, jnp.exp(jnp.maximum(-dif, 0.0) * lg_b), 0.0)
    pos = lax.broadcasted_iota(I32, (c, dk), 0).astype(F32)
    qdec_f = jnp.exp((pos + 1.0) * lg_f)
    kdec_f = jnp.exp((c - 1.0 - pos) * lg_f)
    qdec_b = jnp.exp((c - pos) * lg_b)
    kdec_b = jnp.exp(pos * lg_b)
    cdec_f = jnp.exp(c * lg_f)
    cdec_b = jnp.exp(c * lg_b)
    gn = gn_ref[...]

    if has_init:
        s_scr[...] = s0b_ref[...]
    else:
        s_scr[...] = jnp.zeros_like(s_scr)

    def bwd_body(t, carry):
        n = nchunks - 1 - t
        r0 = pl.multiple_of(n * c, c)
        s = s_scr[...]
        sb_scr[n] = s.astype(BF16)
        kc = k_ref[pl.ds(r0, c), :].astype(F32)
        vc = v_ref[pl.ds(r0, c), :]
        s_scr[...] = s * cdec_b + _dot_tn((kc * kdec_b).astype(BF16), vc)
        return carry

    lax.fori_loop(0, nchunks, bwd_body, 0)
    if emit_final:
        sb_out[...] = s_scr[...]

    if has_init:
        s_scr[...] = s0f_ref[...]
    else:
        s_scr[...] = jnp.zeros_like(s_scr)

    def head_norm(o):
        mu = jnp.mean(o, axis=-1, keepdims=True)
        dlt = o - mu
        var = jnp.mean(dlt * dlt, axis=-1, keepdims=True)
        return dlt * lax.rsqrt(var + EPS) * gn

    def fwd_body(n, carry):
        r0 = pl.multiple_of(n * c, c)
        rows = pl.ds(r0, c)
        qc = q_ref[rows, :]
        kc = k_ref[rows, :]
        vc = v_ref[rows, :]
        qf = qc.astype(F32)
        sc = _dot_nt(qc, kc)
        s = s_scr[...]
        o_f = _dot((sc * decay_f).astype(BF16), vc) + _dot((qf * qdec_f).astype(BF16), s.astype(BF16))
        o_b = _dot((sc * decay_b).astype(BF16), vc) + _dot((qf * qdec_b).astype(BF16), sb_scr[n])
        s_scr[...] = s * cdec_f + _dot_tn((kc.astype(F32) * kdec_f).astype(BF16), vc)
        y = head_norm(o_f) * gf_ref[rows, :].astype(F32) + head_norm(o_b) * gb_ref[rows, :].astype(F32)
        y_ref[rows, :] = y.astype(BF16)
        return carry

    lax.fori_loop(0, nchunks, fwd_body, 0)
    if emit_final:
        sf_out[...] = s_scr[...]


def _retention(qkvg, nseq, seq0, ls, dk, gn_w, decay_f, decay_b, s0f=None, s0b=None, emit_final=False):
    h = RET_HEADS
    dv = 2 * dk
    c = _pick(ls, (128,))
    has_init = s0f is not None
    kq, kk, kv, kgf, kgb = 0, h, h, 2 * h, 3 * h
    in_specs = [
        pl.BlockSpec((None, ls, dk), lambda s, hh: (s + seq0, 0, kq + hh)),
        pl.BlockSpec((None, ls, dk), lambda s, hh: (s + seq0, 0, kk + hh)),
        pl.BlockSpec((None, ls, dv), lambda s, hh: (s + seq0, 0, kv + hh)),
        pl.BlockSpec((None, ls, dv), lambda s, hh: (s + seq0, 0, kgf + hh)),
        pl.BlockSpec((None, ls, dv), lambda s, hh: (s + seq0, 0, kgb + hh)),
        pl.BlockSpec((None, 1, dv), lambda s, hh: (hh, 0, 0)),
        pl.BlockSpec((None, 1, 128), lambda s, hh: (hh, 0, 0)),
        pl.BlockSpec((None, 1, 128), lambda s, hh: (hh, 0, 0)),
    ]
    args = [qkvg, qkvg, qkvg, qkvg, qkvg, gn_w.reshape(h, 1, dv),
            jnp.broadcast_to(decay_f.reshape(h, 1, 1), (h, 1, 128)),
            jnp.broadcast_to(decay_b.reshape(h, 1, 1), (h, 1, 128))]
    if has_init:
        in_specs += [pl.BlockSpec((None, None, dk, dv), lambda s, hh: (s, hh, 0, 0))] * 2
        args += [s0f, s0b]
    out_shape = [jax.ShapeDtypeStruct((nseq, ls, h * dv), BF16)]
    out_specs = [pl.BlockSpec((None, ls, dv), lambda s, hh: (s, 0, hh))]
    if emit_final:
        out_shape += [jax.ShapeDtypeStruct((nseq, h, dk, dv), F32)] * 2
        out_specs += [pl.BlockSpec((None, None, dk, dv), lambda s, hh: (s, hh, 0, 0))] * 2
    kern = functools.partial(_ret_kernel, ls=ls, c=c, has_init=has_init, emit_final=emit_final)
    return pl.pallas_call(
        kern,
        out_shape=out_shape,
        grid=(nseq, h),
        in_specs=in_specs,
        out_specs=out_specs,
        scratch_shapes=[pltpu.VMEM((dk, dv), F32), pltpu.VMEM((ls // c, dk, dv), BF16)],
        compiler_params=_params(("arbitrary", "arbitrary")),
        name="retention_init" if has_init else "retention_zero",
    )(*args)


def _qkv_kernel(x_ref, nw_ref, sh_ref, sc_ref, w_ref, qn_ref, kn_ref, cos_ref, sa_ref, sb_ref,
                o_ref, kv_ref, *, dh):
    hb = _norm_mod(x_ref[...], nw_ref[...], sh_ref[...], sc_ref[...]).astype(BF16)
    nq = ATT_HEADS * dh
    nk = ATT_KV_HEADS * dh
    cos, sa, sb = cos_ref[...], sa_ref[...], sb_ref[...]

    def norm_rope(a, w):
        a = a * lax.rsqrt(jnp.mean(a * a, axis=-1, keepdims=True) + EPS) * w
        roped = a * cos + pltpu.roll(a, dh - dh // 4, 1) * sa + pltpu.roll(a, dh // 4, 1) * sb
        return a, roped

    q = _dot(hb, w_ref[:, :nq])
    for hh in range(ATT_HEADS):
        _, r = norm_rope(q[:, hh * dh:(hh + 1) * dh], qn_ref[...])
        o_ref[:, hh * dh:(hh + 1) * dh] = r.astype(BF16)
    kvv = _dot(hb, w_ref[:, nq:])
    for hh in range(ATT_KV_HEADS):
        a, r = norm_rope(kvv[:, hh * dh:(hh + 1) * dh], kn_ref[...])
        o_ref[:, nq + hh * dh:nq + (hh + 1) * dh] = r.astype(BF16)
        kv_ref[:, hh * dh:(hh + 1) * dh] = a
    o_ref[:, nq + nk:] = kvv[:, nk:].astype(BF16)
    kv_ref[:, nk:] = kvv[:, nk:]


def _rope_tables(l, dh, rotate):
    axis_dim = dh // 2
    half = axis_dim // 2
    t = jnp.arange(l)
    row = (t // GRID_W).astype(F32)
    col = (t % GRID_W).astype(F32)
    inv = ROPE_THETA ** (-jnp.arange(0, axis_dim, 2, dtype=F32) / axis_dim)
    d = jnp.arange(dh)
    pos = jnp.where((d // axis_dim)[None, :] == 0, row[:, None], col[:, None])
    ang = pos * inv[d % half][None, :]
    first = ((d % axis_dim) < half)[None, :]
    cos, sin = jnp.cos(ang), jnp.sin(ang)
    if not rotate:
        cos, sin = jnp.ones_like(cos), jnp.zeros_like(sin)
    return cos, jnp.where(first, -sin, 0.0), jnp.where(first, 0.0, sin)


def _qkv_proj(y, ns, mod, layer, norm_w, w_bf16, qn, kn, dh):
    nb, l, d = y.shape
    n = w_bf16.shape[1]
    nkv = 2 * ATT_KV_HEADS * dh
    tm = _pick(l, (512, 256, 128))
    tabs = [jnp.stack([a, b]) for a, b in zip(_rope_tables(l, dh, True), _rope_tables(l, dh, False))]
    tab_spec = pl.BlockSpec((None, tm, dh), lambda b, i: (jnp.where(b < ns, 0, 1), i, 0))
    return pl.pallas_call(
        functools.partial(_qkv_kernel, dh=dh),
        out_shape=[jax.ShapeDtypeStruct((nb, l, n), BF16), jax.ShapeDtypeStruct((nb, l, nkv), F32)],
        grid=(nb, l // tm),
        in_specs=[
            pl.BlockSpec((None, tm, d), lambda b, i: (b, i, 0)),
            pl.BlockSpec((1, d), lambda b, i: (0, 0)),
            _mod_spec(d, layer, 0),
            _mod_spec(d, layer, 1),
            pl.BlockSpec((d, n), lambda b, i: (0, 0), pipeline_mode=pl.Buffered(1)),
            pl.BlockSpec((1, dh), lambda b, i: (0, 0)),
            pl.BlockSpec((1, dh), lambda b, i: (0, 0)),
            tab_spec, tab_spec, tab_spec,
        ],
        out_specs=[pl.BlockSpec((None, tm, n), lambda b, i: (b, i, 0)),
                   pl.BlockSpec((None, tm, nkv), lambda b, i: (b, i, 0))],
        compiler_params=_params(("arbitrary", "arbitrary")),
        name="qkv_proj",
    )(y, norm_w.reshape(1, d), mod, mod, w_bf16, qn.reshape(1, dh), kn.reshape(1, dh), *tabs)


def _attn_kernel(*refs, dh, group, has_cache, scale):
    if has_cache:
        q_ref, k_ref, v_ref, kc_ref, vc_ref, o_ref, kall, vext = refs
        past = kc_ref.shape[0]
    else:
        q_ref, k_ref, v_ref, o_ref, kall, vext = refs
        past = 0

    @pl.when(pl.program_id(2) == 0)
    def _():
        if has_cache:
            kall[:past, :] = kc_ref[...].astype(BF16)
            vext[:past, :dh] = vc_ref[...].astype(BF16)
        kall[past:, :] = k_ref[...]
        vext[past:, :dh] = v_ref[...]
        vext[:, dh:] = jnp.ones((vext.shape[0], dh), BF16)

    c = scale * 1.4426950408889634
    tq = q_ref.shape[0]
    q = jnp.concatenate([q_ref[:, g * dh:(g + 1) * dh] for g in range(group)], axis=0)
    lk = kall.shape[0]
    ck = _pick(lk, (512, 256, 128))
    m = None
    for j in range(lk // ck):
        s = _dot_nt(q, kall[j * ck:(j + 1) * ck, :])
        mj = jnp.max(s, axis=-1, keepdims=True)
        vj = vext[j * ck:(j + 1) * ck, :]
        if m is None:
            m = mj
            o = _dot(jnp.exp2((s - m) * c).astype(BF16), vj)
        else:
            m_new = jnp.maximum(m, mj)
            o = o * jnp.exp2((m - m_new) * c) + _dot(jnp.exp2((s - m_new) * c).astype(BF16), vj)
            m = m_new
    for g in range(group):
        og = o[g * tq:(g + 1) * tq]
        o_ref[:, g * dh:(g + 1) * dh] = (og[:, :dh] / og[:, dh:dh + 1]).astype(BF16)


def _attention(qkv, nseq, seq0, ls, dh, cache_k=None, cache_v=None):
    group = ATT_HEADS // ATT_KV_HEADS
    has_cache = cache_k is not None
    tq = _pick(ls, (256, 128))
    kcol = ATT_HEADS
    vcol = ATT_HEADS + ATT_KV_HEADS
    in_specs = [
        pl.BlockSpec((None, tq, group * dh), lambda s, kh, i: (s + seq0, i, kh)),
        pl.BlockSpec((None, ls, dh), lambda s, kh, i: (s + seq0, 0, kcol + kh)),
        pl.BlockSpec((None, ls, dh), lambda s, kh, i: (s + seq0, 0, vcol + kh)),
    ]
    args = [qkv, qkv, qkv]
    past = 0
    if has_cache:
        past = cache_k.shape[1]
        in_specs += [pl.BlockSpec((None, past, dh), lambda s, kh, i: (s, 0, kh))] * 2
        args += [cache_k, cache_v]
    kern = functools.partial(_attn_kernel, dh=dh, group=group, has_cache=has_cache, scale=float(dh) ** -0.5)
    return pl.pallas_call(
        kern,
        out_shape=jax.ShapeDtypeStruct((nseq, ls, ATT_HEADS * dh), BF16),
        grid=(nseq, ATT_KV_HEADS, ls // tq),
        in_specs=in_specs,
        out_specs=pl.BlockSpec((None, tq, group * dh), lambda s, kh, i: (s, i, kh)),
        scratch_shapes=[pltpu.VMEM((past + ls, dh), BF16), pltpu.VMEM((past + ls, 2 * dh), BF16)],
        compiler_params=_params(("arbitrary", "arbitrary", "arbitrary")),
        name="attention_cache" if has_cache else "attention_self",
    )(*args)


def _pack_rows(h):
    half = h.shape[1] // 2
    u = lax.bitcast_convert_type(h.astype(BF16).astype(F32), U32)
    return (u[:, :half] >> 16) | (u[:, half:] & jnp.uint32(0xFFFF0000))


def _unpack_rows(p):
    lo = lax.bitcast_convert_type(p << 16, F32)
    hi = lax.bitcast_convert_type(p & jnp.uint32(0xFFFF0000), F32)
    return jnp.concatenate([lo, hi], axis=1).astype(BF16)


def _outproj_kernel(os_ref, op_ref, w_ref, y_ref, g1_ref, nw_ref, sh_ref, sc_ref, rwh_ref, rwl_ref,
                    yo_ref, h_ref, lg_ref, *, ns):
    is_prompt = pl.program_id(0) >= ns
    o = jnp.where(is_prompt, op_ref[...], os_ref[...])
    y = y_ref[...] + g1_ref[...] * _dot(o, w_ref[...])
    yo_ref[...] = y
    h = _norm_mod(y, nw_ref[...], sh_ref[...], sc_ref[...])
    h_ref[...] = _pack_rows(h)
    h_hi = h.astype(BF16)
    h_lo = (h - h_hi.astype(F32)).astype(BF16)
    rwh = rwh_ref[...]
    lg_ref[...] = _dot_nt(rwh, h_hi) + _dot_nt(rwl_ref[...], h_hi) + _dot_nt(rwh, h_lo)


def _outproj(o_s, o_p, w_bf16, y, mod, layer, norm_w, rw_hi, rw_lo):
    nb, l, d = y.shape
    ns = o_s.shape[0]
    kdim = o_s.shape[2]
    ne = rw_hi.shape[0]
    tm = _pick(l, (512, 256, 128))
    ni = l // tm
    dp = jax.eval_shape(_pack_rows, jax.ShapeDtypeStruct((8, d), F32)).shape[1]
    return pl.pallas_call(
        functools.partial(_outproj_kernel, ns=ns),
        out_shape=[jax.ShapeDtypeStruct((nb, l, d), F32), jax.ShapeDtypeStruct((nb, l, dp), U32),
                   jax.ShapeDtypeStruct((nb, ne, l), F32)],
        grid=(nb, ni),
        in_specs=[
            pl.BlockSpec((None, tm, kdim), lambda b, i: (jnp.minimum(b, ns - 1), jnp.where(b < ns, i, ni - 1), 0)),
            pl.BlockSpec((None, tm, kdim), lambda b, i: (0, jnp.where(b < ns, 0, i), 0)),
            pl.BlockSpec((kdim, d), lambda b, i: (0, 0), pipeline_mode=pl.Buffered(1)),
            pl.BlockSpec((None, tm, d), lambda b, i: (b, i, 0)),
            _mod_spec(d, layer, 2),
            pl.BlockSpec((1, d), lambda b, i: (0, 0)),
            _mod_spec(d, layer, 3),
            _mod_spec(d, layer, 4),
            pl.BlockSpec((ne, d), lambda b, i: (0, 0)),
            pl.BlockSpec((ne, d), lambda b, i: (0, 0)),
        ],
        out_specs=[pl.BlockSpec((None, tm, d), lambda b, i: (b, i, 0)),
                   pl.BlockSpec((None, tm, dp), lambda b, i: (b, i, 0)),
                   pl.BlockSpec((None, ne, tm), lambda b, i: (b, 0, i))],
        compiler_params=_params(("arbitrary", "arbitrary")),
        name="outproj_router",
    )(o_s, o_p, w_bf16, y, mod, norm_w.reshape(1, d), mod, mod, rw_hi, rw_lo)


def _router_kernel(lg_ref, b_ref, bkt_ref, rank_ref, wt_ref, cnt_ref, carry_ref):
    ng = N_GROUPS
    sc = jax.nn.sigmoid(lg_ref[...])
    sel = sc + b_ref[...]
    sj = [sel[j * ng:(j + 1) * ng] for j in range(EXPERTS_PER_GROUP)]
    pj = [sc[j * ng:(j + 1) * ng] for j in range(EXPERTS_PER_GROUP)]
    hi1, lo1 = jnp.maximum(sj[0], sj[1]), jnp.minimum(sj[0], sj[1])
    hi2, lo2 = jnp.maximum(sj[2], sj[3]), jnp.minimum(sj[2], sj[3])
    top1 = jnp.maximum(hi1, hi2)
    top2 = jnp.maximum(jnp.minimum(hi1, hi2), jnp.maximum(lo1, lo2))
    gs = top1 + top2
    gmax = jnp.max(gs, axis=0, keepdims=True)
    gi = lax.broadcasted_iota(I32, gs.shape, 0)
    best = jnp.min(jnp.where(gs == gmax, gi, ng), axis=0, keepdims=True)
    onehot = gi == best
    neg = -jnp.inf
    vj = [jnp.max(jnp.where(onehot, a, neg), axis=0, keepdims=True) for a in sj]
    wj = [jnp.max(jnp.where(onehot, a, neg), axis=0, keepdims=True) for a in pj]

    def argmax4(vals):
        m, i = vals[0], jnp.zeros(vals[0].shape, I32)
        for j in range(1, EXPERTS_PER_GROUP):
            gt = vals[j] > m
            m = jnp.where(gt, vals[j], m)
            i = jnp.where(gt, j, i)
        return i

    i1 = argmax4(vj)
    i2 = argmax4([jnp.where(i1 == j, neg, vj[j]) for j in range(EXPERTS_PER_GROUP)])

    def take(vals, i):
        out = vals[0]
        for j in range(1, EXPERTS_PER_GROUP):
            out = jnp.where(i == j, vals[j], out)
        return out

    w1, w2 = take(wj, i1), take(wj, i2)
    tot = w1 + w2
    w1, w2 = w1 / tot, w2 / tot

    lo, hi = jnp.minimum(i1, i2), jnp.maximum(i1, i2)
    pair = jnp.where(lo == 0, hi - 1, jnp.where(lo == 1, jnp.where(hi == 3, 3, 4), 5))
    slot_a = jnp.where(pair < 3, 0, jnp.where(pair < 5, 1, 3))
    bucket = best * N_PAIRS + pair
    a_first = i1 == slot_a
    w_a = jnp.where(a_first, w1, w2)
    w_b = jnp.where(a_first, w2, w1)

    tl = bucket.shape[1]
    nbk = carry_ref.shape[0]

    @pl.when(jnp.logical_and(pl.program_id(0) == 0, pl.program_id(1) == 0))
    def _():
        carry_ref[...] = jnp.zeros_like(carry_ref)

    onehot_b = lax.broadcasted_iota(I32, (nbk, tl), 0) == bucket
    upper = (lax.broadcasted_iota(I32, (tl, tl), 0) <= lax.broadcasted_iota(I32, (tl, tl), 1))
    prefix = _dot(onehot_b.astype(BF16), upper.astype(BF16))
    carry = carry_ref[...]
    rank = jnp.sum(jnp.where(onehot_b, prefix - 1.0 + carry, 0.0), axis=0, keepdims=True)
    carry = carry + prefix[:, tl - 1:tl]
    carry_ref[...] = carry
    bkt_ref[...] = bucket
    rank_ref[...] = rank.astype(I32)
    cnt_ref[...] = jnp.broadcast_to(carry, cnt_ref.shape)
    rows = lax.broadcasted_iota(I32, (wt_ref.shape[1], tl), 0)
    wt_ref[...] = jnp.where(rows == 0, w_a, jnp.where(rows == 1, w_b, 0.0)).T


def _router(logits, bias_perm):
    nb, ne, l = logits.shape
    tl = _pick(l, (512, 256, 128))
    nbk = N_GROUPS * N_PAIRS
    return pl.pallas_call(
        _router_kernel,
        out_shape=[jax.ShapeDtypeStruct((nb, 1, l), I32), jax.ShapeDtypeStruct((nb, 1, l), I32),
                   jax.ShapeDtypeStruct((nb, l, 128), F32), jax.ShapeDtypeStruct((nbk, 128), F32)],
        grid=(nb, l // tl),
        in_specs=[pl.BlockSpec((None, ne, tl), lambda b, i: (b, 0, i)),
                  pl.BlockSpec((ne, 1), lambda b, i: (0, 0))],
        out_specs=[pl.BlockSpec((None, 1, tl), lambda b, i: (b, 0, i)),
                   pl.BlockSpec((None, 1, tl), lambda b, i: (b, 0, i)),
                   pl.BlockSpec((None, tl, 128), lambda b, i: (b, i, 0)),
                   pl.BlockSpec((nbk, 128), lambda b, i: (0, 0))],
        scratch_shapes=[pltpu.VMEM((nbk, 1), F32)],
        compiler_params=_params(("arbitrary", "arbitrary")),
        name="router_rank",
    )(logits, bias_perm.reshape(ne, 1))


def _moe_kernel(ea_ref, eb_ref, nu_ref, x_ref, wga_ref, wua_ref, wda_ref, wgb_ref, wub_ref, wdb_ref, o_ref):
    del ea_ref, eb_ref
    i = pl.program_id(0)
    d = o_ref.shape[1] // 2

    @pl.when(i < nu_ref[0])
    def _():
        x = _unpack_rows(x_ref[...])

        def expert(wg, wu, wd):
            hid = jax.nn.silu(_dot(x, wg[...])) * _dot(x, wu[...])
            return _dot(hid.astype(BF16), wd[...])

        o_ref[:, :d] = expert(wga_ref, wua_ref, wda_ref)
        o_ref[:, d:] = expert(wgb_ref, wub_ref, wdb_ref)

    @pl.when(i >= nu_ref[0])
    def _():
        o_ref[...] = jnp.zeros_like(o_ref)


def _moe_experts(xs, blk_a, blk_b, n_used, wg, wu, wd, bm):
    p, dp = xs.shape
    _, d, de = wg.shape
    nblk = p // bm
    w_in = lambda sel: pl.BlockSpec((None, d, de), lambda i, ea, eb, nu: ((ea, eb)[sel][i], 0, 0))
    w_out = lambda sel: pl.BlockSpec((None, de, d), lambda i, ea, eb, nu: ((ea, eb)[sel][i], 0, 0))
    grid_spec = pltpu.PrefetchScalarGridSpec(
        num_scalar_prefetch=3,
        grid=(nblk,),
        in_specs=[pl.BlockSpec((bm, dp), lambda i, ea, eb, nu: (i, 0)),
                  w_in(0), w_in(0), w_out(0), w_in(1), w_in(1), w_out(1)],
        out_specs=pl.BlockSpec((bm, 2 * d), lambda i, ea, eb, nu: (i, 0)),
    )
    return pl.pallas_call(
        _moe_kernel,
        out_shape=jax.ShapeDtypeStruct((p, 2 * d), F32),
        grid_spec=grid_spec,
        compiler_params=_params(("arbitrary",)),
        name="moe_experts",
    )(blk_a, blk_b, n_used, xs, wg, wu, wd, wg, wu, wd)


def _row_plan(bucket, rank, counts, bm):
    t = bucket.size
    nbk = N_GROUPS * N_PAIRS
    cnt = counts[:, 0].astype(I32)
    padded = (cnt + bm - 1) // bm * bm
    pends = jnp.cumsum(padded)
    pstarts = pends - padded
    onehot = bucket.reshape(t, 1) == jnp.arange(nbk, dtype=I32)[None, :]
    dest = jnp.sum(jnp.where(onehot, pstarts[None, :], 0), axis=1) + rank.reshape(t)
    p = (t + bm - 1) // bm * bm + nbk * bm
    nblk = p // bm
    blk_bucket = jnp.clip(jnp.searchsorted(pends, jnp.arange(nblk, dtype=I32) * bm, side='right'), 0, nbk - 1)
    grp, pair = blk_bucket // N_PAIRS, blk_bucket % N_PAIRS
    blk_a = grp * EXPERTS_PER_GROUP + jnp.asarray(_PAIR_A, I32)[pair]
    blk_b = grp * EXPERTS_PER_GROUP + jnp.asarray(_PAIR_B, I32)[pair]
    n_used = (pends[-1:] // bm).astype(I32)
    return dest.astype(I32), blk_a.astype(I32), blk_b.astype(I32), n_used, p


def _scatter_kernel(dest_ref, src_ref, init_hbm, out_hbm, sem, *, rows):
    del init_hbm
    base = pl.program_id(0) * rows

    def issue(r, carry):
        pltpu.make_async_copy(src_ref.at[r], out_hbm.at[dest_ref[base + r]], sem).start()
        return carry

    lax.fori_loop(0, rows, issue, 0, unroll=8)
    pltpu.make_async_copy(src_ref, out_hbm.at[pl.ds(0, rows)], sem).wait()


def _scatter_rows(h_rows, dest, p):
    t, dp = h_rows.shape
    rows = _pick(t, (2048, 1024, 512))
    grid_spec = pltpu.PrefetchScalarGridSpec(
        num_scalar_prefetch=1,
        grid=(t // rows,),
        in_specs=[pl.BlockSpec((rows, dp), lambda i, dst: (i, 0)), pl.BlockSpec(memory_space=pl.ANY)],
        out_specs=pl.BlockSpec(memory_space=pl.ANY),
        scratch_shapes=[pltpu.SemaphoreType.DMA],
    )
    return pl.pallas_call(
        functools.partial(_scatter_kernel, rows=rows),
        out_shape=jax.ShapeDtypeStruct((p, dp), h_rows.dtype),
        grid_spec=grid_spec,
        input_output_aliases={2: 0},
        compiler_params=_params(("arbitrary",)),
        name="moe_scatter_rows",
    )(dest, h_rows, jnp.zeros((p, dp), h_rows.dtype))


def _combine_kernel(dest_ref, y_ref, g_ref, wt_ref, m_hbm, o_ref, mbuf, sem, *, tm, nsteps):
    step = pl.program_id(0) * pl.num_programs(1) + pl.program_id(1)
    slot = step % 2
    d = y_ref.shape[1]

    def gather(s, sl):
        def issue(r, carry):
            pltpu.make_async_copy(m_hbm.at[dest_ref[s * tm + r]], mbuf.at[sl, r], sem.at[sl]).start()
            return carry
        lax.fori_loop(0, tm, issue, 0, unroll=8)

    @pl.when(step == 0)
    def _():
        gather(0, 0)

    @pl.when(step + 1 < nsteps)
    def _():
        gather(step + 1, 1 - slot)

    pltpu.make_async_copy(m_hbm.at[pl.ds(0, tm)], mbuf.at[slot], sem.at[slot]).wait()
    m = mbuf[slot]
    w = wt_ref[...]
    o_ref[...] = y_ref[...] + g_ref[...] * (w[:, 0:1] * m[:, :d] + w[:, 1:2] * m[:, d:])


def _combine(y, mod, layer, moe_rows, dest, wt):
    nb, l, d = y.shape
    tm = _pick(l, (512, 256, 128))
    ni = l // tm
    grid_spec = pltpu.PrefetchScalarGridSpec(
        num_scalar_prefetch=1,
        grid=(nb, ni),
        in_specs=[
            pl.BlockSpec((None, tm, d), lambda b, i, dst: (b, i, 0)),
            pl.BlockSpec((None, None, None, 1, d), lambda b, i, dst: (layer, 5, b, 0, 0)),
            pl.BlockSpec((None, tm, 128), lambda b, i, dst: (b, i, 0)),
            pl.BlockSpec(memory_space=pl.ANY),
        ],
        out_specs=pl.BlockSpec((None, tm, d), lambda b, i, dst: (b, i, 0)),
        scratch_shapes=[pltpu.VMEM((2, tm, 2 * d), F32), pltpu.SemaphoreType.DMA((2,))],
    )
    return pl.pallas_call(
        functools.partial(_combine_kernel, tm=tm, nsteps=nb * ni),
        out_shape=jax.ShapeDtypeStruct((nb, l, d), F32),
        grid_spec=grid_spec,
        compiler_params=_params(("arbitrary", "arbitrary")),
        name="moe_combine",
    )(dest, y, mod, wt, moe_rows)


def kernel(x_prompt, x_sample, c, state_ret_fwd, state_ret_bwd, cache_attn_k, cache_attn_v, c_ctx,
           ada_w, ada_b, norm1_w, norm2_w, ret_w_in, ret_w_out, ret_gn_w, ret_decay_fwd, ret_decay_bwd,
           attn_w_qkv, attn_w_o, attn_q_norm, attn_k_norm, router_w, router_b,
           moe_w_gate, moe_w_up, moe_w_down):
    n_prompt, seq, d = x_prompt.shape
    ns, l, _ = x_sample.shape
    assert n_prompt * seq == l, "prompt tokens must fill exactly one sample-length row"
    depth = ada_w.shape[0]
    nb = ns + 1
    dk = d // RET_HEADS
    dv = 2 * dk
    dh = d // ATT_HEADS
    bm = MOE_ROW_BLOCK

    y = jnp.concatenate([x_sample, x_prompt.reshape(1, l, d)], axis=0)
    nbp = (nb + 7) // 8 * 8
    cond = jnp.zeros((nbp, d), F32).at[:ns].set(c).at[ns].set(c_ctx)
    mod = _ada_params(cond, ada_w, ada_b)

    perm = (jnp.arange(N_EXPERTS) % N_GROUPS) * EXPERTS_PER_GROUP + jnp.arange(N_EXPERTS) // N_GROUPS
    rw_t = router_w.T[perm]
    rw_hi = rw_t.astype(BF16)
    rw_lo = (rw_t - rw_hi.astype(F32)).astype(BF16)
    rb_perm = router_b[perm]

    ret_f, ret_b, k_new, v_new = [], [], [], []
    for i in range(depth):
        if i % 2 == 0:
            r = i // 2
            qkvg = _ret_inproj(y, mod, i, norm1_w[i], ret_w_in[r].astype(BF16), dk)
            o_s = _retention(qkvg, ns, 0, l, dk, ret_gn_w[r], ret_decay_fwd[r], ret_decay_bwd[r],
                             s0f=state_ret_fwd[:, r], s0b=state_ret_bwd[:, r])
            qkvg_p = qkvg.reshape(nb * n_prompt, seq, qkvg.shape[2])
            o_p, s_f, s_b = _retention(qkvg_p, n_prompt, ns * n_prompt, seq, dk, ret_gn_w[r],
                                       ret_decay_fwd[r], ret_decay_bwd[r], emit_final=True)
            o_s = o_s[0] if isinstance(o_s, (list, tuple)) else o_s
            ret_f.append(s_f)
            ret_b.append(s_b)
            w_o = ret_w_out[r].astype(BF16)
        else:
            a = i // 2
            qkv, kvf = _qkv_proj(y, ns, mod, i, norm1_w[i], attn_w_qkv[a].astype(BF16),
                                 attn_q_norm[a], attn_k_norm[a], dh)
            past = cache_attn_k.shape[2]
            o_s = _attention(qkv, ns, 0, l, dh,
                             cache_k=cache_attn_k[:, a].reshape(ns, past, ATT_KV_HEADS * dh),
                             cache_v=cache_attn_v[:, a].reshape(ns, past, ATT_KV_HEADS * dh))
            o_p = _attention(qkv.reshape(nb * n_prompt, seq, qkv.shape[2]), n_prompt, ns * n_prompt, seq, dh)
            nk = ATT_KV_HEADS * dh
            k_new.append(kvf[ns, :, :nk].reshape(n_prompt, seq, ATT_KV_HEADS, dh))
            v_new.append(kvf[ns, :, nk:].reshape(n_prompt, seq, ATT_KV_HEADS, dh))
            w_o = attn_w_o[a].astype(BF16)
        o_p = o_p.reshape(1, l, o_p.shape[-1])
        y, h2, logits = _outproj(o_s, o_p, w_o, y, mod, i, norm2_w[i], rw_hi, rw_lo)
        bucket, rank, wt, counts = _router(logits, rb_perm)
        dest, blk_a, blk_b, n_used, p = _row_plan(bucket, rank, counts, bm)
        xs = _scatter_rows(h2.reshape(nb * l, h2.shape[2]), dest, p)
        moe_rows = _moe_experts(xs, blk_a, blk_b, n_used, moe_w_gate[i].astype(BF16),
                                moe_w_up[i].astype(BF16), moe_w_down[i].astype(BF16), bm)
        y = _combine(y, mod, i, moe_rows, dest, wt)

    y_p = y[ns].reshape(n_prompt, seq, d)
    y_s = y[:ns]
    return (y_p, y_s, jnp.stack(ret_f, axis=1), jnp.stack(ret_b, axis=1),
            jnp.stack(k_new, axis=1), jnp.stack(v_new, axis=1))
```

```python
import functools

import jax
import jax.numpy as jnp
from jax import lax
from jax.experimental import pallas as pl
from jax.experimental.pallas import tpu as pltpu

F32 = jnp.float32
BF16 = jnp.bfloat16
I32 = jnp.int32
U32 = jnp.uint32

EPS = 1e-6
GRID_W = 64
RET_HEADS = 4
ATT_HEADS = 8
ATT_KV_HEADS = 2
ROPE_THETA = 10000.0
N_EXPERTS = 32
N_GROUPS = 8
EXPERTS_PER_GROUP = N_EXPERTS // N_GROUPS
TOP_K = 2
_PAIR_A = (0, 0, 0, 1, 1, 3)
_PAIR_B = (1, 2, 3, 3, 2, 2)
N_PAIRS = len(_PAIR_A)
MOE_ROW_BLOCK = 256

V7X_VMEM_BYTES = 64 * 1024 * 1024
VMEM_LIMIT_BYTES = 56 * 1024 * 1024


def _params(sem):
    return pltpu.CompilerParams(dimension_semantics=sem, vmem_limit_bytes=VMEM_LIMIT_BYTES)


def _dot(a, b):
    return jnp.dot(a, b, preferred_element_type=F32)


def _dot_nt(a, b):
    return lax.dot_general(a, b, (((1,), (1,)), ((), ())), preferred_element_type=F32)


def _dot_tn(a, b):
    return lax.dot_general(a, b, (((0,), (0,)), ((), ())), preferred_element_type=F32)


def _pick(n, prefs):
    for p in prefs:
        if n % p == 0:
            return p
    return n


def _ada_kernel(c_ref, w_ref, b_ref, o_ref):
    c = c_ref[...]
    s = jax.nn.silu(c).astype(BF16)
    o_ref[...] = _dot(s, w_ref[...].astype(BF16)) + b_ref[...]


def _ada_params(cond, ada_w, ada_b):
    depth, d, n6 = ada_w.shape
    nbp = cond.shape[0]
    tn = _pick(n6, (1536, 1024, 512, 256, 128))
    out = pl.pallas_call(
        _ada_kernel,
        out_shape=jax.ShapeDtypeStruct((depth, nbp, n6), F32),
        grid=(depth, n6 // tn),
        in_specs=[
            pl.BlockSpec((nbp, d), lambda l, j: (0, 0)),
            pl.BlockSpec((None, d, tn), lambda l, j: (l, 0, j)),
            pl.BlockSpec((None, 1, tn), lambda l, j: (l, 0, j)),
        ],
        out_specs=pl.BlockSpec((None, nbp, tn), lambda l, j: (l, 0, j)),
        compiler_params=_params(("arbitrary", "arbitrary")),
        name="ada_params",
    )(cond, ada_w, ada_b.reshape(depth, 1, n6))
    return out.reshape(depth, nbp, 6, 1, d).transpose(0, 2, 1, 3, 4)


def _mod_spec(d, layer, j):
    return pl.BlockSpec((None, None, None, 1, d), lambda b, i: (layer, j, b, 0, 0))


def _norm_mod(x, nw, sh, sc):
    ms = jnp.mean(x * x, axis=-1, keepdims=True)
    h = x * lax.rsqrt(ms + EPS) * nw
    return h * (1.0 + sc) + sh


def _ret_inproj_kernel(x_ref, nw_ref, sh_ref, sc_ref, w_ref, gn_ref, o_ref, *, hq, hv, tn, k_scale):
    hb = _norm_mod(x_ref[...], nw_ref[...], sh_ref[...], sc_ref[...]).astype(BF16)
    n = w_ref.shape[1]
    for j in range(n // tn):
        lo = j * tn
        acc = _dot(hb, w_ref[:, lo:lo + tn])
        if hq <= lo < 2 * hq:
            acc = acc * k_scale
        elif lo >= 2 * hq + hv:
            g0 = (lo - 2 * hq - hv) % hv
            acc = jax.nn.silu(acc) * gn_ref[:, g0:g0 + tn]
        o_ref[:, lo:lo + tn] = acc.astype(BF16)


def _ret_inproj(y, mod, layer, norm_w, w_in_bf16, gn_w, dk):
    nb, l, d = y.shape
    n = w_in_bf16.shape[1]
    hq = RET_HEADS * dk
    hv = 2 * hq
    tm = _pick(l, (512, 256, 128))
    tn = _pick(hq, (1024, 512, 256, 128))
    kern = functools.partial(_ret_inproj_kernel, hq=hq, hv=hv, tn=tn, k_scale=float(dk) ** -0.5)
    return pl.pallas_call(
        kern,
        out_shape=jax.ShapeDtypeStruct((nb, l, n), BF16),
        grid=(nb, l // tm),
        in_specs=[
            pl.BlockSpec((None, tm, d), lambda b, i: (b, i, 0)),
            pl.BlockSpec((1, d), lambda b, i: (0, 0)),
            _mod_spec(d, layer, 0),
            _mod_spec(d, layer, 1),
            pl.BlockSpec((d, n), lambda b, i: (0, 0), pipeline_mode=pl.Buffered(1)),
            pl.BlockSpec((1, hv), lambda b, i: (0, 0)),
        ],
        out_specs=pl.BlockSpec((None, tm, n), lambda b, i: (b, i, 0)),
        compiler_params=_params(("arbitrary", "arbitrary")),
        name="ret_inproj",
    )(y, norm_w.reshape(1, d), mod, mod, w_in_bf16, gn_w.reshape(1, hv))


def _log_sigmoid(x):
    return jnp.minimum(x, 0.0) - jnp.log1p(jnp.exp(-jnp.abs(x)))


def _ret_kernel(*refs, ls, c, has_init, emit_final):
    it = iter(refs)
    q_ref, k_ref, v_ref, gf_ref, gb_ref, df_ref, db_ref = (next(it) for _ in range(7))
    if has_init:
        s0f_ref, s0b_ref = next(it), next(it)
    y_ref = next(it)
    if emit_final:
        sf_out, sb_out = next(it), next(it)
    s_scr, sb_scr = next(it), next(it)

    dk = q_ref.shape[1]
    nchunks = ls // c
    lg_f = _log_sigmoid(df_ref[...])[:, 0:1]
    lg_b = _log_sigmoid(db_ref[...])[:, 0:1]

    ri = lax.broadcasted_iota(I32, (c, c), 0)
    ci = lax.broadcasted_iota(I32, (c, c), 1)
    dif = (ri - ci).astype(F32)
    decay_f = jnp.where(dif >= 0, jnp.exp(jnp.maximum(dif, 0.0) * lg_f), 0.0)
    decay_b = jnp.where(dif <= 0, jnp.exp(jnp.maximum(-dif, 0.0) * lg_b), 0.0)
    pos = lax.broadcasted_iota(I32, (c, dk), 0).astype(F32)
    qdec_f = jnp.exp((pos + 1.0) * lg_f)
    kdec_f = jnp.exp((c - 1.0 - pos) * lg_f)
    qdec_b = jnp.exp((c - pos) * lg_b)
    kdec_b = jnp.exp(pos * lg_b)
    cdec_f = jnp.exp(c * lg_f)
    cdec_b = jnp.exp(c * lg_b)

    if has_init:
        s_scr[...] = s0b_ref[...]
    else:
        s_scr[...] = jnp.zeros_like(s_scr)

    def bwd_body(t, carry):
        n = nchunks - 1 - t
        r0 = pl.multiple_of(n * c, c)
        s = s_scr[...]
        sb_scr[n] = s.astype(BF16)
        kc = k_ref[pl.ds(r0, c), :].astype(F32)
        vc = v_ref[pl.ds(r0, c), :]
        s_scr[...] = s * cdec_b + _dot_tn((kc * kdec_b).astype(BF16), vc)
        return carry

    lax.fori_loop(0, nchunks, bwd_body, 0)
    if emit_final:
        sb_out[...] = s_scr[...]

    if has_init:
        s_scr[...] = s0f_ref[...]
    else:
        s_scr[...] = jnp.zeros_like(s_scr)

    def head_norm(o):
        mu = jnp.mean(o, axis=-1, keepdims=True)
        dlt = o - mu
        var = jnp.mean(dlt * dlt, axis=-1, keepdims=True)
        return dlt * lax.rsqrt(var + EPS)

    def fwd_body(n, carry):
        r0 = pl.multiple_of(n * c, c)
        rows = pl.ds(r0, c)
        qc = q_ref[rows, :]
        kc = k_ref[rows, :]
        vc = v_ref[rows, :]
        qf = qc.astype(F32)
        sc = _dot_nt(qc, kc)
        s = s_scr[...]
        o_f = _dot((sc * decay_f).astype(BF16), vc) + _dot((qf * qdec_f).astype(BF16), s.astype(BF16))
        o_b = _dot((sc * decay_b).astype(BF16), vc) + _dot((qf * qdec_b).astype(BF16), sb_scr[n])
        s_scr[...] = s * cdec_f + _dot_tn((kc.astype(F32) * kdec_f).astype(BF16), vc)
        y = head_norm(o_f) * gf_ref[rows, :].astype(F32) + head_norm(o_b) * gb_ref[rows, :].astype(F32)
        y_ref[rows, :] = y.astype(BF16)
        return carry

    lax.fori_loop(0, nchunks, fwd_body, 0, unroll=2 if nchunks % 2 == 0 else 1)
    if emit_final:
        sf_out[...] = s_scr[...]


def _retention(qkvg, nseq, seq0, ls, dk, decay_f, decay_b, s0f=None, s0b=None, emit_final=False):
    h = RET_HEADS
    dv = 2 * dk
    c = _pick(ls, (256, 128))
    has_init = s0f is not None
    kq, kk, kv, kgf, kgb = 0, h, h, 2 * h, 3 * h
    in_specs = [
        pl.BlockSpec((None, ls, dk), lambda s, hh: (s + seq0, 0, kq + hh)),
        pl.BlockSpec((None, ls, dk), lambda s, hh: (s + seq0, 0, kk + hh)),
        pl.BlockSpec((None, ls, dv), lambda s, hh: (s + seq0, 0, kv + hh)),
        pl.BlockSpec((None, ls, dv), lambda s, hh: (s + seq0, 0, kgf + hh)),
        pl.BlockSpec((None, ls, dv), lambda s, hh: (s + seq0, 0, kgb + hh)),
        pl.BlockSpec((None, 1, 128), lambda s, hh: (hh, 0, 0)),
        pl.BlockSpec((None, 1, 128), lambda s, hh: (hh, 0, 0)),
    ]
    args = [qkvg, qkvg, qkvg, qkvg, qkvg,
            jnp.broadcast_to(decay_f.reshape(h, 1, 1), (h, 1, 128)),
            jnp.broadcast_to(decay_b.reshape(h, 1, 1), (h, 1, 128))]
    if has_init:
        in_specs += [pl.BlockSpec((None, None, dk, dv), lambda s, hh: (s, hh, 0, 0))] * 2
        args += [s0f, s0b]
    out_shape = [jax.ShapeDtypeStruct((nseq, ls, h * dv), BF16)]
    out_specs = [pl.BlockSpec((None, ls, dv), lambda s, hh: (s, 0, hh))]
    if emit_final:
        out_shape += [jax.ShapeDtypeStruct((nseq, h, dk, dv), F32)] * 2
        out_specs += [pl.BlockSpec((None, None, dk, dv), lambda s, hh: (s, hh, 0, 0))] * 2
    kern = functools.partial(_ret_kernel, ls=ls, c=c, has_init=has_init, emit_final=emit_final)
    return pl.pallas_call(
        kern,
        out_shape=out_shape,
        grid=(nseq, h),
        in_specs=in_specs,
        out_specs=out_specs,
        scratch_shapes=[pltpu.VMEM((dk, dv), F32), pltpu.VMEM((ls // c, dk, dv), BF16)],
        compiler_params=_params(("arbitrary", "arbitrary")),
        name="retention_init" if has_init else "retention_zero",
    )(*args)


def _qkv_kernel(x_ref, nw_ref, sh_ref, sc_ref, w_ref, qn_ref, kn_ref, cos_ref, sa_ref, sb_ref,
                o_ref, kv_ref, *, dh):
    hb = _norm_mod(x_ref[...], nw_ref[...], sh_ref[...], sc_ref[...]).astype(BF16)
    nq = ATT_HEADS * dh
    nk = ATT_KV_HEADS * dh
    cos, sa, sb = cos_ref[...], sa_ref[...], sb_ref[...]

    def norm_rope(a, w):
        a = a * lax.rsqrt(jnp.mean(a * a, axis=-1, keepdims=True) + EPS) * w
        roped = a * cos + pltpu.roll(a, dh - dh // 4, 1) * sa + pltpu.roll(a, dh // 4, 1) * sb
        return a, roped

    q = _dot(hb, w_ref[:, :nq])
    for hh in range(ATT_HEADS):
        _, r = norm_rope(q[:, hh * dh:(hh + 1) * dh], qn_ref[...])
        o_ref[:, hh * dh:(hh + 1) * dh] = r.astype(BF16)
    kvv = _dot(hb, w_ref[:, nq:])
    for hh in range(ATT_KV_HEADS):
        a, r = norm_rope(kvv[:, hh * dh:(hh + 1) * dh], kn_ref[...])
        o_ref[:, nq + hh * dh:nq + (hh + 1) * dh] = r.astype(BF16)
        kv_ref[:, hh * dh:(hh + 1) * dh] = a
    o_ref[:, nq + nk:] = kvv[:, nk:].astype(BF16)
    kv_ref[:, nk:] = kvv[:, nk:]


def _rope_tables(l, dh, rotate):
    axis_dim = dh // 2
    half = axis_dim // 2
    t = jnp.arange(l)
    row = (t // GRID_W).astype(F32)
    col = (t % GRID_W).astype(F32)
    inv = ROPE_THETA ** (-jnp.arange(0, axis_dim, 2, dtype=F32) / axis_dim)
    d = jnp.arange(dh)
    pos = jnp.where((d // axis_dim)[None, :] == 0, row[:, None], col[:, None])
    ang = pos * inv[d % half][None, :]
    first = ((d % axis_dim) < half)[None, :]
    cos, sin = jnp.cos(ang), jnp.sin(ang)
    if not rotate:
        cos, sin = jnp.ones_like(cos), jnp.zeros_like(sin)
    return cos, jnp.where(first, -sin, 0.0), jnp.where(first, 0.0, sin)


def _qkv_proj(y, ns, mod, layer, norm_w, w_bf16, qn, kn, dh):
    nb, l, d = y.shape
    n = w_bf16.shape[1]
    nkv = 2 * ATT_KV_HEADS * dh
    tm = _pick(l, (512, 256, 128))
    tabs = [jnp.stack([a, b]) for a, b in zip(_rope_tables(l, dh, True), _rope_tables(l, dh, False))]
    tab_spec = pl.BlockSpec((None, tm, dh), lambda b, i: (jnp.where(b < ns, 0, 1), i, 0))
    return pl.pallas_call(
        functools.partial(_qkv_kernel, dh=dh),
        out_shape=[jax.ShapeDtypeStruct((nb, l, n), BF16), jax.ShapeDtypeStruct((nb, l, nkv), F32)],
        grid=(nb, l // tm),
        in_specs=[
            pl.BlockSpec((None, tm, d), lambda b, i: (b, i, 0)),
            pl.BlockSpec((1, d), lambda b, i: (0, 0)),
            _mod_spec(d, layer, 0),
            _mod_spec(d, layer, 1),
            pl.BlockSpec((d, n), lambda b, i: (0, 0), pipeline_mode=pl.Buffered(1)),
            pl.BlockSpec((1, dh), lambda b, i: (0, 0)),
            pl.BlockSpec((1, dh), lambda b, i: (0, 0)),
            tab_spec, tab_spec, tab_spec,
        ],
        out_specs=[pl.BlockSpec((None, tm, n), lambda b, i: (b, i, 0)),
                   pl.BlockSpec((None, tm, nkv), lambda b, i: (b, i, 0))],
        compiler_params=_params(("arbitrary", "arbitrary")),
        name="qkv_proj",
    )(y, norm_w.reshape(1, d), mod, mod, w_bf16, qn.reshape(1, dh), kn.reshape(1, dh), *tabs)


def _attn_kernel(*refs, dh, group, has_cache, scale):
    if has_cache:
        q_ref, k_ref, v_ref, kc_ref, vc_ref, o_ref, kall, vext = refs
        past = kc_ref.shape[0]
    else:
        q_ref, k_ref, v_ref, o_ref, kall, vext = refs
        past = 0

    @pl.when(pl.program_id(2) == 0)
    def _():
        if has_cache:
            kall[:past, :] = kc_ref[...].astype(BF16)
            vext[:past, :dh] = vc_ref[...].astype(BF16)
        kall[past:, :] = k_ref[...]
        vext[past:, :dh] = v_ref[...]
        vext[:, dh:] = jnp.ones((vext.shape[0], dh), BF16)

    c = scale * 1.4426950408889634
    tq = q_ref.shape[0]
    q = jnp.concatenate([q_ref[:, g * dh:(g + 1) * dh] for g in range(group)], axis=0)
    lk = kall.shape[0]
    ck = _pick(lk, (512, 256, 128))
    m = None
    for j in range(lk // ck):
        s = _dot_nt(q, kall[j * ck:(j + 1) * ck, :])
        mj = jnp.max(s, axis=-1, keepdims=True)
        vj = vext[j * ck:(j + 1) * ck, :]
        if m is None:
            m = mj
            o = _dot(jnp.exp2((s - m) * c).astype(BF16), vj)
        else:
            m_new = jnp.maximum(m, mj)
            o = o * jnp.exp2((m - m_new) * c) + _dot(jnp.exp2((s - m_new) * c).astype(BF16), vj)
            m = m_new
    for g in range(group):
        og = o[g * tq:(g + 1) * tq]
        o_ref[:, g * dh:(g + 1) * dh] = (og[:, :dh] / og[:, dh:dh + 1]).astype(BF16)


def _attention(qkv, nseq, seq0, ls, dh, cache_k=None, cache_v=None):
    group = ATT_HEADS // ATT_KV_HEADS
    has_cache = cache_k is not None
    tq = _pick(ls, (256, 128))
    kcol = ATT_HEADS
    vcol = ATT_HEADS + ATT_KV_HEADS
    in_specs = [
        pl.BlockSpec((None, tq, group * dh), lambda s, kh, i: (s + seq0, i, kh)),
        pl.BlockSpec((None, ls, dh), lambda s, kh, i: (s + seq0, 0, kcol + kh)),
        pl.BlockSpec((None, ls, dh), lambda s, kh, i: (s + seq0, 0, vcol + kh)),
    ]
    args = [qkv, qkv, qkv]
    past = 0
    if has_cache:
        past = cache_k.shape[1]
        in_specs += [pl.BlockSpec((None, past, dh), lambda s, kh, i: (s, 0, kh))] * 2
        args += [cache_k, cache_v]
    kern = functools.partial(_attn_kernel, dh=dh, group=group, has_cache=has_cache, scale=float(dh) ** -0.5)
    return pl.pallas_call(
        kern,
        out_shape=jax.ShapeDtypeStruct((nseq, ls, ATT_HEADS * dh), BF16),
        grid=(nseq, ATT_KV_HEADS, ls // tq),
        in_specs=in_specs,
        out_specs=pl.BlockSpec((None, tq, group * dh), lambda s, kh, i: (s, i, kh)),
        scratch_shapes=[pltpu.VMEM((past + ls, dh), BF16), pltpu.VMEM((past + ls, 2 * dh), BF16)],
        compiler_params=_params(("arbitrary", "arbitrary", "arbitrary")),
        name="attention_cache" if has_cache else "attention_self",
    )(*args)


def _pack_rows(h):
    half = h.shape[1] // 2
    u = lax.bitcast_convert_type(h.astype(BF16).astype(F32), U32)
    return (u[:, :half] >> 16) | (u[:, half:] & jnp.uint32(0xFFFF0000))


def _unpack_rows(p):
    lo = lax.bitcast_convert_type(p << 16, F32)
    hi = lax.bitcast_convert_type(p & jnp.uint32(0xFFFF0000), F32)
    return jnp.concatenate([lo, hi], axis=1).astype(BF16)


def _outproj_kernel(os_ref, op_ref, w_ref, y_ref, g1_ref, nw_ref, sh_ref, sc_ref, rwh_ref, rwl_ref,
                    yo_ref, h_ref, lg_ref, *, ns):
    is_prompt = pl.program_id(0) >= ns
    o = jnp.where(is_prompt, op_ref[...], os_ref[...])
    y = y_ref[...] + g1_ref[...] * _dot(o, w_ref[...])
    yo_ref[...] = y
    h = _norm_mod(y, nw_ref[...], sh_ref[...], sc_ref[...])
    h_ref[...] = _pack_rows(h)
    h_hi = h.astype(BF16)
    h_lo = (h - h_hi.astype(F32)).astype(BF16)
    rwh = rwh_ref[...]
    lg_ref[...] = _dot_nt(rwh, h_hi) + _dot_nt(rwl_ref[...], h_hi) + _dot_nt(rwh, h_lo)


def _outproj(o_s, o_p, w_bf16, y, mod, layer, norm_w, rw_hi, rw_lo):
    nb, l, d = y.shape
    ns = o_s.shape[0]
    kdim = o_s.shape[2]
    ne = rw_hi.shape[0]
    tm = _pick(l, (512, 256, 128))
    ni = l // tm
    dp = jax.eval_shape(_pack_rows, jax.ShapeDtypeStruct((8, d), F32)).shape[1]
    return pl.pallas_call(
        functools.partial(_outproj_kernel, ns=ns),
        out_shape=[jax.ShapeDtypeStruct((nb, l, d), F32), jax.ShapeDtypeStruct((nb, l, dp), U32),
                   jax.ShapeDtypeStruct((nb, ne, l), F32)],
        grid=(nb, ni),
        in_specs=[
            pl.BlockSpec((None, tm, kdim), lambda b, i: (jnp.minimum(b, ns - 1), jnp.where(b < ns, i, ni - 1), 0)),
            pl.BlockSpec((None, tm, kdim), lambda b, i: (0, jnp.where(b < ns, 0, i), 0)),
            pl.BlockSpec((kdim, d), lambda b, i: (0, 0), pipeline_mode=pl.Buffered(1)),
            pl.BlockSpec((None, tm, d), lambda b, i: (b, i, 0)),
            _mod_spec(d, layer, 2),
            pl.BlockSpec((1, d), lambda b, i: (0, 0)),
            _mod_spec(d, layer, 3),
            _mod_spec(d, layer, 4),
            pl.BlockSpec((ne, d), lambda b, i: (0, 0)),
            pl.BlockSpec((ne, d), lambda b, i: (0, 0)),
        ],
        out_specs=[pl.BlockSpec((None, tm, d), lambda b, i: (b, i, 0)),
                   pl.BlockSpec((None, tm, dp), lambda b, i: (b, i, 0)),
                   pl.BlockSpec((None, ne, tm), lambda b, i: (b, 0, i))],
        compiler_params=_params(("arbitrary", "arbitrary")),
        name="outproj_router",
    )(o_s, o_p, w_bf16, y, mod, norm_w.reshape(1, d), mod, mod, rw_hi, rw_lo)


def _router_kernel(lg_ref, b_ref, bkt_ref, rank_ref, wt_ref, cnt_ref, carry_ref):
    ng = N_GROUPS
    sc = jax.nn.sigmoid(lg_ref[...])
    sel = sc + b_ref[...]
    sj = [sel[j * ng:(j + 1) * ng] for j in range(EXPERTS_PER_GROUP)]
    pj = [sc[j * ng:(j + 1) * ng] for j in range(EXPERTS_PER_GROUP)]
    hi1, lo1 = jnp.maximum(sj[0], sj[1]), jnp.minimum(sj[0], sj[1])
    hi2, lo2 = jnp.maximum(sj[2], sj[3]), jnp.minimum(sj[2], sj[3])
    top1 = jnp.maximum(hi1, hi2)
    top2 = jnp.maximum(jnp.minimum(hi1, hi2), jnp.maximum(lo1, lo2))
    gs = top1 + top2
    gmax = jnp.max(gs, axis=0, keepdims=True)
    gi = lax.broadcasted_iota(I32, gs.shape, 0)
    best = jnp.min(jnp.where(gs == gmax, gi, ng), axis=0, keepdims=True)
    onehot = gi == best
    neg = -jnp.inf
    vj = [jnp.max(jnp.where(onehot, a, neg), axis=0, keepdims=True) for a in sj]
    wj = [jnp.max(jnp.where(onehot, a, neg), axis=0, keepdims=True) for a in pj]

    def argmax4(vals):
        m, i = vals[0], jnp.zeros(vals[0].shape, I32)
        for j in range(1, EXPERTS_PER_GROUP):
            gt = vals[j] > m
            m = jnp.where(gt, vals[j], m)
            i = jnp.where(gt, j, i)
        return i

    i1 = argmax4(vj)
    i2 = argmax4([jnp.where(i1 == j, neg, vj[j]) for j in range(EXPERTS_PER_GROUP)])

    def take(vals, i):
        out = vals[0]
        for j in range(1, EXPERTS_PER_GROUP):
            out = jnp.where(i == j, vals[j], out)
        return out

    w1, w2 = take(wj, i1), take(wj, i2)
    tot = w1 + w2
    w1, w2 = w1 / tot, w2 / tot

    lo, hi = jnp.minimum(i1, i2), jnp.maximum(i1, i2)
    pair = jnp.where(lo == 0, hi - 1, jnp.where(lo == 1, jnp.where(hi == 3, 3, 4), 5))
    slot_a = jnp.where(pair < 3, 0, jnp.where(pair < 5, 1, 3))
    bucket = best * N_PAIRS + pair
    a_first = i1 == slot_a
    w_a = jnp.where(a_first, w1, w2)
    w_b = jnp.where(a_first, w2, w1)

    tl = bucket.shape[1]
    nbk = carry_ref.shape[0]

    @pl.when(jnp.logical_and(pl.program_id(0) == 0, pl.program_id(1) == 0))
    def _():
        carry_ref[...] = jnp.zeros_like(carry_ref)

    onehot_b = lax.broadcasted_iota(I32, (nbk, tl), 0) == bucket
    upper = (lax.broadcasted_iota(I32, (tl, tl), 0) <= lax.broadcasted_iota(I32, (tl, tl), 1))
    prefix = _dot(onehot_b.astype(BF16), upper.astype(BF16))
    carry = carry_ref[...]
    rank = jnp.sum(jnp.where(onehot_b, prefix - 1.0 + carry, 0.0), axis=0, keepdims=True)
    carry = carry + prefix[:, tl - 1:tl]
    carry_ref[...] = carry
    bkt_ref[...] = bucket
    rank_ref[...] = rank.astype(I32)
    cnt_ref[...] = jnp.broadcast_to(carry, cnt_ref.shape)
    rows = lax.broadcasted_iota(I32, (wt_ref.shape[1], tl), 0)
    wt_ref[...] = jnp.where(rows == 0, w_a, jnp.where(rows == 1, w_b, 0.0)).T


def _router(logits, bias_perm):
    nb, ne, l = logits.shape
    tl = _pick(l, (512, 256, 128))
    nbk = N_GROUPS * N_PAIRS
    return pl.pallas_call(
        _router_kernel,
        out_shape=[jax.ShapeDtypeStruct((nb, 1, l), I32), jax.ShapeDtypeStruct((nb, 1, l), I32),
                   jax.ShapeDtypeStruct((nb, l, 128), F32), jax.ShapeDtypeStruct((nbk, 128), F32)],
        grid=(nb, l // tl),
        in_specs=[pl.BlockSpec((None, ne, tl), lambda b, i: (b, 0, i)),
                  pl.BlockSpec((ne, 1), lambda b, i: (0, 0))],
        out_specs=[pl.BlockSpec((None, 1, tl), lambda b, i: (b, 0, i)),
                   pl.BlockSpec((None, 1, tl), lambda b, i: (b, 0, i)),
                   pl.BlockSpec((None, tl, 128), lambda b, i: (b, i, 0)),
                   pl.BlockSpec((nbk, 128), lambda b, i: (0, 0))],
        scratch_shapes=[pltpu.VMEM((nbk, 1), F32)],
        compiler_params=_params(("arbitrary", "arbitrary")),
        name="router_rank",
    )(logits, bias_perm.reshape(ne, 1))


def _moe_kernel(ea_ref, eb_ref, nu_ref, x_ref, wga_ref, wua_ref, wda_ref, wgb_ref, wub_ref, wdb_ref, o_ref):
    del ea_ref, eb_ref
    i = pl.program_id(0)
    d = o_ref.shape[1] // 2

    @pl.when(i < nu_ref[0])
    def _():
        x = _unpack_rows(x_ref[...])

        def expert(wg, wu, wd):
            hid = jax.nn.silu(_dot(x, wg[...])) * _dot(x, wu[...])
            return _dot(hid.astype(BF16), wd[...])

        o_ref[:, :d] = expert(wga_ref, wua_ref, wda_ref)
        o_ref[:, d:] = expert(wgb_ref, wub_ref, wdb_ref)

    @pl.when(i >= nu_ref[0])
    def _():
        o_ref[...] = jnp.zeros_like(o_ref)


def _moe_experts(xs, blk_a, blk_b, n_used, wg, wu, wd, bm):
    p, dp = xs.shape
    _, d, de = wg.shape
    nblk = p // bm
    w_in = lambda sel: pl.BlockSpec((None, d, de), lambda i, ea, eb, nu: ((ea, eb)[sel][i], 0, 0))
    w_out = lambda sel: pl.BlockSpec((None, de, d), lambda i, ea, eb, nu: ((ea, eb)[sel][i], 0, 0))
    grid_spec = pltpu.PrefetchScalarGridSpec(
        num_scalar_prefetch=3,
        grid=(nblk,),
        in_specs=[pl.BlockSpec((bm, dp), lambda i, ea, eb, nu: (i, 0)),
                  w_in(0), w_in(0), w_out(0), w_in(1), w_in(1), w_out(1)],
        out_specs=pl.BlockSpec((bm, 2 * d), lambda i, ea, eb, nu: (i, 0)),
    )
    return pl.pallas_call(
        _moe_kernel,
        out_shape=jax.ShapeDtypeStruct((p, 2 * d), F32),
        grid_spec=grid_spec,
        compiler_params=_params(("arbitrary",)),
        name="moe_experts",
    )(blk_a, blk_b, n_used, xs, wg, wu, wd, wg, wu, wd)


def _row_plan(bucket, rank, counts, bm):
    t = bucket.size
    nbk = N_GROUPS * N_PAIRS
    cnt = counts[:, 0].astype(I32)
    padded = (cnt + bm - 1) // bm * bm
    pends = jnp.cumsum(padded)
    pstarts = pends - padded
    onehot = bucket.reshape(t, 1) == jnp.arange(nbk, dtype=I32)[None, :]
    dest = jnp.sum(jnp.where(onehot, pstarts[None, :], 0), axis=1) + rank.reshape(t)
    p = (t + bm - 1) // bm * bm + nbk * bm
    nblk = p // bm
    blk_bucket = jnp.clip(jnp.searchsorted(pends, jnp.arange(nblk, dtype=I32) * bm, side='right'), 0, nbk - 1)
    grp, pair = blk_bucket // N_PAIRS, blk_bucket % N_PAIRS
    blk_a = grp * EXPERTS_PER_GROUP + jnp.asarray(_PAIR_A, I32)[pair]
    blk_b = grp * EXPERTS_PER_GROUP + jnp.asarray(_PAIR_B, I32)[pair]
    n_used = (pends[-1:] // bm).astype(I32)
    return dest.astype(I32), blk_a.astype(I32), blk_b.astype(I32), n_used, p


def _scatter_kernel(dest_ref, src_ref, init_hbm, out_hbm, sem, *, rows):
    del init_hbm
    base = pl.program_id(0) * rows

    def issue(r, carry):
        pltpu.make_async_copy(src_ref.at[r], out_hbm.at[dest_ref[base + r]], sem).start()
        return carry

    lax.fori_loop(0, rows, issue, 0, unroll=8)
    pltpu.make_async_copy(src_ref, out_hbm.at[pl.ds(0, rows)], sem).wait()


def _scatter_rows(h_rows, dest, p):
    t, dp = h_rows.shape
    rows = _pick(t, (2048, 1024, 512))
    grid_spec = pltpu.PrefetchScalarGridSpec(
        num_scalar_prefetch=1,
        grid=(t // rows,),
        in_specs=[pl.BlockSpec((rows, dp), lambda i, dst: (i, 0)), pl.BlockSpec(memory_space=pl.ANY)],
        out_specs=pl.BlockSpec(memory_space=pl.ANY),
        scratch_shapes=[pltpu.SemaphoreType.DMA],
    )
    return pl.pallas_call(
        functools.partial(_scatter_kernel, rows=rows),
        out_shape=jax.ShapeDtypeStruct((p, dp), h_rows.dtype),
        grid_spec=grid_spec,
        input_output_aliases={2: 0},
        compiler_params=_params(("arbitrary",)),
        name="moe_scatter_rows",
    )(dest, h_rows, jnp.zeros((p, dp), h_rows.dtype))


def _combine_kernel(dest_ref, y_ref, g_ref, wt_ref, m_hbm, o_ref, mbuf, sem, *, tm, nsteps):
    step = pl.program_id(0) * pl.num_programs(1) + pl.program_id(1)
    slot = step % 2
    d = y_ref.shape[1]

    def gather(s, sl):
        def issue(r, carry):
            pltpu.make_async_copy(m_hbm.at[dest_ref[s * tm + r]], mbuf.at[sl, r], sem.at[sl]).start()
            return carry
        lax.fori_loop(0, tm, issue, 0, unroll=8)

    @pl.when(step == 0)
    def _():
        gather(0, 0)

    @pl.when(step + 1 < nsteps)
    def _():
        gather(step + 1, 1 - slot)

    pltpu.make_async_copy(m_hbm.at[pl.ds(0, tm)], mbuf.at[slot], sem.at[slot]).wait()
    m = mbuf[slot]
    w = wt_ref[...]
    o_ref[...] = y_ref[...] + g_ref[...] * (w[:, 0:1] * m[:, :d] + w[:, 1:2] * m[:, d:])


def _combine(y, mod, layer, moe_rows, dest, wt):
    nb, l, d = y.shape
    tm = _pick(l, (512, 256, 128))
    ni = l // tm
    grid_spec = pltpu.PrefetchScalarGridSpec(
        num_scalar_prefetch=1,
        grid=(nb, ni),
        in_specs=[
            pl.BlockSpec((None, tm, d), lambda b, i, dst: (b, i, 0)),
            pl.BlockSpec((None, None, None, 1, d), lambda b, i, dst: (layer, 5, b, 0, 0)),
            pl.BlockSpec((None, tm, 128), lambda b, i, dst: (b, i, 0)),
            pl.BlockSpec(memory_space=pl.ANY),
        ],
        out_specs=pl.BlockSpec((None, tm, d), lambda b, i, dst: (b, i, 0)),
        scratch_shapes=[pltpu.VMEM((2, tm, 2 * d), F32), pltpu.SemaphoreType.DMA((2,))],
    )
    return pl.pallas_call(
        functools.partial(_combine_kernel, tm=tm, nsteps=nb * ni),
        out_shape=jax.ShapeDtypeStruct((nb, l, d), F32),
        grid_spec=grid_spec,
        compiler_params=_params(("arbitrary", "arbitrary")),
        name="moe_combine",
    )(dest, y, mod, wt, moe_rows)


def kernel(x_prompt, x_sample, c, state_ret_fwd, state_ret_bwd, cache_attn_k, cache_attn_v, c_ctx,
           ada_w, ada_b, norm1_w, norm2_w, ret_w_in, ret_w_out, ret_gn_w, ret_decay_fwd, ret_decay_bwd,
           attn_w_qkv, attn_w_o, attn_q_norm, attn_k_norm, router_w, router_b,
           moe_w_gate, moe_w_up, moe_w_down):
    n_prompt, seq, d = x_prompt.shape
    ns, l, _ = x_sample.shape
    assert n_prompt * seq == l, "prompt tokens must fill exactly one sample-length row"
    depth = ada_w.shape[0]
    nb = ns + 1
    dk = d // RET_HEADS
    dv = 2 * dk
    dh = d // ATT_HEADS
    bm = MOE_ROW_BLOCK

    y = jnp.concatenate([x_sample, x_prompt.reshape(1, l, d)], axis=0)
    nbp = (nb + 7) // 8 * 8
    cond = jnp.zeros((nbp, d), F32).at[:ns].set(c).at[ns].set(c_ctx)
    mod = _ada_params(cond, ada_w, ada_b)

    perm = (jnp.arange(N_EXPERTS) % N_GROUPS) * EXPERTS_PER_GROUP + jnp.arange(N_EXPERTS) // N_GROUPS
    rw_t = router_w.T[perm]
    rw_hi = rw_t.astype(BF16)
    rw_lo = (rw_t - rw_hi.astype(F32)).astype(BF16)
    rb_perm = router_b[perm]

    ret_f, ret_b, k_new, v_new = [], [], [], []
    for i in range(depth):
        if i % 2 == 0:
            r = i // 2
            qkvg = _ret_inproj(y, mod, i, norm1_w[i], ret_w_in[r].astype(BF16), ret_gn_w[r], dk)
            o_s = _retention(qkvg, ns, 0, l, dk, ret_decay_fwd[r], ret_decay_bwd[r],
                             s0f=state_ret_fwd[:, r], s0b=state_ret_bwd[:, r])
            qkvg_p = qkvg.reshape(nb * n_prompt, seq, qkvg.shape[2])
            o_p, s_f, s_b = _retention(qkvg_p, n_prompt, ns * n_prompt, seq, dk,
                                       ret_decay_fwd[r], ret_decay_bwd[r], emit_final=True)
            o_s = o_s[0] if isinstance(o_s, (list, tuple)) else o_s
            ret_f.append(s_f)
            ret_b.append(s_b)
            w_o = ret_w_out[r].astype(BF16)
        else:
            a = i // 2
            qkv, kvf = _qkv_proj(y, ns, mod, i, norm1_w[i], attn_w_qkv[a].astype(BF16),
                                 attn_q_norm[a], attn_k_norm[a], dh)
            past = cache_attn_k.shape[2]
            o_s = _attention(qkv, ns, 0, l, dh,
                             cache_k=cache_attn_k[:, a].reshape(ns, past, ATT_KV_HEADS * dh),
                             cache_v=cache_attn_v[:, a].reshape(ns, past, ATT_KV_HEADS * dh))
            o_p = _attention(qkv.reshape(nb * n_prompt, seq, qkv.shape[2]), n_prompt, ns * n_prompt, seq, dh)
            nk = ATT_KV_HEADS * dh
            k_new.append(kvf[ns, :, :nk].reshape(n_prompt, seq, ATT_KV_HEADS, dh))
            v_new.append(kvf[ns, :, nk:].reshape(n_prompt, seq, ATT_KV_HEADS, dh))
            w_o = attn_w_o[a].astype(BF16)
        o_p = o_p.reshape(1, l, o_p.shape[-1])
        y, h2, logits = _outproj(o_s, o_p, w_o, y, mod, i, norm2_w[i], rw_hi, rw_lo)
        bucket, rank, wt, counts = _router(logits, rb_perm)
        dest, blk_a, blk_b, n_used, p = _row_plan(bucket, rank, counts, bm)
        xs = _scatter_rows(h2.reshape(nb * l, h2.shape[2]), dest, p)
        moe_rows = _moe_experts(xs, blk_a, blk_b, n_used, moe_w_gate[i].astype(BF16),
                                moe_w_up[i].astype(BF16), moe_w_down[i].astype(BF16), bm)
        y = _combine(y, mod, i, moe_rows, dest, wt)

    y_p = y[ns].reshape(n_prompt, seq, d)
    y_s = y[:ns]
    return (y_p, y_s, jnp.stack(ret_f, axis=1), jnp.stack(ret_b, axis=1),
            jnp.stack(k_new, axis=1), jnp.stack(v_new, axis=1))
```

```python
import functools

import jax
import jax.numpy as jnp
from jax import lax
from jax.experimental import pallas as pl
from jax.experimental.pallas import tpu as pltpu

F32 = jnp.float32
BF16 = jnp.bfloat16
I32 = jnp.int32
U32 = jnp.uint32

EPS = 1e-6
GRID_W = 64
RET_HEADS = 4
ATT_HEADS = 8
ATT_KV_HEADS = 2
ROPE_THETA = 10000.0
N_EXPERTS = 32
N_GROUPS = 8
EXPERTS_PER_GROUP = N_EXPERTS // N_GROUPS
TOP_K = 2
_PAIR_A = (0, 0, 0, 1, 1, 3)
_PAIR_B = (1, 2, 3, 3, 2, 2)
N_PAIRS = len(_PAIR_A)
MOE_ROW_BLOCK = 256

V7X_VMEM_BYTES = 64 * 1024 * 1024
VMEM_LIMIT_BYTES = 56 * 1024 * 1024


def _params(sem):
    return pltpu.CompilerParams(dimension_semantics=sem, vmem_limit_bytes=VMEM_LIMIT_BYTES)


def _dot(a, b):
    return jnp.dot(a, b, preferred_element_type=F32)


def _dot_nt(a, b):
    return lax.dot_general(a, b, (((1,), (1,)), ((), ())), preferred_element_type=F32)


def _dot_tn(a, b):
    return lax.dot_general(a, b, (((0,), (0,)), ((), ())), preferred_element_type=F32)


def _pick(n, prefs):
    for p in prefs:
        if n % p == 0:
            return p
    return n


def _ada_kernel(c_ref, w_ref, b_ref, o_ref):
    c = c_ref[...]
    s = jax.nn.silu(c).astype(BF16)
    o_ref[...] = _dot(s, w_ref[...].astype(BF16)) + b_ref[...]


def _ada_params(cond, ada_w, ada_b):
    depth, d, n6 = ada_w.shape
    nbp = cond.shape[0]
    tn = _pick(n6, (1536, 1024, 512, 256, 128))
    out = pl.pallas_call(
        _ada_kernel,
        out_shape=jax.ShapeDtypeStruct((depth, nbp, n6), F32),
        grid=(depth, n6 // tn),
        in_specs=[
            pl.BlockSpec((nbp, d), lambda l, j: (0, 0)),
            pl.BlockSpec((None, d, tn), lambda l, j: (l, 0, j)),
            pl.BlockSpec((None, 1, tn), lambda l, j: (l, 0, j)),
        ],
        out_specs=pl.BlockSpec((None, nbp, tn), lambda l, j: (l, 0, j)),
        compiler_params=_params(("arbitrary", "arbitrary")),
        name="ada_params",
    )(cond, ada_w, ada_b.reshape(depth, 1, n6))
    return out.reshape(depth, nbp, 6, 1, d).transpose(0, 2, 1, 3, 4)


def _mod_spec(d, layer, j):
    return pl.BlockSpec((None, None, None, 1, d), lambda b, i: (layer, j, b, 0, 0))


def _pair_specs(ns, ni, tm, w):
    return [pl.BlockSpec((None, tm, w), lambda b, i, *_: (jnp.minimum(b, ns - 1), jnp.where(b < ns, i, ni - 1), 0)),
            pl.BlockSpec((None, tm, w), lambda b, i, *_: (0, jnp.where(b < ns, 0, i), 0))]


def _pair_load(s_ref, p_ref, ns):
    return jnp.where(pl.program_id(0) >= ns, p_ref[...], s_ref[...])


def _pair_store(s_ref, p_ref, ns, val):
    @pl.when(pl.program_id(0) < ns)
    def _():
        s_ref[...] = val

    @pl.when(pl.program_id(0) >= ns)
    def _():
        p_ref[...] = val


def _norm_mod(x, nw, sh, sc):
    ms = jnp.mean(x * x, axis=-1, keepdims=True)
    h = x * lax.rsqrt(ms + EPS) * nw
    return h * (1.0 + sc) + sh


def _ret_inproj_kernel(xs_ref, xp_ref, nw_ref, sh_ref, sc_ref, w_ref, gn_ref, o_ref, *, ns, hq, hv, tn, k_scale):
    x = _pair_load(xs_ref, xp_ref, ns)
    hb = _norm_mod(x, nw_ref[...], sh_ref[...], sc_ref[...]).astype(BF16)
    n = w_ref.shape[1]
    for j in range(n // tn):
        lo = j * tn
        acc = _dot(hb, w_ref[:, lo:lo + tn])
        if hq <= lo < 2 * hq:
            acc = acc * k_scale
        elif lo >= 2 * hq + hv:
            g0 = (lo - 2 * hq - hv) % hv
            acc = jax.nn.silu(acc) * gn_ref[:, g0:g0 + tn]
        o_ref[:, lo:lo + tn] = acc.astype(BF16)


def _ret_inproj(y, mod, layer, norm_w, w_in_bf16, gn_w, dk):
    y_s, y_p = y
    ns, l, d = y_s.shape
    nb = ns + 1
    n = w_in_bf16.shape[1]
    hq = RET_HEADS * dk
    hv = 2 * hq
    tm = _pick(l, (512, 256, 128))
    tn = _pick(hq, (1024, 512, 256, 128))
    kern = functools.partial(_ret_inproj_kernel, ns=ns, hq=hq, hv=hv, tn=tn, k_scale=float(dk) ** -0.5)
    return pl.pallas_call(
        kern,
        out_shape=jax.ShapeDtypeStruct((nb, l, n), BF16),
        grid=(nb, l // tm),
        in_specs=_pair_specs(ns, l // tm, tm, d) + [
            pl.BlockSpec((1, d), lambda b, i: (0, 0)),
            _mod_spec(d, layer, 0),
            _mod_spec(d, layer, 1),
            pl.BlockSpec((d, n), lambda b, i: (0, 0), pipeline_mode=pl.Buffered(1)),
            pl.BlockSpec((1, hv), lambda b, i: (0, 0)),
        ],
        out_specs=pl.BlockSpec((None, tm, n), lambda b, i: (b, i, 0)),
        compiler_params=_params(("arbitrary", "arbitrary")),
        name="ret_inproj",
    )(y_s, y_p, norm_w.reshape(1, d), mod, mod, w_in_bf16, gn_w.reshape(1, hv))


def _log_sigmoid(x):
    return jnp.minimum(x, 0.0) - jnp.log1p(jnp.exp(-jnp.abs(x)))


def _ret_kernel(*refs, ls, c, has_init, emit_final):
    it = iter(refs)
    q_ref, k_ref, v_ref, gf_ref, gb_ref, df_ref, db_ref = (next(it) for _ in range(7))
    if has_init:
        s0f_ref, s0b_ref = next(it), next(it)
    y_ref = next(it)
    if emit_final:
        sf_out, sb_out = next(it), next(it)
    s_scr, sb_scr = next(it), next(it)

    dk = q_ref.shape[1]
    nchunks = ls // c
    lg_f = _log_sigmoid(df_ref[...])[:, 0:1]
    lg_b = _log_sigmoid(db_ref[...])[:, 0:1]

    ri = lax.broadcasted_iota(I32, (c, c), 0)
    ci = lax.broadcasted_iota(I32, (c, c), 1)
    dif = (ri - ci).astype(F32)
    decay_f = jnp.where(dif >= 0, jnp.exp(jnp.maximum(dif, 0.0) * lg_f), 0.0)
    decay_b = jnp.where(dif <= 0, jnp.exp(jnp.maximum(-dif, 0.0) * lg_b), 0.0)
    pos = lax.broadcasted_iota(I32, (c, dk), 0).astype(F32)
    qdec_f = jnp.exp((pos + 1.0) * lg_f)
    kdec_f = jnp.exp((c - 1.0 - pos) * lg_f)
    qdec_b = jnp.exp((c - pos) * lg_b)
    kdec_b = jnp.exp(pos * lg_b)
    cdec_f = jnp.exp(c * lg_f)
    cdec_b = jnp.exp(c * lg_b)

    if has_init:
        s_scr[...] = s0b_ref[...]
    else:
        s_scr[...] = jnp.zeros_like(s_scr)

    def bwd_body(t, carry):
        n = nchunks - 1 - t
        r0 = pl.multiple_of(n * c, c)
        s = s_scr[...]
        sb_scr[n] = s.astype(BF16)
        kc = k_ref[pl.ds(r0, c), :].astype(F32)
        vc = v_ref[pl.ds(r0, c), :]
        s_scr[...] = s * cdec_b + _dot_tn((kc * kdec_b).astype(BF16), vc)
        return carry

    lax.fori_loop(0, nchunks, bwd_body, 0)
    if emit_final:
        sb_out[...] = s_scr[...]

    if has_init:
        s_scr[...] = s0f_ref[...]
    else:
        s_scr[...] = jnp.zeros_like(s_scr)

    def head_norm(o):
        mu = jnp.mean(o, axis=-1, keepdims=True)
        dlt = o - mu
        var = jnp.mean(dlt * dlt, axis=-1, keepdims=True)
        return dlt * lax.rsqrt(var + EPS)

    def fwd_body(n, carry):
        r0 = pl.multiple_of(n * c, c)
        rows = pl.ds(r0, c)
        qc = q_ref[rows, :]
        kc = k_ref[rows, :]
        vc = v_ref[rows, :]
        qf = qc.astype(F32)
        sc = _dot_nt(qc, kc)
        s = s_scr[...]
        o_f = _dot((sc * decay_f).astype(BF16), vc) + _dot((qf * qdec_f).astype(BF16), s.astype(BF16))
        o_b = _dot((sc * decay_b).astype(BF16), vc) + _dot((qf * qdec_b).astype(BF16), sb_scr[n])
        s_scr[...] = s * cdec_f + _dot_tn((kc.astype(F32) * kdec_f).astype(BF16), vc)
        y = head_norm(o_f) * gf_ref[rows, :].astype(F32) + head_norm(o_b) * gb_ref[rows, :].astype(F32)
        y_ref[rows, :] = y.astype(BF16)
        return carry

    lax.fori_loop(0, nchunks, fwd_body, 0, unroll=2 if nchunks % 2 == 0 else 1)
    if emit_final:
        sf_out[...] = s_scr[...]


def _retention(qkvg, nseq, seq0, ls, dk, decay_f, decay_b, s0f=None, s0b=None, emit_final=False):
    h = RET_HEADS
    dv = 2 * dk
    c = _pick(ls, (256, 128))
    has_init = s0f is not None
    kq, kk, kv, kgf, kgb = 0, h, h, 2 * h, 3 * h
    in_specs = [
        pl.BlockSpec((None, ls, dk), lambda s, hh: (s + seq0, 0, kq + hh)),
        pl.BlockSpec((None, ls, dk), lambda s, hh: (s + seq0, 0, kk + hh)),
        pl.BlockSpec((None, ls, dv), lambda s, hh: (s + seq0, 0, kv + hh)),
        pl.BlockSpec((None, ls, dv), lambda s, hh: (s + seq0, 0, kgf + hh)),
        pl.BlockSpec((None, ls, dv), lambda s, hh: (s + seq0, 0, kgb + hh)),
        pl.BlockSpec((None, 1, 128), lambda s, hh: (hh, 0, 0)),
        pl.BlockSpec((None, 1, 128), lambda s, hh: (hh, 0, 0)),
    ]
    args = [qkvg, qkvg, qkvg, qkvg, qkvg,
            jnp.broadcast_to(decay_f.reshape(h, 1, 1), (h, 1, 128)),
            jnp.broadcast_to(decay_b.reshape(h, 1, 1), (h, 1, 128))]
    if has_init:
        in_specs += [pl.BlockSpec((None, None, dk, dv), lambda s, hh: (s, hh, 0, 0))] * 2
        args += [s0f, s0b]
    out_shape = [jax.ShapeDtypeStruct((nseq, ls, h * dv), BF16)]
    out_specs = [pl.BlockSpec((None, ls, dv), lambda s, hh: (s, 0, hh))]
    if emit_final:
        out_shape += [jax.ShapeDtypeStruct((nseq, h, dk, dv), F32)] * 2
        out_specs += [pl.BlockSpec((None, None, dk, dv), lambda s, hh: (s, hh, 0, 0))] * 2
    kern = functools.partial(_ret_kernel, ls=ls, c=c, has_init=has_init, emit_final=emit_final)
    return pl.pallas_call(
        kern,
        out_shape=out_shape,
        grid=(nseq, h),
        in_specs=in_specs,
        out_specs=out_specs,
        scratch_shapes=[pltpu.VMEM((dk, dv), F32), pltpu.VMEM((ls // c, dk, dv), BF16)],
        compiler_params=_params(("arbitrary", "arbitrary")),
        name="retention_init" if has_init else "retention_zero",
    )(*args)


def _qkv_kernel(xs_ref, xp_ref, nw_ref, sh_ref, sc_ref, w_ref, qn_ref, kn_ref, cos_ref, sa_ref, sb_ref,
                o_ref, kp_ref, vp_ref, *, ns, dh):
    x = _pair_load(xs_ref, xp_ref, ns)
    hb = _norm_mod(x, nw_ref[...], sh_ref[...], sc_ref[...]).astype(BF16)
    nq = ATT_HEADS * dh
    nk = ATT_KV_HEADS * dh
    cos, sa, sb = cos_ref[...], sa_ref[...], sb_ref[...]
    is_prompt = pl.program_id(0) >= ns

    def norm_rope(a, w):
        a = a * lax.rsqrt(jnp.mean(a * a, axis=-1, keepdims=True) + EPS) * w
        roped = a * cos + pltpu.roll(a, dh - dh // 4, 1) * sa + pltpu.roll(a, dh // 4, 1) * sb
        return a, roped

    q = _dot(hb, w_ref[:, :nq])
    for hh in range(ATT_HEADS):
        _, r = norm_rope(q[:, hh * dh:(hh + 1) * dh], qn_ref[...])
        o_ref[:, hh * dh:(hh + 1) * dh] = r.astype(BF16)
    kvv = _dot(hb, w_ref[:, nq:])
    k_norm = []
    for hh in range(ATT_KV_HEADS):
        a, r = norm_rope(kvv[:, hh * dh:(hh + 1) * dh], kn_ref[...])
        o_ref[:, nq + hh * dh:nq + (hh + 1) * dh] = r.astype(BF16)
        k_norm.append(a)
    o_ref[:, nq + nk:] = kvv[:, nk:].astype(BF16)

    @pl.when(is_prompt)
    def _():
        for hh in range(ATT_KV_HEADS):
            kp_ref[:, hh * dh:(hh + 1) * dh] = k_norm[hh]
        vp_ref[...] = kvv[:, nk:]


def _rope_tables(l, dh, rotate):
    axis_dim = dh // 2
    half = axis_dim // 2
    t = jnp.arange(l)
    row = (t // GRID_W).astype(F32)
    col = (t % GRID_W).astype(F32)
    inv = ROPE_THETA ** (-jnp.arange(0, axis_dim, 2, dtype=F32) / axis_dim)
    d = jnp.arange(dh)
    pos = jnp.where((d // axis_dim)[None, :] == 0, row[:, None], col[:, None])
    ang = pos * inv[d % half][None, :]
    first = ((d % axis_dim) < half)[None, :]
    cos, sin = jnp.cos(ang), jnp.sin(ang)
    if not rotate:
        cos, sin = jnp.ones_like(cos), jnp.zeros_like(sin)
    return cos, jnp.where(first, -sin, 0.0), jnp.where(first, 0.0, sin)


def _qkv_proj(y, mod, layer, norm_w, w_bf16, qn, kn, dh):
    y_s, y_p = y
    ns, l, d = y_s.shape
    nb = ns + 1
    n = w_bf16.shape[1]
    nk = ATT_KV_HEADS * dh
    tm = _pick(l, (512, 256, 128))
    tabs = [jnp.stack([a, b]) for a, b in zip(_rope_tables(l, dh, True), _rope_tables(l, dh, False))]
    tab_spec = pl.BlockSpec((None, tm, dh), lambda b, i: (jnp.where(b < ns, 0, 1), i, 0))
    prompt_spec = pl.BlockSpec((tm, nk), lambda b, i: (jnp.where(b < ns, 0, i), 0))
    return pl.pallas_call(
        functools.partial(_qkv_kernel, ns=ns, dh=dh),
        out_shape=[jax.ShapeDtypeStruct((nb, l, n), BF16), jax.ShapeDtypeStruct((l, nk), F32),
                   jax.ShapeDtypeStruct((l, nk), F32)],
        grid=(nb, l // tm),
        in_specs=_pair_specs(ns, l // tm, tm, d) + [
            pl.BlockSpec((1, d), lambda b, i: (0, 0)),
            _mod_spec(d, layer, 0),
            _mod_spec(d, layer, 1),
            pl.BlockSpec((d, n), lambda b, i: (0, 0), pipeline_mode=pl.Buffered(1)),
            pl.BlockSpec((1, dh), lambda b, i: (0, 0)),
            pl.BlockSpec((1, dh), lambda b, i: (0, 0)),
            tab_spec, tab_spec, tab_spec,
        ],
        out_specs=[pl.BlockSpec((None, tm, n), lambda b, i: (b, i, 0)), prompt_spec, prompt_spec],
        compiler_params=_params(("arbitrary", "arbitrary")),
        name="qkv_proj",
    )(y_s, y_p, norm_w.reshape(1, d), mod, mod, w_bf16, qn.reshape(1, dh), kn.reshape(1, dh), *tabs)


def _attn_kernel(*refs, dh, group, has_cache, scale):
    if has_cache:
        q_ref, k_ref, v_ref, kc_ref, vc_ref, o_ref, kall, vext = refs
        past = kc_ref.shape[0]
    else:
        q_ref, k_ref, v_ref, o_ref, kall, vext = refs
        past = 0

    @pl.when(pl.program_id(2) == 0)
    def _():
        if has_cache:
            kall[:past, :] = kc_ref[...].astype(BF16)
            vext[:past, :dh] = vc_ref[...].astype(BF16)
        kall[past:, :] = k_ref[...]
        vext[past:, :dh] = v_ref[...]
        vext[:, dh:] = jnp.ones((vext.shape[0], dh), BF16)

    c = scale * 1.4426950408889634
    tq = q_ref.shape[0]
    q = jnp.concatenate([q_ref[:, g * dh:(g + 1) * dh] for g in range(group)], axis=0)
    lk = kall.shape[0]
    ck = _pick(lk, (512, 256, 128))
    m = None
    for j in range(lk // ck):
        s = _dot_nt(q, kall[j * ck:(j + 1) * ck, :])
        mj = jnp.max(s, axis=-1, keepdims=True)
        vj = vext[j * ck:(j + 1) * ck, :]
        if m is None:
            m = mj
            o = _dot(jnp.exp2((s - m) * c).astype(BF16), vj)
        else:
            m_new = jnp.maximum(m, mj)
            o = o * jnp.exp2((m - m_new) * c) + _dot(jnp.exp2((s - m_new) * c).astype(BF16), vj)
            m = m_new
    for g in range(group):
        og = o[g * tq:(g + 1) * tq]
        o_ref[:, g * dh:(g + 1) * dh] = (og[:, :dh] / og[:, dh:dh + 1]).astype(BF16)


def _attention(qkv, nseq, seq0, ls, dh, cache_k=None, cache_v=None):
    group = ATT_HEADS // ATT_KV_HEADS
    has_cache = cache_k is not None
    tq = _pick(ls, (256, 128))
    kcol = ATT_HEADS
    vcol = ATT_HEADS + ATT_KV_HEADS
    in_specs = [
        pl.BlockSpec((None, tq, group * dh), lambda s, kh, i: (s + seq0, i, kh)),
        pl.BlockSpec((None, ls, dh), lambda s, kh, i: (s + seq0, 0, kcol + kh)),
        pl.BlockSpec((None, ls, dh), lambda s, kh, i: (s + seq0, 0, vcol + kh)),
    ]
    args = [qkv, qkv, qkv]
    past = 0
    if has_cache:
        past = cache_k.shape[1]
        in_specs += [pl.BlockSpec((None, past, dh), lambda s, kh, i: (s, 0, kh))] * 2
        args += [cache_k, cache_v]
    kern = functools.partial(_attn_kernel, dh=dh, group=group, has_cache=has_cache, scale=float(dh) ** -0.5)
    return pl.pallas_call(
        kern,
        out_shape=jax.ShapeDtypeStruct((nseq, ls, ATT_HEADS * dh), BF16),
        grid=(nseq, ATT_KV_HEADS, ls // tq),
        in_specs=in_specs,
        out_specs=pl.BlockSpec((None, tq, group * dh), lambda s, kh, i: (s, i, kh)),
        scratch_shapes=[pltpu.VMEM((past + ls, dh), BF16), pltpu.VMEM((past + ls, 2 * dh), BF16)],
        compiler_params=_params(("arbitrary", "arbitrary", "arbitrary")),
        name="attention_cache" if has_cache else "attention_self",
    )(*args)


def _pack_rows(h):
    half = h.shape[1] // 2
    u = lax.bitcast_convert_type(h.astype(BF16).astype(F32), U32)
    return (u[:, :half] >> 16) | (u[:, half:] & jnp.uint32(0xFFFF0000))


def _unpack_rows(p):
    lo = lax.bitcast_convert_type(p << 16, F32)
    hi = lax.bitcast_convert_type(p & jnp.uint32(0xFFFF0000), F32)
    return jnp.concatenate([lo, hi], axis=1).astype(BF16)


def _outproj_kernel(os_ref, op_ref, w_ref, ys_ref, yp_ref, g1_ref, nw_ref, sh_ref, sc_ref, rwh_ref, rwl_ref,
                    yos_ref, yop_ref, h_ref, lg_ref, *, ns):
    o = _pair_load(os_ref, op_ref, ns)
    y = _pair_load(ys_ref, yp_ref, ns) + g1_ref[...] * _dot(o, w_ref[...])
    _pair_store(yos_ref, yop_ref, ns, y)
    h = _norm_mod(y, nw_ref[...], sh_ref[...], sc_ref[...])
    h_ref[...] = _pack_rows(h)
    h_hi = h.astype(BF16)
    h_lo = (h - h_hi.astype(F32)).astype(BF16)
    rwh = rwh_ref[...]
    lg_ref[...] = _dot_nt(rwh, h_hi) + _dot_nt(rwl_ref[...], h_hi) + _dot_nt(rwh, h_lo)


def _outproj(o_s, o_p, w_bf16, y, mod, layer, norm_w, rw_hi, rw_lo):
    y_s, y_p = y
    ns, l, d = y_s.shape
    nb = ns + 1
    kdim = o_s.shape[2]
    ne = rw_hi.shape[0]
    tm = _pick(l, (512, 256, 128))
    ni = l // tm
    dp = jax.eval_shape(_pack_rows, jax.ShapeDtypeStruct((8, d), F32)).shape[1]
    outs = pl.pallas_call(
        functools.partial(_outproj_kernel, ns=ns),
        out_shape=[jax.ShapeDtypeStruct(y_s.shape, F32), jax.ShapeDtypeStruct(y_p.shape, F32),
                   jax.ShapeDtypeStruct((nb, l, dp), U32), jax.ShapeDtypeStruct((nb, ne, l), F32)],
        grid=(nb, ni),
        in_specs=_pair_specs(ns, ni, tm, kdim) + [
            pl.BlockSpec((kdim, d), lambda b, i: (0, 0), pipeline_mode=pl.Buffered(1)),
        ] + _pair_specs(ns, ni, tm, d) + [
            _mod_spec(d, layer, 2),
            pl.BlockSpec((1, d), lambda b, i: (0, 0)),
            _mod_spec(d, layer, 3),
            _mod_spec(d, layer, 4),
            pl.BlockSpec((ne, d), lambda b, i: (0, 0)),
            pl.BlockSpec((ne, d), lambda b, i: (0, 0)),
        ],
        out_specs=_pair_specs(ns, ni, tm, d) + [
            pl.BlockSpec((None, tm, dp), lambda b, i: (b, i, 0)),
            pl.BlockSpec((None, ne, tm), lambda b, i: (b, 0, i))],
        compiler_params=_params(("arbitrary", "arbitrary")),
        name="outproj_router",
    )(o_s, o_p, w_bf16, y_s, y_p, mod, norm_w.reshape(1, d), mod, mod, rw_hi, rw_lo)
    return (outs[0], outs[1]), outs[2], outs[3]


def _router_kernel(lg_ref, b_ref, bkt_ref, rank_ref, wt_ref, cnt_ref, carry_ref):
    ng = N_GROUPS
    sc = jax.nn.sigmoid(lg_ref[...])
    sel = sc + b_ref[...]
    sj = [sel[j * ng:(j + 1) * ng] for j in range(EXPERTS_PER_GROUP)]
    pj = [sc[j * ng:(j + 1) * ng] for j in range(EXPERTS_PER_GROUP)]
    hi1, lo1 = jnp.maximum(sj[0], sj[1]), jnp.minimum(sj[0], sj[1])
    hi2, lo2 = jnp.maximum(sj[2], sj[3]), jnp.minimum(sj[2], sj[3])
    top1 = jnp.maximum(hi1, hi2)
    top2 = jnp.maximum(jnp.minimum(hi1, hi2), jnp.maximum(lo1, lo2))
    gs = top1 + top2
    gmax = jnp.max(gs, axis=0, keepdims=True)
    gi = lax.broadcasted_iota(I32, gs.shape, 0)
    best = jnp.min(jnp.where(gs == gmax, gi, ng), axis=0, keepdims=True)
    onehot = gi == best
    neg = -jnp.inf
    vj = [jnp.max(jnp.where(onehot, a, neg), axis=0, keepdims=True) for a in sj]
    wj = [jnp.max(jnp.where(onehot, a, neg), axis=0, keepdims=True) for a in pj]

    def argmax4(vals):
        m, i = vals[0], jnp.zeros(vals[0].shape, I32)
        for j in range(1, EXPERTS_PER_GROUP):
            gt = vals[j] > m
            m = jnp.where(gt, vals[j], m)
            i = jnp.where(gt, j, i)
        return i

    i1 = argmax4(vj)
    i2 = argmax4([jnp.where(i1 == j, neg, vj[j]) for j in range(EXPERTS_PER_GROUP)])

    def take(vals, i):
        out = vals[0]
        for j in range(1, EXPERTS_PER_GROUP):
            out = jnp.where(i == j, vals[j], out)
        return out

    w1, w2 = take(wj, i1), take(wj, i2)
    tot = w1 + w2
    w1, w2 = w1 / tot, w2 / tot

    lo, hi = jnp.minimum(i1, i2), jnp.maximum(i1, i2)
    pair = jnp.where(lo == 0, hi - 1, jnp.where(lo == 1, jnp.where(hi == 3, 3, 4), 5))
    slot_a = jnp.where(pair < 3, 0, jnp.where(pair < 5, 1, 3))
    bucket = best * N_PAIRS + pair
    a_first = i1 == slot_a
    w_a = jnp.where(a_first, w1, w2)
    w_b = jnp.where(a_first, w2, w1)

    tl = bucket.shape[1]
    nbk = carry_ref.shape[0]

    @pl.when(jnp.logical_and(pl.program_id(0) == 0, pl.program_id(1) == 0))
    def _():
        carry_ref[...] = jnp.zeros_like(carry_ref)

    onehot_b = lax.broadcasted_iota(I32, (nbk, tl), 0) == bucket
    upper = (lax.broadcasted_iota(I32, (tl, tl), 0) <= lax.broadcasted_iota(I32, (tl, tl), 1))
    prefix = _dot(onehot_b.astype(BF16), upper.astype(BF16))
    carry = carry_ref[...]
    rank = jnp.sum(jnp.where(onehot_b, prefix - 1.0 + carry, 0.0), axis=0, keepdims=True)
    carry = carry + prefix[:, tl - 1:tl]
    carry_ref[...] = carry
    bkt_ref[...] = bucket
    rank_ref[...] = rank.astype(I32)
    cnt_ref[...] = jnp.broadcast_to(carry, cnt_ref.shape)
    rows = lax.broadcasted_iota(I32, (wt_ref.shape[1], tl), 0)
    wt_ref[...] = jnp.where(rows == 0, w_a, jnp.where(rows == 1, w_b, 0.0)).T


def _router(logits, bias_perm):
    nb, ne, l = logits.shape
    tl = _pick(l, (512, 256, 128))
    nbk = N_GROUPS * N_PAIRS
    return pl.pallas_call(
        _router_kernel,
        out_shape=[jax.ShapeDtypeStruct((nb, 1, l), I32), jax.ShapeDtypeStruct((nb, 1, l), I32),
                   jax.ShapeDtypeStruct((nb, l, 128), F32), jax.ShapeDtypeStruct((nbk, 128), F32)],
        grid=(nb, l // tl),
        in_specs=[pl.BlockSpec((None, ne, tl), lambda b, i: (b, 0, i)),
                  pl.BlockSpec((ne, 1), lambda b, i: (0, 0))],
        out_specs=[pl.BlockSpec((None, 1, tl), lambda b, i: (b, 0, i)),
                   pl.BlockSpec((None, 1, tl), lambda b, i: (b, 0, i)),
                   pl.BlockSpec((None, tl, 128), lambda b, i: (b, i, 0)),
                   pl.BlockSpec((nbk, 128), lambda b, i: (0, 0))],
        scratch_shapes=[pltpu.VMEM((nbk, 1), F32)],
        compiler_params=_params(("arbitrary", "arbitrary")),
        name="router_rank",
    )(logits, bias_perm.reshape(ne, 1))


def _moe_kernel(ea_ref, eb_ref, nu_ref, x_ref, wga_ref, wua_ref, wda_ref, wgb_ref, wub_ref, wdb_ref, o_ref):
    del ea_ref, eb_ref
    i = pl.program_id(0)
    d = o_ref.shape[1] // 2

    @pl.when(i < nu_ref[0])
    def _():
        x = _unpack_rows(x_ref[...])

        def expert(wg, wu, wd):
            hid = jax.nn.silu(_dot(x, wg[...])) * _dot(x, wu[...])
            return _dot(hid.astype(BF16), wd[...])

        o_ref[:, :d] = expert(wga_ref, wua_ref, wda_ref)
        o_ref[:, d:] = expert(wgb_ref, wub_ref, wdb_ref)

    @pl.when(i >= nu_ref[0])
    def _():
        o_ref[...] = jnp.zeros_like(o_ref)


def _moe_experts(xs, blk_a, blk_b, n_used, wg, wu, wd, bm):
    p, dp = xs.shape
    _, d, de = wg.shape
    nblk = p // bm
    w_in = lambda sel: pl.BlockSpec((None, d, de), lambda i, ea, eb, nu: ((ea, eb)[sel][i], 0, 0))
    w_out = lambda sel: pl.BlockSpec((None, de, d), lambda i, ea, eb, nu: ((ea, eb)[sel][i], 0, 0))
    grid_spec = pltpu.PrefetchScalarGridSpec(
        num_scalar_prefetch=3,
        grid=(nblk,),
        in_specs=[pl.BlockSpec((bm, dp), lambda i, ea, eb, nu: (i, 0)),
                  w_in(0), w_in(0), w_out(0), w_in(1), w_in(1), w_out(1)],
        out_specs=pl.BlockSpec((bm, 2 * d), lambda i, ea, eb, nu: (i, 0)),
    )
    return pl.pallas_call(
        _moe_kernel,
        out_shape=jax.ShapeDtypeStruct((p, 2 * d), F32),
        grid_spec=grid_spec,
        compiler_params=_params(("arbitrary",)),
        name="moe_experts",
    )(blk_a, blk_b, n_used, xs, wg, wu, wd, wg, wu, wd)


def _row_plan(bucket, rank, counts, bm):
    t = bucket.size
    nbk = N_GROUPS * N_PAIRS
    cnt = counts[:, 0].astype(I32)
    padded = (cnt + bm - 1) // bm * bm
    pends = jnp.cumsum(padded)
    pstarts = pends - padded
    onehot = bucket.reshape(t, 1) == jnp.arange(nbk, dtype=I32)[None, :]
    dest = jnp.sum(jnp.where(onehot, pstarts[None, :], 0), axis=1) + rank.reshape(t)
    p = (t + bm - 1) // bm * bm + nbk * bm
    nblk = p // bm
    blk_bucket = jnp.clip(jnp.searchsorted(pends, jnp.arange(nblk, dtype=I32) * bm, side='right'), 0, nbk - 1)
    grp, pair = blk_bucket // N_PAIRS, blk_bucket % N_PAIRS
    blk_a = grp * EXPERTS_PER_GROUP + jnp.asarray(_PAIR_A, I32)[pair]
    blk_b = grp * EXPERTS_PER_GROUP + jnp.asarray(_PAIR_B, I32)[pair]
    n_used = (pends[-1:] // bm).astype(I32)
    return dest.astype(I32), blk_a.astype(I32), blk_b.astype(I32), n_used, p


def _scatter_kernel(dest_ref, src_ref, init_hbm, out_hbm, sem, *, rows):
    del init_hbm
    base = pl.program_id(0) * rows

    def issue(r, carry):
        pltpu.make_async_copy(src_ref.at[r], out_hbm.at[dest_ref[base + r]], sem).start()
        return carry

    lax.fori_loop(0, rows, issue, 0, unroll=8)
    pltpu.make_async_copy(src_ref, out_hbm.at[pl.ds(0, rows)], sem).wait()


def _scatter_rows(h_rows, dest, p):
    t, dp = h_rows.shape
    rows = _pick(t, (2048, 1024, 512))
    grid_spec = pltpu.PrefetchScalarGridSpec(
        num_scalar_prefetch=1,
        grid=(t // rows,),
        in_specs=[pl.BlockSpec((rows, dp), lambda i, dst: (i, 0)), pl.BlockSpec(memory_space=pl.ANY)],
        out_specs=pl.BlockSpec(memory_space=pl.ANY),
        scratch_shapes=[pltpu.SemaphoreType.DMA],
    )
    return pl.pallas_call(
        functools.partial(_scatter_kernel, rows=rows),
        out_shape=jax.ShapeDtypeStruct((p, dp), h_rows.dtype),
        grid_spec=grid_spec,
        input_output_aliases={2: 0},
        compiler_params=_params(("arbitrary",)),
        name="moe_scatter_rows",
    )(dest, h_rows, jnp.zeros((p, dp), h_rows.dtype))


def _combine_kernel(dest_ref, ys_ref, yp_ref, g_ref, wt_ref, m_hbm, os_ref, op_ref, mbuf, sem, *, ns, tm, nsteps):
    step = pl.program_id(0) * pl.num_programs(1) + pl.program_id(1)
    slot = step % 2
    d = ys_ref.shape[1]

    def gather(s, sl):
        def issue(r, carry):
            pltpu.make_async_copy(m_hbm.at[dest_ref[s * tm + r]], mbuf.at[sl, r], sem.at[sl]).start()
            return carry
        lax.fori_loop(0, tm, issue, 0, unroll=8)

    @pl.when(step == 0)
    def _():
        gather(0, 0)

    @pl.when(step + 1 < nsteps)
    def _():
        gather(step + 1, 1 - slot)

    pltpu.make_async_copy(m_hbm.at[pl.ds(0, tm)], mbuf.at[slot], sem.at[slot]).wait()
    m = mbuf[slot]
    w = wt_ref[...]
    y = _pair_load(ys_ref, yp_ref, ns)
    _pair_store(os_ref, op_ref, ns, y + g_ref[...] * (w[:, 0:1] * m[:, :d] + w[:, 1:2] * m[:, d:]))


def _combine(y, mod, layer, moe_rows, dest, wt):
    y_s, y_p = y
    ns, l, d = y_s.shape
    nb = ns + 1
    tm = _pick(l, (512, 256, 128))
    ni = l // tm
    grid_spec = pltpu.PrefetchScalarGridSpec(
        num_scalar_prefetch=1,
        grid=(nb, ni),
        in_specs=_pair_specs(ns, ni, tm, d) + [
            pl.BlockSpec((None, None, None, 1, d), lambda b, i, dst: (layer, 5, b, 0, 0)),
            pl.BlockSpec((None, tm, 128), lambda b, i, dst: (b, i, 0)),
            pl.BlockSpec(memory_space=pl.ANY),
        ],
        out_specs=_pair_specs(ns, ni, tm, d),
        scratch_shapes=[pltpu.VMEM((2, tm, 2 * d), F32), pltpu.SemaphoreType.DMA((2,))],
    )
    outs = pl.pallas_call(
        functools.partial(_combine_kernel, ns=ns, tm=tm, nsteps=nb * ni),
        out_shape=[jax.ShapeDtypeStruct(y_s.shape, F32), jax.ShapeDtypeStruct(y_p.shape, F32)],
        grid_spec=grid_spec,
        compiler_params=_params(("arbitrary", "arbitrary")),
        name="moe_combine",
    )(dest, y_s, y_p, mod, wt, moe_rows)
    return (outs[0], outs[1])


def kernel(x_prompt, x_sample, c, state_ret_fwd, state_ret_bwd, cache_attn_k, cache_attn_v, c_ctx,
           ada_w, ada_b, norm1_w, norm2_w, ret_w_in, ret_w_out, ret_gn_w, ret_decay_fwd, ret_decay_bwd,
           attn_w_qkv, attn_w_o, attn_q_norm, attn_k_norm, router_w, router_b,
           moe_w_gate, moe_w_up, moe_w_down):
    n_prompt, seq, d = x_prompt.shape
    ns, l, _ = x_sample.shape
    assert n_prompt * seq == l, "prompt tokens must fill exactly one sample-length row"
    depth = ada_w.shape[0]
    nb = ns + 1
    dk = d // RET_HEADS
    dv = 2 * dk
    dh = d // ATT_HEADS
    bm = MOE_ROW_BLOCK

    y = (x_sample, x_prompt.reshape(1, l, d))
    nbp = (nb + 7) // 8 * 8
    cond = jnp.zeros((nbp, d), F32).at[:ns].set(c).at[ns].set(c_ctx)
    mod = _ada_params(cond, ada_w, ada_b)

    perm = (jnp.arange(N_EXPERTS) % N_GROUPS) * EXPERTS_PER_GROUP + jnp.arange(N_EXPERTS) // N_GROUPS
    rw_t = router_w.T[perm]
    rw_hi = rw_t.astype(BF16)
    rw_lo = (rw_t - rw_hi.astype(F32)).astype(BF16)
    rb_perm = router_b[perm]

    ret_f, ret_b, k_new, v_new = [], [], [], []
    for i in range(depth):
        if i % 2 == 0:
            r = i // 2
            qkvg = _ret_inproj(y, mod, i, norm1_w[i], ret_w_in[r].astype(BF16), ret_gn_w[r], dk)
            o_s = _retention(qkvg, ns, 0, l, dk, ret_decay_fwd[r], ret_decay_bwd[r],
                             s0f=state_ret_fwd[:, r], s0b=state_ret_bwd[:, r])
            qkvg_p = qkvg.reshape(nb * n_prompt, seq, qkvg.shape[2])
            o_p, s_f, s_b = _retention(qkvg_p, n_prompt, ns * n_prompt, seq, dk,
                                       ret_decay_fwd[r], ret_decay_bwd[r], emit_final=True)
            o_s = o_s[0] if isinstance(o_s, (list, tuple)) else o_s
            ret_f.append(s_f)
            ret_b.append(s_b)
            w_o = ret_w_out[r].astype(BF16)
        else:
            a = i // 2
            qkv, k_p, v_p = _qkv_proj(y, mod, i, norm1_w[i], attn_w_qkv[a].astype(BF16),
                                      attn_q_norm[a], attn_k_norm[a], dh)
            past = cache_attn_k.shape[2]
            o_s = _attention(qkv, ns, 0, l, dh,
                             cache_k=cache_attn_k[:, a].reshape(ns, past, ATT_KV_HEADS * dh),
                             cache_v=cache_attn_v[:, a].reshape(ns, past, ATT_KV_HEADS * dh))
            o_p = _attention(qkv.reshape(nb * n_prompt, seq, qkv.shape[2]), n_prompt, ns * n_prompt, seq, dh)
            k_new.append(k_p.reshape(n_prompt, seq, ATT_KV_HEADS, dh))
            v_new.append(v_p.reshape(n_prompt, seq, ATT_KV_HEADS, dh))
            w_o = attn_w_o[a].astype(BF16)
        o_p = o_p.reshape(1, l, o_p.shape[-1])
        y, h2, logits = _outproj(o_s, o_p, w_o, y, mod, i, norm2_w[i], rw_hi, rw_lo)
        bucket, rank, wt, counts = _router(logits, rb_perm)
        dest, blk_a, blk_b, n_used, p = _row_plan(bucket, rank, counts, bm)
        xs = _scatter_rows(h2.reshape(nb * l, h2.shape[2]), dest, p)
        moe_rows = _moe_experts(xs, blk_a, blk_b, n_used, moe_w_gate[i].astype(BF16),
                                moe_w_up[i].astype(BF16), moe_w_down[i].astype(BF16), bm)
        y = _combine(y, mod, i, moe_rows, dest, wt)

    y_s, y_p = y
    return (y_p.reshape(n_prompt, seq, d), y_s, jnp.stack(ret_f, axis=1), jnp.stack(ret_b, axis=1),
            jnp.stack(k_new, axis=1), jnp.stack(v_new, axis=1))
```

```python
import functools

import jax
import jax.numpy as jnp
from jax import lax
from jax.experimental import pallas as pl
from jax.experimental.pallas import tpu as pltpu

F32 = jnp.float32
BF16 = jnp.bfloat16
I32 = jnp.int32
U32 = jnp.uint32

EPS = 1e-6
GRID_W = 64
RET_HEADS = 4
ATT_HEADS = 8
ATT_KV_HEADS = 2
ROPE_THETA = 10000.0
N_EXPERTS = 32
N_GROUPS = 8
EXPERTS_PER_GROUP = N_EXPERTS // N_GROUPS
TOP_K = 2
_PAIR_A = (0, 0, 0, 1, 1, 3)
_PAIR_B = (1, 2, 3, 3, 2, 2)
N_PAIRS = len(_PAIR_A)
MOE_ROW_BLOCK = 256

V7X_VMEM_BYTES = 64 * 1024 * 1024
VMEM_LIMIT_BYTES = 56 * 1024 * 1024


def _params(sem):
    return pltpu.CompilerParams(dimension_semantics=sem, vmem_limit_bytes=VMEM_LIMIT_BYTES)


def _dot(a, b):
    return jnp.dot(a, b, preferred_element_type=F32)


def _dot_nt(a, b):
    return lax.dot_general(a, b, (((1,), (1,)), ((), ())), preferred_element_type=F32)


def _dot_tn(a, b):
    return lax.dot_general(a, b, (((0,), (0,)), ((), ())), preferred_element_type=F32)


def _pick(n, prefs):
    for p in prefs:
        if n % p == 0:
            return p
    return n


def _ada_kernel(c_ref, w_ref, b_ref, o_ref):
    c = c_ref[...]
    s = jax.nn.silu(c).astype(BF16)
    o_ref[...] = _dot(s, w_ref[...].astype(BF16)) + b_ref[...]


def _ada_params(cond, ada_w, ada_b):
    depth, d, n6 = ada_w.shape
    nbp = cond.shape[0]
    tn = _pick(n6, (1536, 1024, 512, 256, 128))
    out = pl.pallas_call(
        _ada_kernel,
        out_shape=jax.ShapeDtypeStruct((depth, nbp, n6), F32),
        grid=(depth, n6 // tn),
        in_specs=[
            pl.BlockSpec((nbp, d), lambda l, j: (0, 0)),
            pl.BlockSpec((None, d, tn), lambda l, j: (l, 0, j)),
            pl.BlockSpec((None, 1, tn), lambda l, j: (l, 0, j)),
        ],
        out_specs=pl.BlockSpec((None, nbp, tn), lambda l, j: (l, 0, j)),
        compiler_params=_params(("arbitrary", "arbitrary")),
        name="ada_params",
    )(cond, ada_w, ada_b.reshape(depth, 1, n6))
    return out.reshape(depth, nbp, 6, 1, d).transpose(0, 2, 1, 3, 4)


def _mod_spec(d, layer, j):
    return pl.BlockSpec((None, None, None, 1, d), lambda b, i: (layer, j, b, 0, 0))


def _pair_specs(ns, ni, tm, w):
    return [pl.BlockSpec((None, tm, w), lambda b, i, *_: (jnp.minimum(b, ns - 1), jnp.where(b < ns, i, ni - 1), 0)),
            pl.BlockSpec((None, tm, w), lambda b, i, *_: (0, jnp.where(b < ns, 0, i), 0))]


def _pair_load(s_ref, p_ref, ns):
    return jnp.where(pl.program_id(0) >= ns, p_ref[...], s_ref[...])


def _pair_store(s_ref, p_ref, ns, val):
    @pl.when(pl.program_id(0) < ns)
    def _():
        s_ref[...] = val

    @pl.when(pl.program_id(0) >= ns)
    def _():
        p_ref[...] = val


def _norm_mod(x, nw, sh, sc):
    ms = jnp.mean(x * x, axis=-1, keepdims=True)
    h = x * lax.rsqrt(ms + EPS) * nw
    return h * (1.0 + sc) + sh


def _ret_inproj_kernel(xs_ref, xp_ref, nw_ref, sh_ref, sc_ref, w_ref, gn_ref, o_ref, *, ns, hq, hv, tn, k_scale):
    x = _pair_load(xs_ref, xp_ref, ns)
    hb = _norm_mod(x, nw_ref[...], sh_ref[...], sc_ref[...]).astype(BF16)
    n = w_ref.shape[1]
    for j in range(n // tn):
        lo = j * tn
        acc = _dot(hb, w_ref[:, lo:lo + tn])
        if hq <= lo < 2 * hq:
            acc = acc * k_scale
        elif lo >= 2 * hq + hv:
            g0 = (lo - 2 * hq - hv) % hv
            acc = jax.nn.silu(acc) * gn_ref[:, g0:g0 + tn]
        o_ref[:, lo:lo + tn] = acc.astype(BF16)


def _ret_inproj(y, mod, layer, norm_w, w_in_bf16, gn_w, dk):
    y_s, y_p = y
    ns, l, d = y_s.shape
    nb = ns + 1
    n = w_in_bf16.shape[1]
    hq = RET_HEADS * dk
    hv = 2 * hq
    tm = _pick(l, (512, 256, 128))
    tn = _pick(hq, (1024, 512, 256, 128))
    kern = functools.partial(_ret_inproj_kernel, ns=ns, hq=hq, hv=hv, tn=tn, k_scale=float(dk) ** -0.5)
    return pl.pallas_call(
        kern,
        out_shape=jax.ShapeDtypeStruct((nb, l, n), BF16),
        grid=(nb, l // tm),
        in_specs=_pair_specs(ns, l // tm, tm, d) + [
            pl.BlockSpec((1, d), lambda b, i: (0, 0)),
            _mod_spec(d, layer, 0),
            _mod_spec(d, layer, 1),
            pl.BlockSpec((d, n), lambda b, i: (0, 0), pipeline_mode=pl.Buffered(1)),
            pl.BlockSpec((1, hv), lambda b, i: (0, 0)),
        ],
        out_specs=pl.BlockSpec((None, tm, n), lambda b, i: (b, i, 0)),
        compiler_params=_params(("arbitrary", "arbitrary")),
        name="ret_inproj",
    )(y_s, y_p, norm_w.reshape(1, d), mod, mod, w_in_bf16, gn_w.reshape(1, hv))


def _log_sigmoid(x):
    return jnp.minimum(x, 0.0) - jnp.log1p(jnp.exp(-jnp.abs(x)))


def _ret_kernel(*refs, ls, c, has_init, emit_final):
    it = iter(refs)
    q_ref, k_ref, v_ref, gf_ref, gb_ref, df_ref, db_ref = (next(it) for _ in range(7))
    if has_init:
        s0f_ref, s0b_ref = next(it), next(it)
    y_ref = next(it)
    if emit_final:
        sf_out, sb_out = next(it), next(it)
    s_scr, sb_scr = next(it), next(it)

    dk = q_ref.shape[1]
    nchunks = ls // c
    lg_f = _log_sigmoid(df_ref[...])[:, 0:1]
    lg_b = _log_sigmoid(db_ref[...])[:, 0:1]

    ri = lax.broadcasted_iota(I32, (c, c), 0)
    ci = lax.broadcasted_iota(I32, (c, c), 1)
    dif = (ri - ci).astype(F32)
    decay_f = jnp.where(dif >= 0, jnp.exp(jnp.maximum(dif, 0.0) * lg_f), 0.0)
    decay_b = jnp.where(dif <= 0, jnp.exp(jnp.maximum(-dif, 0.0) * lg_b), 0.0)
    pos = lax.broadcasted_iota(I32, (c, dk), 0).astype(F32)
    qdec_f = jnp.exp((pos + 1.0) * lg_f)
    kdec_f = jnp.exp((c - 1.0 - pos) * lg_f)
    qdec_b = jnp.exp((c - pos) * lg_b)
    kdec_b = jnp.exp(pos * lg_b)
    cdec_f = jnp.exp(c * lg_f)
    cdec_b = jnp.exp(c * lg_b)

    if has_init:
        s_scr[...] = s0b_ref[...]
    else:
        s_scr[...] = jnp.zeros_like(s_scr)

    def bwd_body(t, carry):
        n = nchunks - 1 - t
        r0 = pl.multiple_of(n * c, c)
        s = s_scr[...]
        sb_scr[n] = s.astype(BF16)
        kc = k_ref[pl.ds(r0, c), :].astype(F32)
        vc = v_ref[pl.ds(r0, c), :]
        s_scr[...] = s * cdec_b + _dot_tn((kc * kdec_b).astype(BF16), vc)
        return carry

    lax.fori_loop(0, nchunks, bwd_body, 0)
    if emit_final:
        sb_out[...] = s_scr[...]

    if has_init:
        s_scr[...] = s0f_ref[...]
    else:
        s_scr[...] = jnp.zeros_like(s_scr)

    def head_norm(o):
        mu = jnp.mean(o, axis=-1, keepdims=True)
        dlt = o - mu
        var = jnp.mean(dlt * dlt, axis=-1, keepdims=True)
        return dlt * lax.rsqrt(var + EPS)

    def fwd_body(n, carry):
        r0 = pl.multiple_of(n * c, c)
        rows = pl.ds(r0, c)
        qc = q_ref[rows, :]
        kc = k_ref[rows, :]
        vc = v_ref[rows, :]
        qf = qc.astype(F32)
        sc = _dot_nt(qc, kc)
        s = s_scr[...]
        o_f = _dot((sc * decay_f).astype(BF16), vc) + _dot((qf * qdec_f).astype(BF16), s.astype(BF16))
        o_b = _dot((sc * decay_b).astype(BF16), vc) + _dot((qf * qdec_b).astype(BF16), sb_scr[n])
        s_scr[...] = s * cdec_f + _dot_tn((kc.astype(F32) * kdec_f).astype(BF16), vc)
        y = head_norm(o_f) * gf_ref[rows, :].astype(F32) + head_norm(o_b) * gb_ref[rows, :].astype(F32)
        y_ref[rows, :] = y.astype(BF16)
        return carry

    lax.fori_loop(0, nchunks, fwd_body, 0, unroll=2 if nchunks % 2 == 0 else 1)
    if emit_final:
        sf_out[...] = s_scr[...]


def _retention(qkvg, nseq, seq0, ls, dk, decay_f, decay_b, s0f=None, s0b=None, emit_final=False):
    h = RET_HEADS
    dv = 2 * dk
    c = _pick(ls, (256, 128))
    has_init = s0f is not None
    kq, kk, kv, kgf, kgb = 0, h, h, 2 * h, 3 * h
    in_specs = [
        pl.BlockSpec((None, ls, dk), lambda s, hh: (s + seq0, 0, kq + hh)),
        pl.BlockSpec((None, ls, dk), lambda s, hh: (s + seq0, 0, kk + hh)),
        pl.BlockSpec((None, ls, dv), lambda s, hh: (s + seq0, 0, kv + hh)),
        pl.BlockSpec((None, ls, dv), lambda s, hh: (s + seq0, 0, kgf + hh)),
        pl.BlockSpec((None, ls, dv), lambda s, hh: (s + seq0, 0, kgb + hh)),
        pl.BlockSpec((None, 1, 128), lambda s, hh: (hh, 0, 0)),
        pl.BlockSpec((None, 1, 128), lambda s, hh: (hh, 0, 0)),
    ]
    args = [qkvg, qkvg, qkvg, qkvg, qkvg,
            jnp.broadcast_to(decay_f.reshape(h, 1, 1), (h, 1, 128)),
            jnp.broadcast_to(decay_b.reshape(h, 1, 1), (h, 1, 128))]
    if has_init:
        in_specs += [pl.BlockSpec((None, None, dk, dv), lambda s, hh: (s, hh, 0, 0))] * 2
        args += [s0f, s0b]
    out_shape = [jax.ShapeDtypeStruct((nseq, ls, h * dv), BF16)]
    out_specs = [pl.BlockSpec((None, ls, dv), lambda s, hh: (s, 0, hh))]
    if emit_final:
        out_shape += [jax.ShapeDtypeStruct((nseq, h, dk, dv), F32)] * 2
        out_specs += [pl.BlockSpec((None, None, dk, dv), lambda s, hh: (s, hh, 0, 0))] * 2
    kern = functools.partial(_ret_kernel, ls=ls, c=c, has_init=has_init, emit_final=emit_final)
    return pl.pallas_call(
        kern,
        out_shape=out_shape,
        grid=(nseq, h),
        in_specs=in_specs,
        out_specs=out_specs,
        scratch_shapes=[pltpu.VMEM((dk, dv), F32), pltpu.VMEM((ls // c, dk, dv), BF16)],
        compiler_params=_params(("arbitrary", "arbitrary")),
        name="retention_init" if has_init else "retention_zero",
    )(*args)


def _qkv_kernel(xs_ref, xp_ref, nw_ref, sh_ref, sc_ref, w_ref, qn_ref, kn_ref, cos_ref, sa_ref, sb_ref,
                o_ref, kp_ref, vp_ref, *, ns, dh):
    x = _pair_load(xs_ref, xp_ref, ns)
    hb = _norm_mod(x, nw_ref[...], sh_ref[...], sc_ref[...]).astype(BF16)
    nq = ATT_HEADS * dh
    nk = ATT_KV_HEADS * dh
    cos, sa, sb = cos_ref[...], sa_ref[...], sb_ref[...]
    is_prompt = pl.program_id(0) >= ns

    def norm_rope(a, w):
        a = a * lax.rsqrt(jnp.mean(a * a, axis=-1, keepdims=True) + EPS) * w
        roped = a * cos + pltpu.roll(a, dh - dh // 4, 1) * sa + pltpu.roll(a, dh // 4, 1) * sb
        return a, roped

    q = _dot(hb, w_ref[:, :nq])
    for hh in range(ATT_HEADS):
        _, r = norm_rope(q[:, hh * dh:(hh + 1) * dh], qn_ref[...])
        o_ref[:, hh * dh:(hh + 1) * dh] = r.astype(BF16)
    kvv = _dot(hb, w_ref[:, nq:])
    k_norm = []
    for hh in range(ATT_KV_HEADS):
        a, r = norm_rope(kvv[:, hh * dh:(hh + 1) * dh], kn_ref[...])
        o_ref[:, nq + hh * dh:nq + (hh + 1) * dh] = r.astype(BF16)
        k_norm.append(a)
    o_ref[:, nq + nk:] = kvv[:, nk:].astype(BF16)

    @pl.when(is_prompt)
    def _():
        for hh in range(ATT_KV_HEADS):
            kp_ref[:, hh * dh:(hh + 1) * dh] = k_norm[hh]
        vp_ref[...] = kvv[:, nk:]


def _rope_tables(l, dh, rotate):
    axis_dim = dh // 2
    half = axis_dim // 2
    t = jnp.arange(l)
    row = (t // GRID_W).astype(F32)
    col = (t % GRID_W).astype(F32)
    inv = ROPE_THETA ** (-jnp.arange(0, axis_dim, 2, dtype=F32) / axis_dim)
    d = jnp.arange(dh)
    pos = jnp.where((d // axis_dim)[None, :] == 0, row[:, None], col[:, None])
    ang = pos * inv[d % half][None, :]
    first = ((d % axis_dim) < half)[None, :]
    cos, sin = jnp.cos(ang), jnp.sin(ang)
    if not rotate:
        cos, sin = jnp.ones_like(cos), jnp.zeros_like(sin)
    return cos, jnp.where(first, -sin, 0.0), jnp.where(first, 0.0, sin)


def _qkv_proj(y, mod, layer, norm_w, w_bf16, qn, kn, dh):
    y_s, y_p = y
    ns, l, d = y_s.shape
    nb = ns + 1
    n = w_bf16.shape[1]
    nk = ATT_KV_HEADS * dh
    tm = _pick(l, (512, 256, 128))
    tabs = [jnp.stack([a, b]) for a, b in zip(_rope_tables(l, dh, True), _rope_tables(l, dh, False))]
    tab_spec = pl.BlockSpec((None, tm, dh), lambda b, i: (jnp.where(b < ns, 0, 1), i, 0))
    prompt_spec = pl.BlockSpec((tm, nk), lambda b, i: (jnp.where(b < ns, 0, i), 0))
    return pl.pallas_call(
        functools.partial(_qkv_kernel, ns=ns, dh=dh),
        out_shape=[jax.ShapeDtypeStruct((nb, l, n), BF16), jax.ShapeDtypeStruct((l, nk), F32),
                   jax.ShapeDtypeStruct((l, nk), F32)],
        grid=(nb, l // tm),
        in_specs=_pair_specs(ns, l // tm, tm, d) + [
            pl.BlockSpec((1, d), lambda b, i: (0, 0)),
            _mod_spec(d, layer, 0),
            _mod_spec(d, layer, 1),
            pl.BlockSpec((d, n), lambda b, i: (0, 0), pipeline_mode=pl.Buffered(1)),
            pl.BlockSpec((1, dh), lambda b, i: (0, 0)),
            pl.BlockSpec((1, dh), lambda b, i: (0, 0)),
            tab_spec, tab_spec, tab_spec,
        ],
        out_specs=[pl.BlockSpec((None, tm, n), lambda b, i: (b, i, 0)), prompt_spec, prompt_spec],
        compiler_params=_params(("arbitrary", "arbitrary")),
        name="qkv_proj",
    )(y_s, y_p, norm_w.reshape(1, d), mod, mod, w_bf16, qn.reshape(1, dh), kn.reshape(1, dh), *tabs)


def _attn_kernel(*refs, dh, group, has_cache, scale):
    if has_cache:
        q_ref, k_ref, v_ref, kc_ref, vc_ref, o_ref, kall, vext = refs
        past = kc_ref.shape[0]
    else:
        q_ref, k_ref, v_ref, o_ref, kall, vext = refs
        past = 0

    @pl.when(pl.program_id(2) == 0)
    def _():
        if has_cache:
            kall[:past, :] = kc_ref[...].astype(BF16)
            vext[:past, :dh] = vc_ref[...].astype(BF16)
        kall[past:, :] = k_ref[...]
        vext[past:, :dh] = v_ref[...]
        vext[:, dh:] = jnp.ones((vext.shape[0], dh), BF16)

    c = scale * 1.4426950408889634
    tq = q_ref.shape[0]
    q = jnp.concatenate([q_ref[:, g * dh:(g + 1) * dh] for g in range(group)], axis=0)
    lk = kall.shape[0]
    ck = _pick(lk, (512, 256, 128))
    m = None
    for j in range(lk // ck):
        s = _dot_nt(q, kall[j * ck:(j + 1) * ck, :])
        mj = jnp.max(s, axis=-1, keepdims=True)
        vj = vext[j * ck:(j + 1) * ck, :]
        if m is None:
            m = mj
            o = _dot(jnp.exp2((s - m) * c).astype(BF16), vj)
        else:
            m_new = jnp.maximum(m, mj)
            o = o * jnp.exp2((m - m_new) * c) + _dot(jnp.exp2((s - m_new) * c).astype(BF16), vj)
            m = m_new
    for g in range(group):
        og = o[g * tq:(g + 1) * tq]
        o_ref[:, g * dh:(g + 1) * dh] = (og[:, :dh] / og[:, dh:dh + 1]).astype(BF16)


def _attention(qkv, nseq, seq0, ls, dh, cache_k=None, cache_v=None):
    group = ATT_HEADS // ATT_KV_HEADS
    has_cache = cache_k is not None
    tq = _pick(ls, (256, 128))
    kcol = ATT_HEADS
    vcol = ATT_HEADS + ATT_KV_HEADS
    in_specs = [
        pl.BlockSpec((None, tq, group * dh), lambda s, kh, i: (s + seq0, i, kh)),
        pl.BlockSpec((None, ls, dh), lambda s, kh, i: (s + seq0, 0, kcol + kh)),
        pl.BlockSpec((None, ls, dh), lambda s, kh, i: (s + seq0, 0, vcol + kh)),
    ]
    args = [qkv, qkv, qkv]
    past = 0
    if has_cache:
        past = cache_k.shape[1]
        in_specs += [pl.BlockSpec((None, past, dh), lambda s, kh, i: (s, 0, kh))] * 2
        args += [cache_k, cache_v]
    kern = functools.partial(_attn_kernel, dh=dh, group=group, has_cache=has_cache, scale=float(dh) ** -0.5)
    return pl.pallas_call(
        kern,
        out_shape=jax.ShapeDtypeStruct((nseq, ls, ATT_HEADS * dh), BF16),
        grid=(nseq, ATT_KV_HEADS, ls // tq),
        in_specs=in_specs,
        out_specs=pl.BlockSpec((None, tq, group * dh), lambda s, kh, i: (s, i, kh)),
        scratch_shapes=[pltpu.VMEM((past + ls, dh), BF16), pltpu.VMEM((past + ls, 2 * dh), BF16)],
        compiler_params=_params(("arbitrary", "arbitrary", "arbitrary")),
        name="attention_cache" if has_cache else "attention_self",
    )(*args)


def _pack_rows(h):
    half = h.shape[1] // 2
    u = lax.bitcast_convert_type(h.astype(BF16).astype(F32), U32)
    return (u[:, :half] >> 16) | (u[:, half:] & jnp.uint32(0xFFFF0000))


def _unpack_rows(p):
    lo = lax.bitcast_convert_type(p << 16, F32)
    hi = lax.bitcast_convert_type(p & jnp.uint32(0xFFFF0000), F32)
    return jnp.concatenate([lo, hi], axis=1).astype(BF16)


def _outproj_kernel(os_ref, op_ref, w_ref, ys_ref, yp_ref, g1_ref, nw_ref, sh_ref, sc_ref, rwh_ref, rwl_ref,
                    yos_ref, yop_ref, h_ref, lg_ref, *, ns):
    o = _pair_load(os_ref, op_ref, ns)
    y = _pair_load(ys_ref, yp_ref, ns) + g1_ref[...] * _dot(o, w_ref[...])
    _pair_store(yos_ref, yop_ref, ns, y)
    h = _norm_mod(y, nw_ref[...], sh_ref[...], sc_ref[...])
    h_ref[...] = _pack_rows(h)
    h_hi = h.astype(BF16)
    h_lo = (h - h_hi.astype(F32)).astype(BF16)
    rwh = rwh_ref[...]
    lg_ref[...] = _dot_nt(rwh, h_hi) + _dot_nt(rwl_ref[...], h_hi) + _dot_nt(rwh, h_lo)


def _outproj(o_s, o_p, w_bf16, y, mod, layer, norm_w, rw_hi, rw_lo):
    y_s, y_p = y
    ns, l, d = y_s.shape
    nb = ns + 1
    kdim = o_s.shape[2]
    ne = rw_hi.shape[0]
    tm = _pick(l, (512, 256, 128))
    ni = l // tm
    dp = jax.eval_shape(_pack_rows, jax.ShapeDtypeStruct((8, d), F32)).shape[1]
    outs = pl.pallas_call(
        functools.partial(_outproj_kernel, ns=ns),
        out_shape=[jax.ShapeDtypeStruct(y_s.shape, F32), jax.ShapeDtypeStruct(y_p.shape, F32),
                   jax.ShapeDtypeStruct((nb, l, dp), U32), jax.ShapeDtypeStruct((nb, ne, l), F32)],
        grid=(nb, ni),
        in_specs=_pair_specs(ns, ni, tm, kdim) + [
            pl.BlockSpec((kdim, d), lambda b, i: (0, 0), pipeline_mode=pl.Buffered(1)),
        ] + _pair_specs(ns, ni, tm, d) + [
            _mod_spec(d, layer, 2),
            pl.BlockSpec((1, d), lambda b, i: (0, 0)),
            _mod_spec(d, layer, 3),
            _mod_spec(d, layer, 4),
            pl.BlockSpec((ne, d), lambda b, i: (0, 0)),
            pl.BlockSpec((ne, d), lambda b, i: (0, 0)),
        ],
        out_specs=_pair_specs(ns, ni, tm, d) + [
            pl.BlockSpec((None, tm, dp), lambda b, i: (b, i, 0)),
            pl.BlockSpec((None, ne, tm), lambda b, i: (b, 0, i))],
        compiler_params=_params(("arbitrary", "arbitrary")),
        name="outproj_router",
    )(o_s, o_p, w_bf16, y_s, y_p, mod, norm_w.reshape(1, d), mod, mod, rw_hi, rw_lo)
    return (outs[0], outs[1]), outs[2], outs[3]


def _router_kernel(lg_ref, b_ref, bkt_ref, rank_ref, wt_ref, cnt_ref, carry_ref):
    ng = N_GROUPS
    sc = jax.nn.sigmoid(lg_ref[...])
    sel = sc + b_ref[...]
    sj = [sel[j * ng:(j + 1) * ng] for j in range(EXPERTS_PER_GROUP)]
    pj = [sc[j * ng:(j + 1) * ng] for j in range(EXPERTS_PER_GROUP)]
    hi1, lo1 = jnp.maximum(sj[0], sj[1]), jnp.minimum(sj[0], sj[1])
    hi2, lo2 = jnp.maximum(sj[2], sj[3]), jnp.minimum(sj[2], sj[3])
    top1 = jnp.maximum(hi1, hi2)
    top2 = jnp.maximum(jnp.minimum(hi1, hi2), jnp.maximum(lo1, lo2))
    gs = top1 + top2
    gmax = jnp.max(gs, axis=0, keepdims=True)
    gi = lax.broadcasted_iota(I32, gs.shape, 0)
    best = jnp.min(jnp.where(gs == gmax, gi, ng), axis=0, keepdims=True)
    onehot = gi == best
    neg = -jnp.inf
    vj = [jnp.max(jnp.where(onehot, a, neg), axis=0, keepdims=True) for a in sj]
    wj = [jnp.max(jnp.where(onehot, a, neg), axis=0, keepdims=True) for a in pj]

    def argmax4(vals):
        m, i = vals[0], jnp.zeros(vals[0].shape, I32)
        for j in range(1, EXPERTS_PER_GROUP):
            gt = vals[j] > m
            m = jnp.where(gt, vals[j], m)
            i = jnp.where(gt, j, i)
        return i

    i1 = argmax4(vj)
    i2 = argmax4([jnp.where(i1 == j, neg, vj[j]) for j in range(EXPERTS_PER_GROUP)])

    def take(vals, i):
        out = vals[0]
        for j in range(1, EXPERTS_PER_GROUP):
            out = jnp.where(i == j, vals[j], out)
        return out

    w1, w2 = take(wj, i1), take(wj, i2)
    tot = w1 + w2
    w1, w2 = w1 / tot, w2 / tot

    lo, hi = jnp.minimum(i1, i2), jnp.maximum(i1, i2)
    pair = jnp.where(lo == 0, hi - 1, jnp.where(lo == 1, jnp.where(hi == 3, 3, 4), 5))
    slot_a = jnp.where(pair < 3, 0, jnp.where(pair < 5, 1, 3))
    bucket = best * N_PAIRS + pair
    a_first = i1 == slot_a
    w_a = jnp.where(a_first, w1, w2)
    w_b = jnp.where(a_first, w2, w1)

    tl = bucket.shape[1]
    nbk = carry_ref.shape[0]

    @pl.when(jnp.logical_and(pl.program_id(0) == 0, pl.program_id(1) == 0))
    def _():
        carry_ref[...] = jnp.zeros_like(carry_ref)

    onehot_b = lax.broadcasted_iota(I32, (nbk, tl), 0) == bucket
    upper = (lax.broadcasted_iota(I32, (tl, tl), 0) <= lax.broadcasted_iota(I32, (tl, tl), 1))
    prefix = _dot(onehot_b.astype(BF16), upper.astype(BF16))
    carry = carry_ref[...]
    rank = jnp.sum(jnp.where(onehot_b, prefix - 1.0 + carry, 0.0), axis=0, keepdims=True)
    carry = carry + prefix[:, tl - 1:tl]
    carry_ref[...] = carry
    bkt_ref[...] = bucket
    rank_ref[...] = rank.astype(I32)
    cnt_ref[...] = jnp.broadcast_to(carry, cnt_ref.shape)
    rows = lax.broadcasted_iota(I32, (wt_ref.shape[1], tl), 0)
    wt_ref[...] = jnp.where(rows == 0, w_a, jnp.where(rows == 1, w_b, 0.0)).T


def _router(logits, bias_perm):
    nb, ne, l = logits.shape
    tl = _pick(l, (512, 256, 128))
    nbk = N_GROUPS * N_PAIRS
    return pl.pallas_call(
        _router_kernel,
        out_shape=[jax.ShapeDtypeStruct((nb, 1, l), I32), jax.ShapeDtypeStruct((nb, 1, l), I32),
                   jax.ShapeDtypeStruct((nb, l, 128), F32), jax.ShapeDtypeStruct((nbk, 128), F32)],
        grid=(nb, l // tl),
        in_specs=[pl.BlockSpec((None, ne, tl), lambda b, i: (b, 0, i)),
                  pl.BlockSpec((ne, 1), lambda b, i: (0, 0))],
        out_specs=[pl.BlockSpec((None, 1, tl), lambda b, i: (b, 0, i)),
                   pl.BlockSpec((None, 1, tl), lambda b, i: (b, 0, i)),
                   pl.BlockSpec((None, tl, 128), lambda b, i: (b, i, 0)),
                   pl.BlockSpec((nbk, 128), lambda b, i: (0, 0))],
        scratch_shapes=[pltpu.VMEM((nbk, 1), F32)],
        compiler_params=_params(("arbitrary", "arbitrary")),
        name="router_rank",
    )(logits, bias_perm.reshape(ne, 1))


def _moe_kernel(ea_ref, eb_ref, nu_ref, x_ref, wga_ref, wua_ref, wda_ref, wgb_ref, wub_ref, wdb_ref, o_ref):
    del ea_ref, eb_ref
    i = pl.program_id(0)
    d = o_ref.shape[1] // 2

    @pl.when(i < nu_ref[0])
    def _():
        x = _unpack_rows(x_ref[...])

        def expert(wg, wu, wd):
            hid = jax.nn.silu(_dot(x, wg[...])) * _dot(x, wu[...])
            return _dot(hid.astype(BF16), wd[...])

        o_ref[:, :d] = expert(wga_ref, wua_ref, wda_ref)
        o_ref[:, d:] = expert(wgb_ref, wub_ref, wdb_ref)

    @pl.when(i >= nu_ref[0])
    def _():
        o_ref[...] = jnp.zeros_like(o_ref)


def _moe_experts(xs, blk_a, blk_b, n_used, wg, wu, wd, layer, bm):
    p, dp = xs.shape
    _, _, d, de = wg.shape
    nblk = p // bm
    w_in = lambda sel: pl.BlockSpec((None, None, d, de), lambda i, ea, eb, nu: (layer, (ea, eb)[sel][i], 0, 0))
    w_out = lambda sel: pl.BlockSpec((None, None, de, d), lambda i, ea, eb, nu: (layer, (ea, eb)[sel][i], 0, 0))
    grid_spec = pltpu.PrefetchScalarGridSpec(
        num_scalar_prefetch=3,
        grid=(nblk,),
        in_specs=[pl.BlockSpec((bm, dp), lambda i, ea, eb, nu: (i, 0)),
                  w_in(0), w_in(0), w_out(0), w_in(1), w_in(1), w_out(1)],
        out_specs=pl.BlockSpec((bm, 2 * d), lambda i, ea, eb, nu: (i, 0)),
    )
    return pl.pallas_call(
        _moe_kernel,
        out_shape=jax.ShapeDtypeStruct((p, 2 * d), F32),
        grid_spec=grid_spec,
        compiler_params=_params(("arbitrary",)),
        name="moe_experts",
    )(blk_a, blk_b, n_used, xs, wg, wu, wd, wg, wu, wd)


def _row_plan(bucket, rank, counts, bm):
    t = bucket.size
    nbk = N_GROUPS * N_PAIRS
    cnt = counts[:, 0].astype(I32)
    padded = (cnt + bm - 1) // bm * bm
    pends = jnp.cumsum(padded)
    pstarts = pends - padded
    onehot = bucket.reshape(t, 1) == jnp.arange(nbk, dtype=I32)[None, :]
    dest = jnp.sum(jnp.where(onehot, pstarts[None, :], 0), axis=1) + rank.reshape(t)
    p = (t + bm - 1) // bm * bm + nbk * bm
    nblk = p // bm
    blk_bucket = jnp.clip(jnp.searchsorted(pends, jnp.arange(nblk, dtype=I32) * bm, side='right'), 0, nbk - 1)
    grp, pair = blk_bucket // N_PAIRS, blk_bucket % N_PAIRS
    blk_a = grp * EXPERTS_PER_GROUP + jnp.asarray(_PAIR_A, I32)[pair]
    blk_b = grp * EXPERTS_PER_GROUP + jnp.asarray(_PAIR_B, I32)[pair]
    n_used = (pends[-1:] // bm).astype(I32)
    return dest.astype(I32), blk_a.astype(I32), blk_b.astype(I32), n_used, p


def _scatter_kernel(dest_ref, src_ref, init_hbm, out_hbm, sem, *, rows):
    del init_hbm
    base = pl.program_id(0) * rows
    for r in range(rows):
        pltpu.make_async_copy(src_ref.at[r], out_hbm.at[dest_ref[base + r]], sem).start()
    pltpu.make_async_copy(src_ref, out_hbm.at[pl.ds(0, rows)], sem).wait()


def _scatter_rows(h_rows, dest, p):
    t, dp = h_rows.shape
    rows = _pick(t, (512, 256, 128))
    grid_spec = pltpu.PrefetchScalarGridSpec(
        num_scalar_prefetch=1,
        grid=(t // rows,),
        in_specs=[pl.BlockSpec((rows, dp), lambda i, dst: (i, 0)), pl.BlockSpec(memory_space=pl.ANY)],
        out_specs=pl.BlockSpec(memory_space=pl.ANY),
        scratch_shapes=[pltpu.SemaphoreType.DMA],
    )
    return pl.pallas_call(
        functools.partial(_scatter_kernel, rows=rows),
        out_shape=jax.ShapeDtypeStruct((p, dp), h_rows.dtype),
        grid_spec=grid_spec,
        input_output_aliases={2: 0},
        compiler_params=_params(("arbitrary",)),
        name="moe_scatter_rows",
    )(dest, h_rows, jnp.zeros((p, dp), h_rows.dtype))


def _combine_kernel(dest_ref, ys_ref, yp_ref, g_ref, wt_ref, m_hbm, os_ref, op_ref, mbuf0, mbuf1, sem,
                    *, ns, th, nsteps):
    step = pl.program_id(0) * pl.num_programs(1) + pl.program_id(1)
    d = ys_ref.shape[1]
    bufs = (mbuf0, mbuf1)

    def gather(first_row, k):
        for r in range(th):
            pltpu.make_async_copy(m_hbm.at[dest_ref[first_row + r]], bufs[k].at[r], sem.at[k]).start()

    def wait(k):
        pltpu.make_async_copy(m_hbm.at[pl.ds(0, th)], bufs[k], sem.at[k]).wait()

    @pl.when(step == 0)
    def _():
        gather(0, 0)

    is_prompt = pl.program_id(0) >= ns
    g = g_ref[...]
    row0 = step * (2 * th)
    for k in range(2):
        rows = slice(k * th, (k + 1) * th)
        wait(k)
        nxt = jnp.minimum(row0 + (k + 1) * th, (nsteps * 2 - 1) * th)
        gather(nxt, 1 - k)
        m = bufs[k][...]
        w = wt_ref[rows, :]
        y = jnp.where(is_prompt, yp_ref[rows, :], ys_ref[rows, :])
        val = y + g * (w[:, 0:1] * m[:, :d] + w[:, 1:2] * m[:, d:])

        @pl.when(jnp.logical_not(is_prompt))
        def _():
            os_ref[rows, :] = val

        @pl.when(is_prompt)
        def _():
            op_ref[rows, :] = val

    @pl.when(step == nsteps - 1)
    def _():
        wait(0)


def _combine(y, mod, layer, moe_rows, dest, wt):
    y_s, y_p = y
    ns, l, d = y_s.shape
    nb = ns + 1
    th = _pick(l // 2, (512, 256, 128, 64))
    tm = 2 * th
    ni = l // tm
    grid_spec = pltpu.PrefetchScalarGridSpec(
        num_scalar_prefetch=1,
        grid=(nb, ni),
        in_specs=_pair_specs(ns, ni, tm, d) + [
            pl.BlockSpec((None, None, None, 1, d), lambda b, i, dst: (layer, 5, b, 0, 0)),
            pl.BlockSpec((None, tm, 128), lambda b, i, dst: (b, i, 0)),
            pl.BlockSpec(memory_space=pl.ANY),
        ],
        out_specs=_pair_specs(ns, ni, tm, d),
        scratch_shapes=[pltpu.VMEM((th, 2 * d), F32), pltpu.VMEM((th, 2 * d), F32),
                        pltpu.SemaphoreType.DMA((2,))],
    )
    outs = pl.pallas_call(
        functools.partial(_combine_kernel, ns=ns, th=th, nsteps=nb * ni),
        out_shape=[jax.ShapeDtypeStruct(y_s.shape, F32), jax.ShapeDtypeStruct(y_p.shape, F32)],
        grid_spec=grid_spec,
        compiler_params=_params(("arbitrary", "arbitrary")),
        name="moe_combine",
    )(dest, y_s, y_p, mod, wt, moe_rows)
    return (outs[0], outs[1])


def kernel(x_prompt, x_sample, c, state_ret_fwd, state_ret_bwd, cache_attn_k, cache_attn_v, c_ctx,
           ada_w, ada_b, norm1_w, norm2_w, ret_w_in, ret_w_out, ret_gn_w, ret_decay_fwd, ret_decay_bwd,
           attn_w_qkv, attn_w_o, attn_q_norm, attn_k_norm, router_w, router_b,
           moe_w_gate, moe_w_up, moe_w_down):
    n_prompt, seq, d = x_prompt.shape
    ns, l, _ = x_sample.shape
    assert n_prompt * seq == l, "prompt tokens must fill exactly one sample-length row"
    depth = ada_w.shape[0]
    nb = ns + 1
    dk = d // RET_HEADS
    dv = 2 * dk
    dh = d // ATT_HEADS
    bm = MOE_ROW_BLOCK

    y = (x_sample, x_prompt.reshape(1, l, d))
    nbp = (nb + 7) // 8 * 8
    cond = jnp.zeros((nbp, d), F32).at[:ns].set(c).at[ns].set(c_ctx)
    mod = _ada_params(cond, ada_w, ada_b)

    perm = (jnp.arange(N_EXPERTS) % N_GROUPS) * EXPERTS_PER_GROUP + jnp.arange(N_EXPERTS) // N_GROUPS
    rw_t = router_w.T[perm]
    rw_hi = rw_t.astype(BF16)
    rw_lo = (rw_t - rw_hi.astype(F32)).astype(BF16)
    rb_perm = router_b[perm]
    wg_bf16, wu_bf16, wd_bf16 = moe_w_gate.astype(BF16), moe_w_up.astype(BF16), moe_w_down.astype(BF16)

    ret_f, ret_b, k_new, v_new = [], [], [], []
    for i in range(depth):
        if i % 2 == 0:
            r = i // 2
            qkvg = _ret_inproj(y, mod, i, norm1_w[i], ret_w_in[r].astype(BF16), ret_gn_w[r], dk)
            o_s = _retention(qkvg, ns, 0, l, dk, ret_decay_fwd[r], ret_decay_bwd[r],
                             s0f=state_ret_fwd[:, r], s0b=state_ret_bwd[:, r])
            qkvg_p = qkvg.reshape(nb * n_prompt, seq, qkvg.shape[2])
            o_p, s_f, s_b = _retention(qkvg_p, n_prompt, ns * n_prompt, seq, dk,
                                       ret_decay_fwd[r], ret_decay_bwd[r], emit_final=True)
            o_s = o_s[0] if isinstance(o_s, (list, tuple)) else o_s
            ret_f.append(s_f)
            ret_b.append(s_b)
            w_o = ret_w_out[r].astype(BF16)
        else:
            a = i // 2
            qkv, k_p, v_p = _qkv_proj(y, mod, i, norm1_w[i], attn_w_qkv[a].astype(BF16),
                                      attn_q_norm[a], attn_k_norm[a], dh)
            past = cache_attn_k.shape[2]
            o_s = _attention(qkv, ns, 0, l, dh,
                             cache_k=cache_attn_k[:, a].reshape(ns, past, ATT_KV_HEADS * dh),
                             cache_v=cache_attn_v[:, a].reshape(ns, past, ATT_KV_HEADS * dh))
            o_p = _attention(qkv.reshape(nb * n_prompt, seq, qkv.shape[2]), n_prompt, ns * n_prompt, seq, dh)
            k_new.append(k_p.reshape(n_prompt, seq, ATT_KV_HEADS, dh))
            v_new.append(v_p.reshape(n_prompt, seq, ATT_KV_HEADS, dh))
            w_o = attn_w_o[a].astype(BF16)
        o_p = o_p.reshape(1, l, o_p.shape[-1])
        y, h2, logits = _outproj(o_s, o_p, w_o, y, mod, i, norm2_w[i], rw_hi, rw_lo)
        bucket, rank, wt, counts = _router(logits, rb_perm)
        dest, blk_a, blk_b, n_used, p = _row_plan(bucket, rank, counts, bm)
        xs = _scatter_rows(h2.reshape(nb * l, h2.shape[2]), dest, p)
        moe_rows = _moe_experts(xs, blk_a, blk_b, n_used, wg_bf16, wu_bf16, wd_bf16, i, bm)
        y = _combine(y, mod, i, moe_rows, dest, wt)

    y_s, y_p = y
    return (y_p.reshape(n_prompt, seq, d), y_s, jnp.stack(ret_f, axis=1), jnp.stack(ret_b, axis=1),
            jnp.stack(k_new, axis=1), jnp.stack(v_new, axis=1))
```

```python
import functools

import jax
import jax.numpy as jnp
from jax import lax
from jax.experimental import pallas as pl
from jax.experimental.pallas import tpu as pltpu

F32 = jnp.float32
BF16 = jnp.bfloat16
I32 = jnp.int32
U32 = jnp.uint32

EPS = 1e-6
GRID_W = 64
RET_HEADS = 4
ATT_HEADS = 8
ATT_KV_HEADS = 2
ROPE_THETA = 10000.0
N_EXPERTS = 32
N_GROUPS = 8
EXPERTS_PER_GROUP = N_EXPERTS // N_GROUPS
TOP_K = 2
_PAIR_A = (0, 0, 0, 1, 1, 3)
_PAIR_B = (1, 2, 3, 3, 2, 2)
N_PAIRS = len(_PAIR_A)
MOE_ROW_BLOCK = 256

V7X_VMEM_BYTES = 64 * 1024 * 1024
VMEM_LIMIT_BYTES = 56 * 1024 * 1024


def _params(sem):
    return pltpu.CompilerParams(dimension_semantics=sem, vmem_limit_bytes=VMEM_LIMIT_BYTES)


def _dot(a, b):
    return jnp.dot(a, b, preferred_element_type=F32)


def _dot_nt(a, b):
    return lax.dot_general(a, b, (((1,), (1,)), ((), ())), preferred_element_type=F32)


def _dot_tn(a, b):
    return lax.dot_general(a, b, (((0,), (0,)), ((), ())), preferred_element_type=F32)


def _pick(n, prefs):
    for p in prefs:
        if n % p == 0:
            return p
    return n


def _ada_kernel(c_ref, w_ref, b_ref, o_ref):
    c = c_ref[...]
    s = jax.nn.silu(c).astype(BF16)
    o_ref[...] = _dot(s, w_ref[...].astype(BF16)) + b_ref[...]


def _ada_params(cond, ada_w, ada_b):
    depth, d, n6 = ada_w.shape
    nbp = cond.shape[0]
    tn = _pick(n6, (1536, 1024, 512, 256, 128))
    out = pl.pallas_call(
        _ada_kernel,
        out_shape=jax.ShapeDtypeStruct((depth, nbp, n6), F32),
        grid=(depth, n6 // tn),
        in_specs=[
            pl.BlockSpec((nbp, d), lambda l, j: (0, 0)),
            pl.BlockSpec((None, d, tn), lambda l, j: (l, 0, j)),
            pl.BlockSpec((None, 1, tn), lambda l, j: (l, 0, j)),
        ],
        out_specs=pl.BlockSpec((None, nbp, tn), lambda l, j: (l, 0, j)),
        compiler_params=_params(("arbitrary", "arbitrary")),
        name="ada_params",
    )(cond, ada_w, ada_b.reshape(depth, 1, n6))
    return out.reshape(depth, nbp, 6, 1, d).transpose(0, 2, 1, 3, 4)


def _mod_spec(d, layer, j):
    return pl.BlockSpec((None, None, None, 1, d), lambda b, i: (layer, j, b, 0, 0))


def _pair_specs(ns, ni, tm, w):
    return [pl.BlockSpec((None, tm, w), lambda b, i, *_: (jnp.minimum(b, ns - 1), jnp.where(b < ns, i, ni - 1), 0)),
            pl.BlockSpec((None, tm, w), lambda b, i, *_: (0, jnp.where(b < ns, 0, i), 0))]


def _pair_load(s_ref, p_ref, ns):
    return jnp.where(pl.program_id(0) >= ns, p_ref[...], s_ref[...])


def _pair_store(s_ref, p_ref, ns, val):
    @pl.when(pl.program_id(0) < ns)
    def _():
        s_ref[...] = val

    @pl.when(pl.program_id(0) >= ns)
    def _():
        p_ref[...] = val


def _norm_mod(x, nw, sh, sc):
    ms = jnp.mean(x * x, axis=-1, keepdims=True)
    h = x * lax.rsqrt(ms + EPS) * nw
    return h * (1.0 + sc) + sh


def _ret_inproj_kernel(xs_ref, xp_ref, nw_ref, sh_ref, sc_ref, w_ref, gn_ref, o_ref, *, ns, hq, hv, tn, k_scale):
    x = _pair_load(xs_ref, xp_ref, ns)
    hb = _norm_mod(x, nw_ref[...], sh_ref[...], sc_ref[...]).astype(BF16)
    n = w_ref.shape[1]
    for j in range(n // tn):
        lo = j * tn
        acc = _dot(hb, w_ref[:, lo:lo + tn])
        if hq <= lo < 2 * hq:
            acc = acc * k_scale
        elif lo >= 2 * hq + hv:
            g0 = (lo - 2 * hq - hv) % hv
            acc = jax.nn.silu(acc) * gn_ref[:, g0:g0 + tn]
        o_ref[:, lo:lo + tn] = acc.astype(BF16)


def _ret_inproj(y, mod, layer, norm_w, w_in_bf16, gn_w, dk):
    y_s, y_p = y
    ns, l, d = y_s.shape
    nb = ns + 1
    n = w_in_bf16.shape[1]
    hq = RET_HEADS * dk
    hv = 2 * hq
    tm = _pick(l, (512, 256, 128))
    tn = _pick(hq, (1024, 512, 256, 128))
    kern = functools.partial(_ret_inproj_kernel, ns=ns, hq=hq, hv=hv, tn=tn, k_scale=float(dk) ** -0.5)
    return pl.pallas_call(
        kern,
        out_shape=jax.ShapeDtypeStruct((nb, l, n), BF16),
        grid=(nb, l // tm),
        in_specs=_pair_specs(ns, l // tm, tm, d) + [
            pl.BlockSpec((1, d), lambda b, i: (0, 0)),
            _mod_spec(d, layer, 0),
            _mod_spec(d, layer, 1),
            pl.BlockSpec((d, n), lambda b, i: (0, 0), pipeline_mode=pl.Buffered(1)),
            pl.BlockSpec((1, hv), lambda b, i: (0, 0)),
        ],
        out_specs=pl.BlockSpec((None, tm, n), lambda b, i: (b, i, 0)),
        compiler_params=_params(("arbitrary", "arbitrary")),
        name="ret_inproj",
    )(y_s, y_p, norm_w.reshape(1, d), mod, mod, w_in_bf16, gn_w.reshape(1, hv))


def _log_sigmoid(x):
    return jnp.minimum(x, 0.0) - jnp.log1p(jnp.exp(-jnp.abs(x)))


def _ret_kernel(*refs, ls, c, has_init, emit_final):
    it = iter(refs)
    q_ref, k_ref, v_ref, gf_ref, gb_ref, df_ref, db_ref = (next(it) for _ in range(7))
    if has_init:
        s0f_ref, s0b_ref = next(it), next(it)
    y_ref = next(it)
    if emit_final:
        sf_out, sb_out = next(it), next(it)
    s_scr, sb_scr = next(it), next(it)

    dk = q_ref.shape[1]
    nchunks = ls // c
    lg_f = _log_sigmoid(df_ref[...])[:, 0:1]
    lg_b = _log_sigmoid(db_ref[...])[:, 0:1]

    ri = lax.broadcasted_iota(I32, (c, c), 0)
    ci = lax.broadcasted_iota(I32, (c, c), 1)
    dif = (ri - ci).astype(F32)
    decay_f = jnp.where(dif >= 0, jnp.exp(jnp.maximum(dif, 0.0) * lg_f), 0.0)
    decay_b = jnp.where(dif <= 0, jnp.exp(jnp.maximum(-dif, 0.0) * lg_b), 0.0)
    pos = lax.broadcasted_iota(I32, (c, dk), 0).astype(F32)
    qdec_f = jnp.exp((pos + 1.0) * lg_f)
    kdec_f = jnp.exp((c - 1.0 - pos) * lg_f)
    qdec_b = jnp.exp((c - pos) * lg_b)
    kdec_b = jnp.exp(pos * lg_b)
    cdec_f = jnp.exp(c * lg_f)
    cdec_b = jnp.exp(c * lg_b)

    if has_init:
        s_scr[...] = s0b_ref[...]
    else:
        s_scr[...] = jnp.zeros_like(s_scr)

    def bwd_body(t, carry):
        n = nchunks - 1 - t
        r0 = pl.multiple_of(n * c, c)
        s = s_scr[...]
        sb_scr[n] = s.astype(BF16)
        kc = k_ref[pl.ds(r0, c), :].astype(F32)
        vc = v_ref[pl.ds(r0, c), :]
        s_scr[...] = s * cdec_b + _dot_tn((kc * kdec_b).astype(BF16), vc)
        return carry

    lax.fori_loop(0, nchunks, bwd_body, 0)
    if emit_final:
        sb_out[...] = s_scr[...]

    if has_init:
        s_scr[...] = s0f_ref[...]
    else:
        s_scr[...] = jnp.zeros_like(s_scr)

    def head_norm(o):
        mu = jnp.mean(o, axis=-1, keepdims=True)
        dlt = o - mu
        var = jnp.mean(dlt * dlt, axis=-1, keepdims=True)
        return dlt * lax.rsqrt(var + EPS)

    def fwd_body(n, carry):
        r0 = pl.multiple_of(n * c, c)
        rows = pl.ds(r0, c)
        qc = q_ref[rows, :]
        kc = k_ref[rows, :]
        vc = v_ref[rows, :]
        qf = qc.astype(F32)
        sc = _dot_nt(qc, kc)
        s = s_scr[...]
        o_f = _dot((sc * decay_f).astype(BF16), vc) + _dot((qf * qdec_f).astype(BF16), s.astype(BF16))
        o_b = _dot((sc * decay_b).astype(BF16), vc) + _dot((qf * qdec_b).astype(BF16), sb_scr[n])
        s_scr[...] = s * cdec_f + _dot_tn((kc.astype(F32) * kdec_f).astype(BF16), vc)
        y = head_norm(o_f) * gf_ref[rows, :].astype(F32) + head_norm(o_b) * gb_ref[rows, :].astype(F32)
        y_ref[rows, :] = y.astype(BF16)
        return carry

    lax.fori_loop(0, nchunks, fwd_body, 0, unroll=2 if nchunks % 2 == 0 else 1)
    if emit_final:
        sf_out[...] = s_scr[...]


def _retention(qkvg, nseq, seq0, ls, dk, decay_f, decay_b, s0f=None, s0b=None, emit_final=False):
    h = RET_HEADS
    dv = 2 * dk
    c = _pick(ls, (256, 128))
    has_init = s0f is not None
    kq, kk, kv, kgf, kgb = 0, h, h, 2 * h, 3 * h
    in_specs = [
        pl.BlockSpec((None, ls, dk), lambda s, hh: (s + seq0, 0, kq + hh)),
        pl.BlockSpec((None, ls, dk), lambda s, hh: (s + seq0, 0, kk + hh)),
        pl.BlockSpec((None, ls, dv), lambda s, hh: (s + seq0, 0, kv + hh)),
        pl.BlockSpec((None, ls, dv), lambda s, hh: (s + seq0, 0, kgf + hh)),
        pl.BlockSpec((None, ls, dv), lambda s, hh: (s + seq0, 0, kgb + hh)),
        pl.BlockSpec((None, 1, 128), lambda s, hh: (hh, 0, 0)),
        pl.BlockSpec((None, 1, 128), lambda s, hh: (hh, 0, 0)),
    ]
    args = [qkvg, qkvg, qkvg, qkvg, qkvg,
            jnp.broadcast_to(decay_f.reshape(h, 1, 1), (h, 1, 128)),
            jnp.broadcast_to(decay_b.reshape(h, 1, 1), (h, 1, 128))]
    if has_init:
        in_specs += [pl.BlockSpec((None, None, dk, dv), lambda s, hh: (s, hh, 0, 0))] * 2
        args += [s0f, s0b]
    out_shape = [jax.ShapeDtypeStruct((nseq, ls, h * dv), BF16)]
    out_specs = [pl.BlockSpec((None, ls, dv), lambda s, hh: (s, 0, hh))]
    if emit_final:
        out_shape += [jax.ShapeDtypeStruct((nseq, h, dk, dv), F32)] * 2
        out_specs += [pl.BlockSpec((None, None, dk, dv), lambda s, hh: (s, hh, 0, 0))] * 2
    kern = functools.partial(_ret_kernel, ls=ls, c=c, has_init=has_init, emit_final=emit_final)
    return pl.pallas_call(
        kern,
        out_shape=out_shape,
        grid=(nseq, h),
        in_specs=in_specs,
        out_specs=out_specs,
        scratch_shapes=[pltpu.VMEM((dk, dv), F32), pltpu.VMEM((ls // c, dk, dv), BF16)],
        compiler_params=_params(("arbitrary", "arbitrary")),
        name="retention_init" if has_init else "retention_zero",
    )(*args)


def _qkv_kernel(xs_ref, xp_ref, nw_ref, sh_ref, sc_ref, w_ref, qn_ref, kn_ref, cos_ref, sa_ref, sb_ref,
                o_ref, kp_ref, vp_ref, *, ns, dh):
    x = _pair_load(xs_ref, xp_ref, ns)
    hb = _norm_mod(x, nw_ref[...], sh_ref[...], sc_ref[...]).astype(BF16)
    nq = ATT_HEADS * dh
    nk = ATT_KV_HEADS * dh
    cos, sa, sb = cos_ref[...], sa_ref[...], sb_ref[...]
    is_prompt = pl.program_id(0) >= ns

    def norm_rope(a, w):
        a = a * lax.rsqrt(jnp.mean(a * a, axis=-1, keepdims=True) + EPS) * w
        roped = a * cos + pltpu.roll(a, dh - dh // 4, 1) * sa + pltpu.roll(a, dh // 4, 1) * sb
        return a, roped

    q = _dot(hb, w_ref[:, :nq])
    for hh in range(ATT_HEADS):
        _, r = norm_rope(q[:, hh * dh:(hh + 1) * dh], qn_ref[...])
        o_ref[:, hh * dh:(hh + 1) * dh] = r.astype(BF16)
    kvv = _dot(hb, w_ref[:, nq:])
    k_norm = []
    for hh in range(ATT_KV_HEADS):
        a, r = norm_rope(kvv[:, hh * dh:(hh + 1) * dh], kn_ref[...])
        o_ref[:, nq + hh * dh:nq + (hh + 1) * dh] = r.astype(BF16)
        k_norm.append(a)
    o_ref[:, nq + nk:] = kvv[:, nk:].astype(BF16)

    @pl.when(is_prompt)
    def _():
        for hh in range(ATT_KV_HEADS):
            kp_ref[:, hh * dh:(hh + 1) * dh] = k_norm[hh]
        vp_ref[...] = kvv[:, nk:]


def _rope_tables(l, dh, rotate):
    axis_dim = dh // 2
    half = axis_dim // 2
    t = jnp.arange(l)
    row = (t // GRID_W).astype(F32)
    col = (t % GRID_W).astype(F32)
    inv = ROPE_THETA ** (-jnp.arange(0, axis_dim, 2, dtype=F32) / axis_dim)
    d = jnp.arange(dh)
    pos = jnp.where((d // axis_dim)[None, :] == 0, row[:, None], col[:, None])
    ang = pos * inv[d % half][None, :]
    first = ((d % axis_dim) < half)[None, :]
    cos, sin = jnp.cos(ang), jnp.sin(ang)
    if not rotate:
        cos, sin = jnp.ones_like(cos), jnp.zeros_like(sin)
    return cos, jnp.where(first, -sin, 0.0), jnp.where(first, 0.0, sin)


def _qkv_proj(y, mod, layer, norm_w, w_bf16, qn, kn, dh):
    y_s, y_p = y
    ns, l, d = y_s.shape
    nb = ns + 1
    n = w_bf16.shape[1]
    nk = ATT_KV_HEADS * dh
    tm = _pick(l, (512, 256, 128))
    tabs = [jnp.stack([a, b]) for a, b in zip(_rope_tables(l, dh, True), _rope_tables(l, dh, False))]
    tab_spec = pl.BlockSpec((None, tm, dh), lambda b, i: (jnp.where(b < ns, 0, 1), i, 0))
    prompt_spec = pl.BlockSpec((tm, nk), lambda b, i: (jnp.where(b < ns, 0, i), 0))
    return pl.pallas_call(
        functools.partial(_qkv_kernel, ns=ns, dh=dh),
        out_shape=[jax.ShapeDtypeStruct((nb, l, n), BF16), jax.ShapeDtypeStruct((l, nk), F32),
                   jax.ShapeDtypeStruct((l, nk), F32)],
        grid=(nb, l // tm),
        in_specs=_pair_specs(ns, l // tm, tm, d) + [
            pl.BlockSpec((1, d), lambda b, i: (0, 0)),
            _mod_spec(d, layer, 0),
            _mod_spec(d, layer, 1),
            pl.BlockSpec((d, n), lambda b, i: (0, 0), pipeline_mode=pl.Buffered(1)),
            pl.BlockSpec((1, dh), lambda b, i: (0, 0)),
            pl.BlockSpec((1, dh), lambda b, i: (0, 0)),
            tab_spec, tab_spec, tab_spec,
        ],
        out_specs=[pl.BlockSpec((None, tm, n), lambda b, i: (b, i, 0)), prompt_spec, prompt_spec],
        compiler_params=_params(("arbitrary", "arbitrary")),
        name="qkv_proj",
    )(y_s, y_p, norm_w.reshape(1, d), mod, mod, w_bf16, qn.reshape(1, dh), kn.reshape(1, dh), *tabs)


def _attn_kernel(*refs, dh, group, has_cache, scale):
    if has_cache:
        q_ref, k_ref, v_ref, kc_ref, vc_ref, o_ref, kall, vext = refs
        past = kc_ref.shape[0]
    else:
        q_ref, k_ref, v_ref, o_ref, kall, vext = refs
        past = 0

    @pl.when(pl.program_id(2) == 0)
    def _():
        if has_cache:
            kall[:past, :] = kc_ref[...].astype(BF16)
            vext[:past, :dh] = vc_ref[...].astype(BF16)
        kall[past:, :] = k_ref[...]
        vext[past:, :dh] = v_ref[...]
        vext[:, dh:] = jnp.ones((vext.shape[0], dh), BF16)

    c = scale * 1.4426950408889634
    tq = q_ref.shape[0]
    q = jnp.concatenate([q_ref[:, g * dh:(g + 1) * dh] for g in range(group)], axis=0)
    lk = kall.shape[0]
    ck = _pick(lk, (512, 256, 128))
    m = None
    for j in range(lk // ck):
        s = _dot_nt(q, kall[j * ck:(j + 1) * ck, :])
        mj = jnp.max(s, axis=-1, keepdims=True)
        vj = vext[j * ck:(j + 1) * ck, :]
        if m is None:
            m = mj
            o = _dot(jnp.exp2((s - m) * c).astype(BF16), vj)
        else:
            m_new = jnp.maximum(m, mj)
            o = o * jnp.exp2((m - m_new) * c) + _dot(jnp.exp2((s - m_new) * c).astype(BF16), vj)
            m = m_new
    for g in range(group):
        og = o[g * tq:(g + 1) * tq]
        o_ref[:, g * dh:(g + 1) * dh] = (og[:, :dh] / og[:, dh:dh + 1]).astype(BF16)


def _attention(qkv, nseq, seq0, ls, dh, cache_k=None, cache_v=None):
    group = ATT_HEADS // ATT_KV_HEADS
    has_cache = cache_k is not None
    tq = _pick(ls, (256, 128))
    kcol = ATT_HEADS
    vcol = ATT_HEADS + ATT_KV_HEADS
    in_specs = [
        pl.BlockSpec((None, tq, group * dh), lambda s, kh, i: (s + seq0, i, kh)),
        pl.BlockSpec((None, ls, dh), lambda s, kh, i: (s + seq0, 0, kcol + kh)),
        pl.BlockSpec((None, ls, dh), lambda s, kh, i: (s + seq0, 0, vcol + kh)),
    ]
    args = [qkv, qkv, qkv]
    past = 0
    if has_cache:
        past = cache_k.shape[1]
        in_specs += [pl.BlockSpec((None, past, dh), lambda s, kh, i: (s, 0, kh))] * 2
        args += [cache_k, cache_v]
    kern = functools.partial(_attn_kernel, dh=dh, group=group, has_cache=has_cache, scale=float(dh) ** -0.5)
    return pl.pallas_call(
        kern,
        out_shape=jax.ShapeDtypeStruct((nseq, ls, ATT_HEADS * dh), BF16),
        grid=(nseq, ATT_KV_HEADS, ls // tq),
        in_specs=in_specs,
        out_specs=pl.BlockSpec((None, tq, group * dh), lambda s, kh, i: (s, i, kh)),
        scratch_shapes=[pltpu.VMEM((past + ls, dh), BF16), pltpu.VMEM((past + ls, 2 * dh), BF16)],
        compiler_params=_params(("arbitrary", "arbitrary", "arbitrary")),
        name="attention_cache" if has_cache else "attention_self",
    )(*args)


def _pack_rows(h):
    half = h.shape[1] // 2
    u = lax.bitcast_convert_type(h.astype(BF16).astype(F32), U32)
    return (u[:, :half] >> 16) | (u[:, half:] & jnp.uint32(0xFFFF0000))


def _unpack_rows(p):
    lo = lax.bitcast_convert_type(p << 16, F32)
    hi = lax.bitcast_convert_type(p & jnp.uint32(0xFFFF0000), F32)
    return jnp.concatenate([lo, hi], axis=1).astype(BF16)


def _outproj_kernel(os_ref, op_ref, w_ref, ys_ref, yp_ref, g1_ref, nw_ref, sh_ref, sc_ref, rwh_ref, rwl_ref,
                    yos_ref, yop_ref, h_ref, lg_ref, *, ns):
    o = _pair_load(os_ref, op_ref, ns)
    y = _pair_load(ys_ref, yp_ref, ns) + g1_ref[...] * _dot(o, w_ref[...])
    _pair_store(yos_ref, yop_ref, ns, y)
    h = _norm_mod(y, nw_ref[...], sh_ref[...], sc_ref[...])
    h_ref[...] = _pack_rows(h)
    h_hi = h.astype(BF16)
    h_lo = (h - h_hi.astype(F32)).astype(BF16)
    rwh = rwh_ref[...]
    lg_ref[...] = _dot_nt(rwh, h_hi) + _dot_nt(rwl_ref[...], h_hi) + _dot_nt(rwh, h_lo)


def _outproj(o_s, o_p, w_bf16, y, mod, layer, norm_w, rw_hi, rw_lo):
    y_s, y_p = y
    ns, l, d = y_s.shape
    nb = ns + 1
    kdim = o_s.shape[2]
    ne = rw_hi.shape[0]
    tm = _pick(l, (512, 256, 128))
    ni = l // tm
    dp = jax.eval_shape(_pack_rows, jax.ShapeDtypeStruct((8, d), F32)).shape[1]
    outs = pl.pallas_call(
        functools.partial(_outproj_kernel, ns=ns),
        out_shape=[jax.ShapeDtypeStruct(y_s.shape, F32), jax.ShapeDtypeStruct(y_p.shape, F32),
                   jax.ShapeDtypeStruct((nb, l, dp), U32), jax.ShapeDtypeStruct((nb, ne, l), F32)],
        grid=(nb, ni),
        in_specs=_pair_specs(ns, ni, tm, kdim) + [
            pl.BlockSpec((kdim, d), lambda b, i: (0, 0), pipeline_mode=pl.Buffered(1)),
        ] + _pair_specs(ns, ni, tm, d) + [
            _mod_spec(d, layer, 2),
            pl.BlockSpec((1, d), lambda b, i: (0, 0)),
            _mod_spec(d, layer, 3),
            _mod_spec(d, layer, 4),
            pl.BlockSpec((ne, d), lambda b, i: (0, 0)),
            pl.BlockSpec((ne, d), lambda b, i: (0, 0)),
        ],
        out_specs=_pair_specs(ns, ni, tm, d) + [
            pl.BlockSpec((None, tm, dp), lambda b, i: (b, i, 0)),
            pl.BlockSpec((None, ne, tm), lambda b, i: (b, 0, i))],
        compiler_params=_params(("arbitrary", "arbitrary")),
        name="outproj_router",
    )(o_s, o_p, w_bf16, y_s, y_p, mod, norm_w.reshape(1, d), mod, mod, rw_hi, rw_lo)
    return (outs[0], outs[1]), outs[2], outs[3]


def _router_kernel(lg_ref, b_ref, bkt_ref, rank_ref, wt_ref, cnt_ref, carry_ref):
    ng = N_GROUPS
    sc = jax.nn.sigmoid(lg_ref[...])
    sel = sc + b_ref[...]
    sj = [sel[j * ng:(j + 1) * ng] for j in range(EXPERTS_PER_GROUP)]
    pj = [sc[j * ng:(j + 1) * ng] for j in range(EXPERTS_PER_GROUP)]
    hi1, lo1 = jnp.maximum(sj[0], sj[1]), jnp.minimum(sj[0], sj[1])
    hi2, lo2 = jnp.maximum(sj[2], sj[3]), jnp.minimum(sj[2], sj[3])
    top1 = jnp.maximum(hi1, hi2)
    top2 = jnp.maximum(jnp.minimum(hi1, hi2), jnp.maximum(lo1, lo2))
    gs = top1 + top2
    gmax = jnp.max(gs, axis=0, keepdims=True)
    gi = lax.broadcasted_iota(I32, gs.shape, 0)
    best = jnp.min(jnp.where(gs == gmax, gi, ng), axis=0, keepdims=True)
    onehot = gi == best
    neg = -jnp.inf
    vj = [jnp.max(jnp.where(onehot, a, neg), axis=0, keepdims=True) for a in sj]
    wj = [jnp.max(jnp.where(onehot, a, neg), axis=0, keepdims=True) for a in pj]

    def argmax4(vals):
        m, i = vals[0], jnp.zeros(vals[0].shape, I32)
        for j in range(1, EXPERTS_PER_GROUP):
            gt = vals[j] > m
            m = jnp.where(gt, vals[j], m)
            i = jnp.where(gt, j, i)
        return i

    i1 = argmax4(vj)
    i2 = argmax4([jnp.where(i1 == j, neg, vj[j]) for j in range(EXPERTS_PER_GROUP)])

    def take(vals, i):
        out = vals[0]
        for j in range(1, EXPERTS_PER_GROUP):
            out = jnp.where(i == j, vals[j], out)
        return out

    w1, w2 = take(wj, i1), take(wj, i2)
    tot = w1 + w2
    w1, w2 = w1 / tot, w2 / tot

    lo, hi = jnp.minimum(i1, i2), jnp.maximum(i1, i2)
    pair = jnp.where(lo == 0, hi - 1, jnp.where(lo == 1, jnp.where(hi == 3, 3, 4), 5))
    slot_a = jnp.where(pair < 3, 0, jnp.where(pair < 5, 1, 3))
    bucket = best * N_PAIRS + pair
    a_first = i1 == slot_a
    w_a = jnp.where(a_first, w1, w2)
    w_b = jnp.where(a_first, w2, w1)

    tl = bucket.shape[1]
    nbk = carry_ref.shape[0]

    @pl.when(jnp.logical_and(pl.program_id(0) == 0, pl.program_id(1) == 0))
    def _():
        carry_ref[...] = jnp.zeros_like(carry_ref)

    onehot_b = lax.broadcasted_iota(I32, (nbk, tl), 0) == bucket
    upper = (lax.broadcasted_iota(I32, (tl, tl), 0) <= lax.broadcasted_iota(I32, (tl, tl), 1))
    prefix = _dot(onehot_b.astype(BF16), upper.astype(BF16))
    carry = carry_ref[...]
    rank = jnp.sum(jnp.where(onehot_b, prefix - 1.0 + carry, 0.0), axis=0, keepdims=True)
    carry = carry + prefix[:, tl - 1:tl]
    carry_ref[...] = carry
    bkt_ref[...] = bucket
    rank_ref[...] = rank.astype(I32)
    cnt_ref[...] = jnp.broadcast_to(carry, cnt_ref.shape)
    rows = lax.broadcasted_iota(I32, (wt_ref.shape[1], tl), 0)
    wt_ref[...] = jnp.where(rows == 0, w_a, jnp.where(rows == 1, w_b, 0.0)).T


def _router(logits, bias_perm):
    nb, ne, l = logits.shape
    tl = _pick(l, (512, 256, 128))
    nbk = N_GROUPS * N_PAIRS
    return pl.pallas_call(
        _router_kernel,
        out_shape=[jax.ShapeDtypeStruct((nb, 1, l), I32), jax.ShapeDtypeStruct((nb, 1, l), I32),
                   jax.ShapeDtypeStruct((nb, l, 128), F32), jax.ShapeDtypeStruct((nbk, 128), F32)],
        grid=(nb, l // tl),
        in_specs=[pl.BlockSpec((None, ne, tl), lambda b, i: (b, 0, i)),
                  pl.BlockSpec((ne, 1), lambda b, i: (0, 0))],
        out_specs=[pl.BlockSpec((None, 1, tl), lambda b, i: (b, 0, i)),
                   pl.BlockSpec((None, 1, tl), lambda b, i: (b, 0, i)),
                   pl.BlockSpec((None, tl, 128), lambda b, i: (b, i, 0)),
                   pl.BlockSpec((nbk, 128), lambda b, i: (0, 0))],
        scratch_shapes=[pltpu.VMEM((nbk, 1), F32)],
        compiler_params=_params(("arbitrary", "arbitrary")),
        name="router_rank",
    )(logits, bias_perm.reshape(ne, 1))


def _moe_kernel(ea_ref, eb_ref, nu_ref, x_ref, wga_ref, wua_ref, wda_ref, wgb_ref, wub_ref, wdb_ref, o_ref):
    del ea_ref, eb_ref
    i = pl.program_id(0)
    d = o_ref.shape[1] // 2

    @pl.when(i < nu_ref[0])
    def _():
        x = _unpack_rows(x_ref[...])

        def expert(wg, wu, wd):
            hid = jax.nn.silu(_dot(x, wg[...])) * _dot(x, wu[...])
            return _dot(hid.astype(BF16), wd[...])

        o_ref[:, :d] = expert(wga_ref, wua_ref, wda_ref)
        o_ref[:, d:] = expert(wgb_ref, wub_ref, wdb_ref)

    @pl.when(i >= nu_ref[0])
    def _():
        o_ref[...] = jnp.zeros_like(o_ref)


def _moe_experts(xs, blk_a, blk_b, n_used, wg, wu, wd, layer, bm):
    p, dp = xs.shape
    _, _, d, de = wg.shape
    nblk = p // bm
    w_in = lambda sel: pl.BlockSpec((None, None, d, de), lambda i, ea, eb, nu: (layer, (ea, eb)[sel][i], 0, 0))
    w_out = lambda sel: pl.BlockSpec((None, None, de, d), lambda i, ea, eb, nu: (layer, (ea, eb)[sel][i], 0, 0))
    grid_spec = pltpu.PrefetchScalarGridSpec(
        num_scalar_prefetch=3,
        grid=(nblk,),
        in_specs=[pl.BlockSpec((bm, dp), lambda i, ea, eb, nu: (i, 0)),
                  w_in(0), w_in(0), w_out(0), w_in(1), w_in(1), w_out(1)],
        out_specs=pl.BlockSpec((bm, 2 * d), lambda i, ea, eb, nu: (i, 0)),
    )
    return pl.pallas_call(
        _moe_kernel,
        out_shape=jax.ShapeDtypeStruct((p, 2 * d), F32),
        grid_spec=grid_spec,
        compiler_params=_params(("arbitrary",)),
        name="moe_experts",
    )(blk_a, blk_b, n_used, xs, wg, wu, wd, wg, wu, wd)


def _row_plan(bucket, rank, counts, bm):
    t = bucket.size
    nbk = N_GROUPS * N_PAIRS
    cnt = counts[:, 0].astype(I32)
    padded = (cnt + bm - 1) // bm * bm
    pends = jnp.cumsum(padded)
    pstarts = pends - padded
    onehot = bucket.reshape(t, 1) == jnp.arange(nbk, dtype=I32)[None, :]
    dest = jnp.sum(jnp.where(onehot, pstarts[None, :], 0), axis=1) + rank.reshape(t)
    p = (t + bm - 1) // bm * bm + nbk * bm
    nblk = p // bm
    blk_bucket = jnp.clip(jnp.searchsorted(pends, jnp.arange(nblk, dtype=I32) * bm, side='right'), 0, nbk - 1)
    grp, pair = blk_bucket // N_PAIRS, blk_bucket % N_PAIRS
    blk_a = grp * EXPERTS_PER_GROUP + jnp.asarray(_PAIR_A, I32)[pair]
    blk_b = grp * EXPERTS_PER_GROUP + jnp.asarray(_PAIR_B, I32)[pair]
    n_used = (pends[-1:] // bm).astype(I32)
    return dest.astype(I32), blk_a.astype(I32), blk_b.astype(I32), n_used, p


def _scatter_kernel(dest_ref, src_ref, init_hbm, out_hbm, sem, *, rows):
    del init_hbm
    base = pl.program_id(0) * rows
    for r in range(rows):
        pltpu.make_async_copy(src_ref.at[r], out_hbm.at[dest_ref[base + r]], sem).start(priority=r % 2)
    pltpu.make_async_copy(src_ref, out_hbm.at[pl.ds(0, rows)], sem).wait()


def _scatter_rows(h_rows, dest, p):
    t, dp = h_rows.shape
    rows = _pick(t, (512, 256, 128))
    grid_spec = pltpu.PrefetchScalarGridSpec(
        num_scalar_prefetch=1,
        grid=(t // rows,),
        in_specs=[pl.BlockSpec((rows, dp), lambda i, dst: (i, 0)), pl.BlockSpec(memory_space=pl.ANY)],
        out_specs=pl.BlockSpec(memory_space=pl.ANY),
        scratch_shapes=[pltpu.SemaphoreType.DMA],
    )
    return pl.pallas_call(
        functools.partial(_scatter_kernel, rows=rows),
        out_shape=jax.ShapeDtypeStruct((p, dp), h_rows.dtype),
        grid_spec=grid_spec,
        input_output_aliases={2: 0},
        compiler_params=_params(("arbitrary",)),
        name="moe_scatter_rows",
    )(dest, h_rows, jnp.zeros((p, dp), h_rows.dtype))


def _combine_kernel(dest_ref, ys_ref, yp_ref, g_ref, wt_ref, m_hbm, os_ref, op_ref, mbuf0, mbuf1, sem,
                    *, ns, th, nsteps):
    step = pl.program_id(0) * pl.num_programs(1) + pl.program_id(1)
    d = ys_ref.shape[1]
    bufs = (mbuf0, mbuf1)

    def gather(first_row, k):
        for r in range(th):
            pltpu.make_async_copy(m_hbm.at[dest_ref[first_row + r]], bufs[k].at[r],
                                  sem.at[k]).start(priority=r % 2)

    def wait(k):
        pltpu.make_async_copy(m_hbm.at[pl.ds(0, th)], bufs[k], sem.at[k]).wait()

    @pl.when(step == 0)
    def _():
        gather(0, 0)

    is_prompt = pl.program_id(0) >= ns
    g = g_ref[...]
    row0 = step * (2 * th)
    for k in range(2):
        rows = slice(k * th, (k + 1) * th)
        wait(k)
        nxt = jnp.minimum(row0 + (k + 1) * th, (nsteps * 2 - 1) * th)
        gather(nxt, 1 - k)
        m = bufs[k][...]
        w = wt_ref[rows, :]
        y = jnp.where(is_prompt, yp_ref[rows, :], ys_ref[rows, :])
        val = y + g * (w[:, 0:1] * m[:, :d] + w[:, 1:2] * m[:, d:])

        @pl.when(jnp.logical_not(is_prompt))
        def _():
            os_ref[rows, :] = val

        @pl.when(is_prompt)
        def _():
            op_ref[rows, :] = val

    @pl.when(step == nsteps - 1)
    def _():
        wait(0)


def _combine(y, mod, layer, moe_rows, dest, wt):
    y_s, y_p = y
    ns, l, d = y_s.shape
    nb = ns + 1
    th = _pick(l // 2, (512, 256, 128, 64))
    tm = 2 * th
    ni = l // tm
    grid_spec = pltpu.PrefetchScalarGridSpec(
        num_scalar_prefetch=1,
        grid=(nb, ni),
        in_specs=_pair_specs(ns, ni, tm, d) + [
            pl.BlockSpec((None, None, None, 1, d), lambda b, i, dst: (layer, 5, b, 0, 0)),
            pl.BlockSpec((None, tm, 128), lambda b, i, dst: (b, i, 0)),
            pl.BlockSpec(memory_space=pl.ANY),
        ],
        out_specs=_pair_specs(ns, ni, tm, d),
        scratch_shapes=[pltpu.VMEM((th, 2 * d), F32), pltpu.VMEM((th, 2 * d), F32),
                        pltpu.SemaphoreType.DMA((2,))],
    )
    outs = pl.pallas_call(
        functools.partial(_combine_kernel, ns=ns, th=th, nsteps=nb * ni),
        out_shape=[jax.ShapeDtypeStruct(y_s.shape, F32), jax.ShapeDtypeStruct(y_p.shape, F32)],
        grid_spec=grid_spec,
        compiler_params=_params(("arbitrary", "arbitrary")),
        name="moe_combine",
    )(dest, y_s, y_p, mod, wt, moe_rows)
    return (outs[0], outs[1])


def kernel(x_prompt, x_sample, c, state_ret_fwd, state_ret_bwd, cache_attn_k, cache_attn_v, c_ctx,
           ada_w, ada_b, norm1_w, norm2_w, ret_w_in, ret_w_out, ret_gn_w, ret_decay_fwd, ret_decay_bwd,
           attn_w_qkv, attn_w_o, attn_q_norm, attn_k_norm, router_w, router_b,
           moe_w_gate, moe_w_up, moe_w_down):
    n_prompt, seq, d = x_prompt.shape
    ns, l, _ = x_sample.shape
    assert n_prompt * seq == l, "prompt tokens must fill exactly one sample-length row"
    depth = ada_w.shape[0]
    nb = ns + 1
    dk = d // RET_HEADS
    dv = 2 * dk
    dh = d // ATT_HEADS
    bm = MOE_ROW_BLOCK

    y = (x_sample, x_prompt.reshape(1, l, d))
    nbp = (nb + 7) // 8 * 8
    cond = jnp.zeros((nbp, d), F32).at[:ns].set(c).at[ns].set(c_ctx)
    mod = _ada_params(cond, ada_w, ada_b)

    perm = (jnp.arange(N_EXPERTS) % N_GROUPS) * EXPERTS_PER_GROUP + jnp.arange(N_EXPERTS) // N_GROUPS
    rw_t = router_w.T[perm]
    rw_hi = rw_t.astype(BF16)
    rw_lo = (rw_t - rw_hi.astype(F32)).astype(BF16)
    rb_perm = router_b[perm]
    wg_bf16, wu_bf16, wd_bf16 = moe_w_gate.astype(BF16), moe_w_up.astype(BF16), moe_w_down.astype(BF16)

    ret_f, ret_b, k_new, v_new = [], [], [], []
    for i in range(depth):
        if i % 2 == 0:
            r = i // 2
            qkvg = _ret_inproj(y, mod, i, norm1_w[i], ret_w_in[r].astype(BF16), ret_gn_w[r], dk)
            o_s = _retention(qkvg, ns, 0, l, dk, ret_decay_fwd[r], ret_decay_bwd[r],
                             s0f=state_ret_fwd[:, r], s0b=state_ret_bwd[:, r])
            qkvg_p = qkvg.reshape(nb * n_prompt, seq, qkvg.shape[2])
            o_p, s_f, s_b = _retention(qkvg_p, n_prompt, ns * n_prompt, seq, dk,
                                       ret_decay_fwd[r], ret_decay_bwd[r], emit_final=True)
            o_s = o_s[0] if isinstance(o_s, (list, tuple)) else o_s
            ret_f.append(s_f)
            ret_b.append(s_b)
            w_o = ret_w_out[r].astype(BF16)
        else:
            a = i // 2
            qkv, k_p, v_p = _qkv_proj(y, mod, i, norm1_w[i], attn_w_qkv[a].astype(BF16),
                                      attn_q_norm[a], attn_k_norm[a], dh)
            past = cache_attn_k.shape[2]
            o_s = _attention(qkv, ns, 0, l, dh,
                             cache_k=cache_attn_k[:, a].reshape(ns, past, ATT_KV_HEADS * dh),
                             cache_v=cache_attn_v[:, a].reshape(ns, past, ATT_KV_HEADS * dh))
            o_p = _attention(qkv.reshape(nb * n_prompt, seq, qkv.shape[2]), n_prompt, ns * n_prompt, seq, dh)
            k_new.append(k_p.reshape(n_prompt, seq, ATT_KV_HEADS, dh))
            v_new.append(v_p.reshape(n_prompt, seq, ATT_KV_HEADS, dh))
            w_o = attn_w_o[a].astype(BF16)
        o_p = o_p.reshape(1, l, o_p.shape[-1])
        y, h2, logits = _outproj(o_s, o_p, w_o, y, mod, i, norm2_w[i], rw_hi, rw_lo)
        bucket, rank, wt, counts = _router(logits, rb_perm)
        dest, blk_a, blk_b, n_used, p = _row_plan(bucket, rank, counts, bm)
        xs = _scatter_rows(h2.reshape(nb * l, h2.shape[2]), dest, p)
        moe_rows = _moe_experts(xs, blk_a, blk_b, n_used, wg_bf16, wu_bf16, wd_bf16, i, bm)
        y = _combine(y, mod, i, moe_rows, dest, wt)

    y_s, y_p = y
    return (y_p.reshape(n_prompt, seq, d), y_s, jnp.stack(ret_f, axis=1), jnp.stack(ret_b, axis=1),
            jnp.stack(k_new, axis=1), jnp.stack(v_new, axis=1))
```

```python
import functools

import jax
import jax.numpy as jnp
from jax import lax
from jax.experimental import pallas as pl
from jax.experimental.pallas import tpu as pltpu

F32 = jnp.float32
BF16 = jnp.bfloat16
I32 = jnp.int32
U32 = jnp.uint32

EPS = 1e-6
GRID_W = 64
RET_HEADS = 4
ATT_HEADS = 8
ATT_KV_HEADS = 2
ROPE_THETA = 10000.0
N_EXPERTS = 32
N_GROUPS = 8
EXPERTS_PER_GROUP = N_EXPERTS // N_GROUPS
TOP_K = 2
_PAIR_A = (0, 0, 0, 1, 1, 3)
_PAIR_B = (1, 2, 3, 3, 2, 2)
N_PAIRS = len(_PAIR_A)
MOE_ROW_BLOCK = 256

V7X_VMEM_BYTES = 64 * 1024 * 1024
VMEM_LIMIT_BYTES = 56 * 1024 * 1024


def _params(sem):
    return pltpu.CompilerParams(dimension_semantics=sem, vmem_limit_bytes=VMEM_LIMIT_BYTES)


def _dot(a, b):
    return jnp.dot(a, b, preferred_element_type=F32)


def _dot_nt(a, b):
    return lax.dot_general(a, b, (((1,), (1,)), ((), ())), preferred_element_type=F32)


def _dot_tn(a, b):
    return lax.dot_general(a, b, (((0,), (0,)), ((), ())), preferred_element_type=F32)


def _pick(n, prefs):
    for p in prefs:
        if n % p == 0:
            return p
    return n


def _ada_kernel(c_ref, w_ref, b_ref, o_ref):
    c = c_ref[...]
    s = jax.nn.silu(c).astype(BF16)
    o_ref[...] = _dot(s, w_ref[...].astype(BF16)) + b_ref[...]


def _ada_params(cond, ada_w, ada_b):
    depth, d, n6 = ada_w.shape
    nbp = cond.shape[0]
    tn = _pick(n6, (1536, 1024, 512, 256, 128))
    out = pl.pallas_call(
        _ada_kernel,
        out_shape=jax.ShapeDtypeStruct((depth, nbp, n6), F32),
        grid=(depth, n6 // tn),
        in_specs=[
            pl.BlockSpec((nbp, d), lambda l, j: (0, 0)),
            pl.BlockSpec((None, d, tn), lambda l, j: (l, 0, j)),
            pl.BlockSpec((None, 1, tn), lambda l, j: (l, 0, j)),
        ],
        out_specs=pl.BlockSpec((None, nbp, tn), lambda l, j: (l, 0, j)),
        compiler_params=_params(("arbitrary", "arbitrary")),
        name="ada_params",
    )(cond, ada_w, ada_b.reshape(depth, 1, n6))
    return out.reshape(depth, nbp, 6, 1, d).transpose(0, 2, 1, 3, 4)


def _mod_spec(d, layer, j):
    return pl.BlockSpec((None, None, None, 1, d), lambda b, i: (layer, j, b, 0, 0))


def _pair_specs(ns, ni, tm, w):
    return [pl.BlockSpec((None, tm, w), lambda b, i, *_: (jnp.minimum(b, ns - 1), jnp.where(b < ns, i, ni - 1), 0)),
            pl.BlockSpec((None, tm, w), lambda b, i, *_: (0, jnp.where(b < ns, 0, i), 0))]


def _pair_load(s_ref, p_ref, ns):
    return jnp.where(pl.program_id(0) >= ns, p_ref[...], s_ref[...])


def _pair_store(s_ref, p_ref, ns, val):
    @pl.when(pl.program_id(0) < ns)
    def _():
        s_ref[...] = val

    @pl.when(pl.program_id(0) >= ns)
    def _():
        p_ref[...] = val


def _norm_mod(x, nw, sh, sc):
    ms = jnp.mean(x * x, axis=-1, keepdims=True)
    h = x * lax.rsqrt(ms + EPS) * nw
    return h * (1.0 + sc) + sh


def _ret_inproj_kernel(xs_ref, xp_ref, nw_ref, sh_ref, sc_ref, w_ref, gn_ref, o_ref, *, ns, hq, hv, tn, k_scale):
    x = _pair_load(xs_ref, xp_ref, ns)
    hb = _norm_mod(x, nw_ref[...], sh_ref[...], sc_ref[...]).astype(BF16)
    n = w_ref.shape[1]
    for j in range(n // tn):
        lo = j * tn
        acc = _dot(hb, w_ref[:, lo:lo + tn])
        if hq <= lo < 2 * hq:
            acc = acc * k_scale
        elif lo >= 2 * hq + hv:
            g0 = (lo - 2 * hq - hv) % hv
            acc = jax.nn.silu(acc) * gn_ref[:, g0:g0 + tn]
        o_ref[:, lo:lo + tn] = acc.astype(BF16)


def _ret_inproj(y, mod, layer, norm_w, w_in_bf16, gn_w, dk):
    y_s, y_p = y
    ns, l, d = y_s.shape
    nb = ns + 1
    n = w_in_bf16.shape[1]
    hq = RET_HEADS * dk
    hv = 2 * hq
    tm = _pick(l, (512, 256, 128))
    tn = _pick(hq, (1024, 512, 256, 128))
    kern = functools.partial(_ret_inproj_kernel, ns=ns, hq=hq, hv=hv, tn=tn, k_scale=float(dk) ** -0.5)
    return pl.pallas_call(
        kern,
        out_shape=jax.ShapeDtypeStruct((nb, l, n), BF16),
        grid=(nb, l // tm),
        in_specs=_pair_specs(ns, l // tm, tm, d) + [
            pl.BlockSpec((1, d), lambda b, i: (0, 0)),
            _mod_spec(d, layer, 0),
            _mod_spec(d, layer, 1),
            pl.BlockSpec((d, n), lambda b, i: (0, 0), pipeline_mode=pl.Buffered(1)),
            pl.BlockSpec((1, hv), lambda b, i: (0, 0)),
        ],
        out_specs=pl.BlockSpec((None, tm, n), lambda b, i: (b, i, 0)),
        compiler_params=_params(("arbitrary", "arbitrary")),
        name="ret_inproj",
    )(y_s, y_p, norm_w.reshape(1, d), mod, mod, w_in_bf16, gn_w.reshape(1, hv))


def _log_sigmoid(x):
    return jnp.minimum(x, 0.0) - jnp.log1p(jnp.exp(-jnp.abs(x)))


def _ret_kernel(*refs, ls, c, has_init, emit_final):
    it = iter(refs)
    q_ref, k_ref, v_ref, gf_ref, gb_ref, df_ref, db_ref = (next(it) for _ in range(7))
    if has_init:
        s0f_ref, s0b_ref = next(it), next(it)
    y_ref = next(it)
    if emit_final:
        sf_out, sb_out = next(it), next(it)
    s_scr, sb_scr = next(it), next(it)

    dk = q_ref.shape[1]
    nchunks = ls // c
    lg_f = _log_sigmoid(df_ref[...])[:, 0:1]
    lg_b = _log_sigmoid(db_ref[...])[:, 0:1]

    ri = lax.broadcasted_iota(I32, (c, c), 0)
    ci = lax.broadcasted_iota(I32, (c, c), 1)
    dif = (ri - ci).astype(F32)
    decay_f = jnp.where(dif >= 0, jnp.exp(jnp.maximum(dif, 0.0) * lg_f), 0.0).astype(BF16)
    decay_b = jnp.where(dif <= 0, jnp.exp(jnp.maximum(-dif, 0.0) * lg_b), 0.0).astype(BF16)
    pos = lax.broadcasted_iota(I32, (c, dk), 0).astype(F32)
    qdec_f = jnp.exp((pos + 1.0) * lg_f).astype(BF16)
    kdec_f = jnp.exp((c - 1.0 - pos) * lg_f).astype(BF16)
    qdec_b = jnp.exp((c - pos) * lg_b).astype(BF16)
    kdec_b = jnp.exp(pos * lg_b).astype(BF16)
    cdec_f = jnp.exp(c * lg_f)
    cdec_b = jnp.exp(c * lg_b)

    if has_init:
        s_scr[...] = s0b_ref[...]
    else:
        s_scr[...] = jnp.zeros_like(s_scr)

    def bwd_body(t, carry):
        n = nchunks - 1 - t
        r0 = pl.multiple_of(n * c, c)
        s = s_scr[...]
        sb_scr[n] = s.astype(BF16)
        kc = k_ref[pl.ds(r0, c), :]
        vc = v_ref[pl.ds(r0, c), :]
        s_scr[...] = s * cdec_b + _dot_tn(kc * kdec_b, vc)
        return carry

    lax.fori_loop(0, nchunks, bwd_body, 0)
    if emit_final:
        sb_out[...] = s_scr[...]

    if has_init:
        s_scr[...] = s0f_ref[...]
    else:
        s_scr[...] = jnp.zeros_like(s_scr)

    def head_norm(o):
        mu = jnp.mean(o, axis=-1, keepdims=True)
        dlt = o - mu
        var = jnp.mean(dlt * dlt, axis=-1, keepdims=True)
        return dlt * lax.rsqrt(var + EPS)

    def fwd_body(n, carry):
        r0 = pl.multiple_of(n * c, c)
        rows = pl.ds(r0, c)
        qc = q_ref[rows, :]
        kc = k_ref[rows, :]
        vc = v_ref[rows, :]
        sc = _dot_nt(qc, kc).astype(BF16)
        s = s_scr[...]
        o_f = _dot(sc * decay_f, vc) + _dot(qc * qdec_f, s.astype(BF16))
        o_b = _dot(sc * decay_b, vc) + _dot(qc * qdec_b, sb_scr[n])
        s_scr[...] = s * cdec_f + _dot_tn(kc * kdec_f, vc)
        y_ref[rows, :] = (head_norm(o_f).astype(BF16) * gf_ref[rows, :]
                          + head_norm(o_b).astype(BF16) * gb_ref[rows, :])
        return carry

    lax.fori_loop(0, nchunks, fwd_body, 0, unroll=2 if nchunks % 2 == 0 else 1)
    if emit_final:
        sf_out[...] = s_scr[...]


def _retention(qkvg, nseq, seq0, ls, dk, decay_f, decay_b, s0f=None, s0b=None, emit_final=False):
    h = RET_HEADS
    dv = 2 * dk
    c = _pick(ls, (256, 128))
    has_init = s0f is not None
    kq, kk, kv, kgf, kgb = 0, h, h, 2 * h, 3 * h
    in_specs = [
        pl.BlockSpec((None, ls, dk), lambda s, hh: (s + seq0, 0, kq + hh)),
        pl.BlockSpec((None, ls, dk), lambda s, hh: (s + seq0, 0, kk + hh)),
        pl.BlockSpec((None, ls, dv), lambda s, hh: (s + seq0, 0, kv + hh)),
        pl.BlockSpec((None, ls, dv), lambda s, hh: (s + seq0, 0, kgf + hh)),
        pl.BlockSpec((None, ls, dv), lambda s, hh: (s + seq0, 0, kgb + hh)),
        pl.BlockSpec((None, 1, 128), lambda s, hh: (hh, 0, 0)),
        pl.BlockSpec((None, 1, 128), lambda s, hh: (hh, 0, 0)),
    ]
    args = [qkvg, qkvg, qkvg, qkvg, qkvg,
            jnp.broadcast_to(decay_f.reshape(h, 1, 1), (h, 1, 128)),
            jnp.broadcast_to(decay_b.reshape(h, 1, 1), (h, 1, 128))]
    if has_init:
        in_specs += [pl.BlockSpec((None, None, dk, dv), lambda s, hh: (s, hh, 0, 0))] * 2
        args += [s0f, s0b]
    out_shape = [jax.ShapeDtypeStruct((nseq, ls, h * dv), BF16)]
    out_specs = [pl.BlockSpec((None, ls, dv), lambda s, hh: (s, 0, hh))]
    if emit_final:
        out_shape += [jax.ShapeDtypeStruct((nseq, h, dk, dv), F32)] * 2
        out_specs += [pl.BlockSpec((None, None, dk, dv), lambda s, hh: (s, hh, 0, 0))] * 2
    kern = functools.partial(_ret_kernel, ls=ls, c=c, has_init=has_init, emit_final=emit_final)
    return pl.pallas_call(
        kern,
        out_shape=out_shape,
        grid=(nseq, h),
        in_specs=in_specs,
        out_specs=out_specs,
        scratch_shapes=[pltpu.VMEM((dk, dv), F32), pltpu.VMEM((ls // c, dk, dv), BF16)],
        compiler_params=_params(("arbitrary", "arbitrary")),
        name="retention_init" if has_init else "retention_zero",
    )(*args)


def _qkv_kernel(xs_ref, xp_ref, nw_ref, sh_ref, sc_ref, w_ref, qn_ref, kn_ref, cos_ref, sa_ref, sb_ref,
                o_ref, kp_ref, vp_ref, *, ns, dh):
    x = _pair_load(xs_ref, xp_ref, ns)
    hb = _norm_mod(x, nw_ref[...], sh_ref[...], sc_ref[...]).astype(BF16)
    nq = ATT_HEADS * dh
    nk = ATT_KV_HEADS * dh
    cos, sa, sb = cos_ref[...], sa_ref[...], sb_ref[...]
    is_prompt = pl.program_id(0) >= ns

    def norm_rope(a, w):
        a = a * lax.rsqrt(jnp.mean(a * a, axis=-1, keepdims=True) + EPS) * w
        roped = a * cos + pltpu.roll(a, dh - dh // 4, 1) * sa + pltpu.roll(a, dh // 4, 1) * sb
        return a, roped

    q = _dot(hb, w_ref[:, :nq])
    for hh in range(ATT_HEADS):
        _, r = norm_rope(q[:, hh * dh:(hh + 1) * dh], qn_ref[...])
        o_ref[:, hh * dh:(hh + 1) * dh] = r.astype(BF16)
    kvv = _dot(hb, w_ref[:, nq:])
    k_norm = []
    for hh in range(ATT_KV_HEADS):
        a, r = norm_rope(kvv[:, hh * dh:(hh + 1) * dh], kn_ref[...])
        o_ref[:, nq + hh * dh:nq + (hh + 1) * dh] = r.astype(BF16)
        k_norm.append(a)
    o_ref[:, nq + nk:] = kvv[:, nk:].astype(BF16)

    @pl.when(is_prompt)
    def _():
        for hh in range(ATT_KV_HEADS):
            kp_ref[:, hh * dh:(hh + 1) * dh] = k_norm[hh]
        vp_ref[...] = kvv[:, nk:]


def _rope_tables(l, dh, rotate):
    axis_dim = dh // 2
    half = axis_dim // 2
    t = jnp.arange(l)
    row = (t // GRID_W).astype(F32)
    col = (t % GRID_W).astype(F32)
    inv = ROPE_THETA ** (-jnp.arange(0, axis_dim, 2, dtype=F32) / axis_dim)
    d = jnp.arange(dh)
    pos = jnp.where((d // axis_dim)[None, :] == 0, row[:, None], col[:, None])
    ang = pos * inv[d % half][None, :]
    first = ((d % axis_dim) < half)[None, :]
    cos, sin = jnp.cos(ang), jnp.sin(ang)
    if not rotate:
        cos, sin = jnp.ones_like(cos), jnp.zeros_like(sin)
    return cos, jnp.where(first, -sin, 0.0), jnp.where(first, 0.0, sin)


def _qkv_proj(y, mod, layer, norm_w, w_bf16, qn, kn, dh):
    y_s, y_p = y
    ns, l, d = y_s.shape
    nb = ns + 1
    n = w_bf16.shape[1]
    nk = ATT_KV_HEADS * dh
    tm = _pick(l, (512, 256, 128))
    tabs = [jnp.stack([a, b]) for a, b in zip(_rope_tables(l, dh, True), _rope_tables(l, dh, False))]
    tab_spec = pl.BlockSpec((None, tm, dh), lambda b, i: (jnp.where(b < ns, 0, 1), i, 0))
    prompt_spec = pl.BlockSpec((tm, nk), lambda b, i: (jnp.where(b < ns, 0, i), 0))
    return pl.pallas_call(
        functools.partial(_qkv_kernel, ns=ns, dh=dh),
        out_shape=[jax.ShapeDtypeStruct((nb, l, n), BF16), jax.ShapeDtypeStruct((l, nk), F32),
                   jax.ShapeDtypeStruct((l, nk), F32)],
        grid=(nb, l // tm),
        in_specs=_pair_specs(ns, l // tm, tm, d) + [
            pl.BlockSpec((1, d), lambda b, i: (0, 0)),
            _mod_spec(d, layer, 0),
            _mod_spec(d, layer, 1),
            pl.BlockSpec((d, n), lambda b, i: (0, 0), pipeline_mode=pl.Buffered(1)),
            pl.BlockSpec((1, dh), lambda b, i: (0, 0)),
            pl.BlockSpec((1, dh), lambda b, i: (0, 0)),
            tab_spec, tab_spec, tab_spec,
        ],
        out_specs=[pl.BlockSpec((None, tm, n), lambda b, i: (b, i, 0)), prompt_spec, prompt_spec],
        compiler_params=_params(("arbitrary", "arbitrary")),
        name="qkv_proj",
    )(y_s, y_p, norm_w.reshape(1, d), mod, mod, w_bf16, qn.reshape(1, dh), kn.reshape(1, dh), *tabs)


def _attn_kernel(*refs, dh, group, has_cache, scale):
    if has_cache:
        q_ref, k_ref, v_ref, kc_ref, vc_ref, o_ref, kall, vext = refs
        past = kc_ref.shape[0]
    else:
        q_ref, k_ref, v_ref, o_ref, kall, vext = refs
        past = 0

    @pl.when(pl.program_id(2) == 0)
    def _():
        if has_cache:
            kall[:past, :] = kc_ref[...].astype(BF16)
            vext[:past, :dh] = vc_ref[...].astype(BF16)
        kall[past:, :] = k_ref[...]
        vext[past:, :dh] = v_ref[...]
        vext[:, dh:] = jnp.ones((vext.shape[0], dh), BF16)

    c = scale * 1.4426950408889634
    tq = q_ref.shape[0]
    q = jnp.concatenate([q_ref[:, g * dh:(g + 1) * dh] for g in range(group)], axis=0)
    lk = kall.shape[0]
    ck = _pick(lk, (1152, 512, 256, 128))
    m = None
    for j in range(lk // ck):
        s = _dot_nt(q, kall[j * ck:(j + 1) * ck, :])
        mj = jnp.max(s, axis=-1, keepdims=True)
        vj = vext[j * ck:(j + 1) * ck, :]
        if m is None:
            m = mj
            o = _dot(jnp.exp2((s - m) * c).astype(BF16), vj)
        else:
            m_new = jnp.maximum(m, mj)
            o = o * jnp.exp2((m - m_new) * c) + _dot(jnp.exp2((s - m_new) * c).astype(BF16), vj)
            m = m_new
    for g in range(group):
        og = o[g * tq:(g + 1) * tq]
        o_ref[:, g * dh:(g + 1) * dh] = (og[:, :dh] / og[:, dh:dh + 1]).astype(BF16)


def _attention(qkv, nseq, seq0, ls, dh, cache_k=None, cache_v=None):
    group = ATT_HEADS // ATT_KV_HEADS
    has_cache = cache_k is not None
    tq = _pick(ls, (256, 128))
    kcol = ATT_HEADS
    vcol = ATT_HEADS + ATT_KV_HEADS
    in_specs = [
        pl.BlockSpec((None, tq, group * dh), lambda s, kh, i: (s + seq0, i, kh)),
        pl.BlockSpec((None, ls, dh), lambda s, kh, i: (s + seq0, 0, kcol + kh)),
        pl.BlockSpec((None, ls, dh), lambda s, kh, i: (s + seq0, 0, vcol + kh)),
    ]
    args = [qkv, qkv, qkv]
    past = 0
    if has_cache:
        past = cache_k.shape[1]
        in_specs += [pl.BlockSpec((None, past, dh), lambda s, kh, i: (s, 0, kh))] * 2
        args += [cache_k, cache_v]
    kern = functools.partial(_attn_kernel, dh=dh, group=group, has_cache=has_cache, scale=float(dh) ** -0.5)
    return pl.pallas_call(
        kern,
        out_shape=jax.ShapeDtypeStruct((nseq, ls, ATT_HEADS * dh), BF16),
        grid=(nseq, ATT_KV_HEADS, ls // tq),
        in_specs=in_specs,
        out_specs=pl.BlockSpec((None, tq, group * dh), lambda s, kh, i: (s, i, kh)),
        scratch_shapes=[pltpu.VMEM((past + ls, dh), BF16), pltpu.VMEM((past + ls, 2 * dh), BF16)],
        compiler_params=_params(("arbitrary", "arbitrary", "arbitrary")),
        name="attention_cache" if has_cache else "attention_self",
    )(*args)


def _pack_rows(h):
    half = h.shape[1] // 2
    u = lax.bitcast_convert_type(h.astype(BF16).astype(F32), U32)
    return (u[:, :half] >> 16) | (u[:, half:] & jnp.uint32(0xFFFF0000))


def _unpack_rows(p):
    lo = lax.bitcast_convert_type(p << 16, F32)
    hi = lax.bitcast_convert_type(p & jnp.uint32(0xFFFF0000), F32)
    return jnp.concatenate([lo, hi], axis=1).astype(BF16)


def _pack_pair(a, b):
    ua = lax.bitcast_convert_type(a.astype(BF16).astype(F32), U32)
    ub = lax.bitcast_convert_type(b.astype(BF16).astype(F32), U32)
    return (ua >> 16) | (ub & jnp.uint32(0xFFFF0000))


def _unpack_pair(p):
    return (lax.bitcast_convert_type(p << 16, F32),
            lax.bitcast_convert_type(p & jnp.uint32(0xFFFF0000), F32))


def _outproj_kernel(os_ref, op_ref, w_ref, ys_ref, yp_ref, g1_ref, nw_ref, sh_ref, sc_ref, rwh_ref, rwl_ref,
                    yos_ref, yop_ref, h_ref, lg_ref, *, ns):
    o = _pair_load(os_ref, op_ref, ns)
    y = _pair_load(ys_ref, yp_ref, ns) + g1_ref[...] * _dot(o, w_ref[...])
    _pair_store(yos_ref, yop_ref, ns, y)
    h = _norm_mod(y, nw_ref[...], sh_ref[...], sc_ref[...])
    h_ref[...] = _pack_rows(h)
    h_hi = h.astype(BF16)
    h_lo = (h - h_hi.astype(F32)).astype(BF16)
    rwh = rwh_ref[...]
    lg_ref[...] = _dot_nt(rwh, h_hi) + _dot_nt(rwl_ref[...], h_hi) + _dot_nt(rwh, h_lo)


def _outproj(o_s, o_p, w_bf16, y, mod, layer, norm_w, rw_hi, rw_lo):
    y_s, y_p = y
    ns, l, d = y_s.shape
    nb = ns + 1
    kdim = o_s.shape[2]
    ne = rw_hi.shape[0]
    tm = _pick(l, (512, 256, 128))
    ni = l // tm
    dp = jax.eval_shape(_pack_rows, jax.ShapeDtypeStruct((8, d), F32)).shape[1]
    outs = pl.pallas_call(
        functools.partial(_outproj_kernel, ns=ns),
        out_shape=[jax.ShapeDtypeStruct(y_s.shape, F32), jax.ShapeDtypeStruct(y_p.shape, F32),
                   jax.ShapeDtypeStruct((nb, l, dp), U32), jax.ShapeDtypeStruct((nb, ne, l), F32)],
        grid=(nb, ni),
        in_specs=_pair_specs(ns, ni, tm, kdim) + [
            pl.BlockSpec((kdim, d), lambda b, i: (0, 0), pipeline_mode=pl.Buffered(1)),
        ] + _pair_specs(ns, ni, tm, d) + [
            _mod_spec(d, layer, 2),
            pl.BlockSpec((1, d), lambda b, i: (0, 0)),
            _mod_spec(d, layer, 3),
            _mod_spec(d, layer, 4),
            pl.BlockSpec((ne, d), lambda b, i: (0, 0)),
            pl.BlockSpec((ne, d), lambda b, i: (0, 0)),
        ],
        out_specs=_pair_specs(ns, ni, tm, d) + [
            pl.BlockSpec((None, tm, dp), lambda b, i: (b, i, 0)),
            pl.BlockSpec((None, ne, tm), lambda b, i: (b, 0, i))],
        compiler_params=_params(("arbitrary", "arbitrary")),
        name="outproj_router",
    )(o_s, o_p, w_bf16, y_s, y_p, mod, norm_w.reshape(1, d), mod, mod, rw_hi, rw_lo)
    return (outs[0], outs[1]), outs[2], outs[3]


def _router_kernel(lg_ref, b_ref, bkt_ref, rank_ref, wt_ref, cnt_ref, carry_ref):
    ng = N_GROUPS
    sc = jax.nn.sigmoid(lg_ref[...])
    sel = sc + b_ref[...]
    sj = [sel[j * ng:(j + 1) * ng] for j in range(EXPERTS_PER_GROUP)]
    pj = [sc[j * ng:(j + 1) * ng] for j in range(EXPERTS_PER_GROUP)]
    hi1, lo1 = jnp.maximum(sj[0], sj[1]), jnp.minimum(sj[0], sj[1])
    hi2, lo2 = jnp.maximum(sj[2], sj[3]), jnp.minimum(sj[2], sj[3])
    top1 = jnp.maximum(hi1, hi2)
    top2 = jnp.maximum(jnp.minimum(hi1, hi2), jnp.maximum(lo1, lo2))
    gs = top1 + top2
    gmax = jnp.max(gs, axis=0, keepdims=True)
    gi = lax.broadcasted_iota(I32, gs.shape, 0)
    best = jnp.min(jnp.where(gs == gmax, gi, ng), axis=0, keepdims=True)
    onehot = gi == best
    neg = -jnp.inf
    vj = [jnp.max(jnp.where(onehot, a, neg), axis=0, keepdims=True) for a in sj]
    wj = [jnp.max(jnp.where(onehot, a, neg), axis=0, keepdims=True) for a in pj]

    def argmax4(vals):
        m, i = vals[0], jnp.zeros(vals[0].shape, I32)
        for j in range(1, EXPERTS_PER_GROUP):
            gt = vals[j] > m
            m = jnp.where(gt, vals[j], m)
            i = jnp.where(gt, j, i)
        return i

    i1 = argmax4(vj)
    i2 = argmax4([jnp.where(i1 == j, neg, vj[j]) for j in range(EXPERTS_PER_GROUP)])

    def take(vals, i):
        out = vals[0]
        for j in range(1, EXPERTS_PER_GROUP):
            out = jnp.where(i == j, vals[j], out)
        return out

    w1, w2 = take(wj, i1), take(wj, i2)
    tot = w1 + w2
    w1, w2 = w1 / tot, w2 / tot

    lo, hi = jnp.minimum(i1, i2), jnp.maximum(i1, i2)
    pair = jnp.where(lo == 0, hi - 1, jnp.where(lo == 1, jnp.where(hi == 3, 3, 4), 5))
    slot_a = jnp.where(pair < 3, 0, jnp.where(pair < 5, 1, 3))
    bucket = best * N_PAIRS + pair
    a_first = i1 == slot_a
    w_a = jnp.where(a_first, w1, w2)
    w_b = jnp.where(a_first, w2, w1)

    tl = bucket.shape[1]
    nbk = carry_ref.shape[0]

    @pl.when(jnp.logical_and(pl.program_id(0) == 0, pl.program_id(1) == 0))
    def _():
        carry_ref[...] = jnp.zeros_like(carry_ref)

    onehot_b = lax.broadcasted_iota(I32, (nbk, tl), 0) == bucket
    upper = (lax.broadcasted_iota(I32, (tl, tl), 0) <= lax.broadcasted_iota(I32, (tl, tl), 1))
    prefix = _dot(onehot_b.astype(BF16), upper.astype(BF16))
    carry = carry_ref[...]
    rank = jnp.sum(jnp.where(onehot_b, prefix - 1.0 + carry, 0.0), axis=0, keepdims=True)
    carry = carry + prefix[:, tl - 1:tl]
    carry_ref[...] = carry
    bkt_ref[...] = bucket
    rank_ref[...] = rank.astype(I32)
    cnt_ref[...] = jnp.broadcast_to(carry, cnt_ref.shape)
    rows = lax.broadcasted_iota(I32, (wt_ref.shape[1], tl), 0)
    wt_ref[...] = jnp.where(rows == 0, w_a, jnp.where(rows == 1, w_b, 0.0)).T


def _router(logits, bias_perm):
    nb, ne, l = logits.shape
    tl = _pick(l, (512, 256, 128))
    nbk = N_GROUPS * N_PAIRS
    return pl.pallas_call(
        _router_kernel,
        out_shape=[jax.ShapeDtypeStruct((nb, 1, l), I32), jax.ShapeDtypeStruct((nb, 1, l), I32),
                   jax.ShapeDtypeStruct((nb, l, 128), F32), jax.ShapeDtypeStruct((nbk, 128), F32)],
        grid=(nb, l // tl),
        in_specs=[pl.BlockSpec((None, ne, tl), lambda b, i: (b, 0, i)),
                  pl.BlockSpec((ne, 1), lambda b, i: (0, 0))],
        out_specs=[pl.BlockSpec((None, 1, tl), lambda b, i: (b, 0, i)),
                   pl.BlockSpec((None, 1, tl), lambda b, i: (b, 0, i)),
                   pl.BlockSpec((None, tl, 128), lambda b, i: (b, i, 0)),
                   pl.BlockSpec((nbk, 128), lambda b, i: (0, 0))],
        scratch_shapes=[pltpu.VMEM((nbk, 1), F32)],
        compiler_params=_params(("arbitrary", "arbitrary")),
        name="router_rank",
    )(logits, bias_perm.reshape(ne, 1))


def _moe_kernel(ea_ref, eb_ref, nu_ref, x_ref, wga_ref, wua_ref, wda_ref, wgb_ref, wub_ref, wdb_ref, o_ref):
    del ea_ref, eb_ref
    i = pl.program_id(0)

    @pl.when(i < nu_ref[0])
    def _():
        x = _unpack_rows(x_ref[...])

        def expert(wg, wu, wd):
            hid = jax.nn.silu(_dot(x, wg[...])) * _dot(x, wu[...])
            return _dot(hid.astype(BF16), wd[...])

        o_ref[...] = _pack_pair(expert(wga_ref, wua_ref, wda_ref), expert(wgb_ref, wub_ref, wdb_ref))

    @pl.when(i >= nu_ref[0])
    def _():
        o_ref[...] = jnp.zeros_like(o_ref)


def _moe_experts(xs, blk_a, blk_b, n_used, wg, wu, wd, layer, bm):
    p, dp = xs.shape
    _, _, d, de = wg.shape
    nblk = p // bm
    row = jax.ShapeDtypeStruct((8, d), F32)
    dq = jax.eval_shape(_pack_pair, row, row).shape[1]
    w_in = lambda sel: pl.BlockSpec((None, None, d, de), lambda i, ea, eb, nu: (layer, (ea, eb)[sel][i], 0, 0))
    w_out = lambda sel: pl.BlockSpec((None, None, de, d), lambda i, ea, eb, nu: (layer, (ea, eb)[sel][i], 0, 0))
    grid_spec = pltpu.PrefetchScalarGridSpec(
        num_scalar_prefetch=3,
        grid=(nblk,),
        in_specs=[pl.BlockSpec((bm, dp), lambda i, ea, eb, nu: (i, 0)),
                  w_in(0), w_in(0), w_out(0), w_in(1), w_in(1), w_out(1)],
        out_specs=pl.BlockSpec((bm, dq), lambda i, ea, eb, nu: (i, 0)),
    )
    return pl.pallas_call(
        _moe_kernel,
        out_shape=jax.ShapeDtypeStruct((p, dq), U32),
        grid_spec=grid_spec,
        compiler_params=_params(("arbitrary",)),
        name="moe_experts",
    )(blk_a, blk_b, n_used, xs, wg, wu, wd, wg, wu, wd)


def _row_plan(bucket, rank, counts, bm):
    t = bucket.size
    nbk = N_GROUPS * N_PAIRS
    cnt = counts[:, 0].astype(I32)
    padded = (cnt + bm - 1) // bm * bm
    pends = jnp.cumsum(padded)
    pstarts = pends - padded
    onehot = bucket.reshape(t, 1) == jnp.arange(nbk, dtype=I32)[None, :]
    dest = jnp.sum(jnp.where(onehot, pstarts[None, :], 0), axis=1) + rank.reshape(t)
    p = (t + bm - 1) // bm * bm + nbk * bm
    nblk = p // bm
    blk_bucket = jnp.clip(jnp.searchsorted(pends, jnp.arange(nblk, dtype=I32) * bm, side='right'), 0, nbk - 1)
    grp, pair = blk_bucket // N_PAIRS, blk_bucket % N_PAIRS
    blk_a = grp * EXPERTS_PER_GROUP + jnp.asarray(_PAIR_A, I32)[pair]
    blk_b = grp * EXPERTS_PER_GROUP + jnp.asarray(_PAIR_B, I32)[pair]
    n_used = (pends[-1:] // bm).astype(I32)
    return dest.astype(I32), blk_a.astype(I32), blk_b.astype(I32), n_used, p


def _scatter_kernel(dest_ref, src_ref, init_hbm, out_hbm, sem, *, rows):
    del init_hbm
    base = pl.program_id(0) * rows
    for r in range(rows):
        pltpu.make_async_copy(src_ref.at[r], out_hbm.at[dest_ref[base + r]], sem).start(priority=r % 2)
    pltpu.make_async_copy(src_ref, out_hbm.at[pl.ds(0, rows)], sem).wait()


def _scatter_rows(h_rows, dest, p):
    t, dp = h_rows.shape
    rows = _pick(t, (512, 256, 128))
    grid_spec = pltpu.PrefetchScalarGridSpec(
        num_scalar_prefetch=1,
        grid=(t // rows,),
        in_specs=[pl.BlockSpec((rows, dp), lambda i, dst: (i, 0)), pl.BlockSpec(memory_space=pl.ANY)],
        out_specs=pl.BlockSpec(memory_space=pl.ANY),
        scratch_shapes=[pltpu.SemaphoreType.DMA],
    )
    return pl.pallas_call(
        functools.partial(_scatter_kernel, rows=rows),
        out_shape=jax.ShapeDtypeStruct((p, dp), h_rows.dtype),
        grid_spec=grid_spec,
        input_output_aliases={2: 0},
        compiler_params=_params(("arbitrary",)),
        name="moe_scatter_rows",
    )(dest, h_rows, jnp.zeros((p, dp), h_rows.dtype))


def _combine_kernel(dest_ref, ys_ref, yp_ref, g_ref, wt_ref, m_hbm, os_ref, op_ref, mbuf0, mbuf1, sem,
                    *, ns, th, nsteps):
    step = pl.program_id(0) * pl.num_programs(1) + pl.program_id(1)
    d = ys_ref.shape[1]
    bufs = (mbuf0, mbuf1)

    def gather(first_row, k):
        for r in range(th):
            pltpu.make_async_copy(m_hbm.at[dest_ref[first_row + r]], bufs[k].at[r],
                                  sem.at[k]).start(priority=r % 2)

    def wait(k):
        pltpu.make_async_copy(m_hbm.at[pl.ds(0, th)], bufs[k], sem.at[k]).wait()

    @pl.when(step == 0)
    def _():
        gather(0, 0)

    is_prompt = pl.program_id(0) >= ns
    g = g_ref[...]
    row0 = step * (2 * th)
    for k in range(2):
        rows = slice(k * th, (k + 1) * th)
        wait(k)
        nxt = jnp.minimum(row0 + (k + 1) * th, (nsteps * 2 - 1) * th)
        gather(nxt, 1 - k)
        m_a, m_b = _unpack_pair(bufs[k][...])
        w = wt_ref[rows, :]
        y = jnp.where(is_prompt, yp_ref[rows, :], ys_ref[rows, :])
        val = y + g * (w[:, 0:1] * m_a + w[:, 1:2] * m_b)

        @pl.when(jnp.logical_not(is_prompt))
        def _():
            os_ref[rows, :] = val

        @pl.when(is_prompt)
        def _():
            op_ref[rows, :] = val

    @pl.when(step == nsteps - 1)
    def _():
        wait(0)


def _combine(y, mod, layer, moe_rows, dest, wt):
    y_s, y_p = y
    ns, l, d = y_s.shape
    nb = ns + 1
    th = _pick(l // 2, (512, 256, 128, 64))
    tm = 2 * th
    ni = l // tm
    grid_spec = pltpu.PrefetchScalarGridSpec(
        num_scalar_prefetch=1,
        grid=(nb, ni),
        in_specs=_pair_specs(ns, ni, tm, d) + [
            pl.BlockSpec((None, None, None, 1, d), lambda b, i, dst: (layer, 5, b, 0, 0)),
            pl.BlockSpec((None, tm, 128), lambda b, i, dst: (b, i, 0)),
            pl.BlockSpec(memory_space=pl.ANY),
        ],
        out_specs=_pair_specs(ns, ni, tm, d),
        scratch_shapes=[pltpu.VMEM((th, moe_rows.shape[1]), moe_rows.dtype),
                        pltpu.VMEM((th, moe_rows.shape[1]), moe_rows.dtype),
                        pltpu.SemaphoreType.DMA((2,))],
    )
    outs = pl.pallas_call(
        functools.partial(_combine_kernel, ns=ns, th=th, nsteps=nb * ni),
        out_shape=[jax.ShapeDtypeStruct(y_s.shape, F32), jax.ShapeDtypeStruct(y_p.shape, F32)],
        grid_spec=grid_spec,
        compiler_params=_params(("arbitrary", "arbitrary")),
        name="moe_combine",
    )(dest, y_s, y_p, mod, wt, moe_rows)
    return (outs[0], outs[1])


def kernel(x_prompt, x_sample, c, state_ret_fwd, state_ret_bwd, cache_attn_k, cache_attn_v, c_ctx,
           ada_w, ada_b, norm1_w, norm2_w, ret_w_in, ret_w_out, ret_gn_w, ret_decay_fwd, ret_decay_bwd,
           attn_w_qkv, attn_w_o, attn_q_norm, attn_k_norm, router_w, router_b,
           moe_w_gate, moe_w_up, moe_w_down):
    n_prompt, seq, d = x_prompt.shape
    ns, l, _ = x_sample.shape
    assert n_prompt * seq == l, "prompt tokens must fill exactly one sample-length row"
    depth = ada_w.shape[0]
    nb = ns + 1
    dk = d // RET_HEADS
    dv = 2 * dk
    dh = d // ATT_HEADS
    bm = MOE_ROW_BLOCK

    y = (x_sample, x_prompt.reshape(1, l, d))
    nbp = (nb + 7) // 8 * 8
    cond = jnp.zeros((nbp, d), F32).at[:ns].set(c).at[ns].set(c_ctx)
    mod = _ada_params(cond, ada_w, ada_b)

    perm = (jnp.arange(N_EXPERTS) % N_GROUPS) * EXPERTS_PER_GROUP + jnp.arange(N_EXPERTS) // N_GROUPS
    rw_t = router_w.T[perm]
    rw_hi = rw_t.astype(BF16)
    rw_lo = (rw_t - rw_hi.astype(F32)).astype(BF16)
    rb_perm = router_b[perm]
    wg_bf16, wu_bf16, wd_bf16 = moe_w_gate.astype(BF16), moe_w_up.astype(BF16), moe_w_down.astype(BF16)

    ret_f, ret_b, k_new, v_new = [], [], [], []
    for i in range(depth):
        if i % 2 == 0:
            r = i // 2
            qkvg = _ret_inproj(y, mod, i, norm1_w[i], ret_w_in[r].astype(BF16), ret_gn_w[r], dk)
            o_s = _retention(qkvg, ns, 0, l, dk, ret_decay_fwd[r], ret_decay_bwd[r],
                             s0f=state_ret_fwd[:, r], s0b=state_ret_bwd[:, r])
            qkvg_p = qkvg.reshape(nb * n_prompt, seq, qkvg.shape[2])
            o_p, s_f, s_b = _retention(qkvg_p, n_prompt, ns * n_prompt, seq, dk,
                                       ret_decay_fwd[r], ret_decay_bwd[r], emit_final=True)
            o_s = o_s[0] if isinstance(o_s, (list, tuple)) else o_s
            ret_f.append(s_f)
            ret_b.append(s_b)
            w_o = ret_w_out[r].astype(BF16)
        else:
            a = i // 2
            qkv, k_p, v_p = _qkv_proj(y, mod, i, norm1_w[i], attn_w_qkv[a].astype(BF16),
                                      attn_q_norm[a], attn_k_norm[a], dh)
            past = cache_attn_k.shape[2]
            o_s = _attention(qkv, ns, 0, l, dh,
                             cache_k=cache_attn_k[:, a].reshape(ns, past, ATT_KV_HEADS * dh),
                             cache_v=cache_attn_v[:, a].reshape(ns, past, ATT_KV_HEADS * dh))
            o_p = _attention(qkv.reshape(nb * n_prompt, seq, qkv.shape[2]), n_prompt, ns * n_prompt, seq, dh)
            k_new.append(k_p.reshape(n_prompt, seq, ATT_KV_HEADS, dh))
            v_new.append(v_p.reshape(n_prompt, seq, ATT_KV_HEADS, dh))
            w_o = attn_w_o[a].astype(BF16)
        o_p = o_p.reshape(1, l, o_p.shape[-1])
        y, h2, logits = _outproj(o_s, o_p, w_o, y, mod, i, norm2_w[i], rw_hi, rw_lo)
        bucket, rank, wt, counts = _router(logits, rb_perm)
        dest, blk_a, blk_b, n_used, p = _row_plan(bucket, rank, counts, bm)
        xs = _scatter_rows(h2.reshape(nb * l, h2.shape[2]), dest, p)
        moe_rows = _moe_experts(xs, blk_a, blk_b, n_used, wg_bf16, wu_bf16, wd_bf16, i, bm)
        y = _combine(y, mod, i, moe_rows, dest, wt)

    y_s, y_p = y
    return (y_p.reshape(n_prompt, seq, d), y_s, jnp.stack(ret_f, axis=1), jnp.stack(ret_b, axis=1),
            jnp.stack(k_new, axis=1), jnp.stack(v_new, axis=1))
```

```python
import functools

import jax
import jax.numpy as jnp
from jax import lax
from jax.experimental import pallas as pl
from jax.experimental.pallas import tpu as pltpu

F32 = jnp.float32
BF16 = jnp.bfloat16
I32 = jnp.int32
U32 = jnp.uint32

EPS = 1e-6
GRID_W = 64
RET_HEADS = 4
ATT_HEADS = 8
ATT_KV_HEADS = 2
ROPE_THETA = 10000.0
N_EXPERTS = 32
N_GROUPS = 8
EXPERTS_PER_GROUP = N_EXPERTS // N_GROUPS
TOP_K = 2
_PAIR_A = (0, 0, 0, 1, 1, 3)
_PAIR_B = (1, 2, 3, 3, 2, 2)
N_PAIRS = len(_PAIR_A)
MOE_ROW_BLOCK = 256

V7X_VMEM_BYTES = 64 * 1024 * 1024
VMEM_LIMIT_BYTES = 56 * 1024 * 1024


def _params(sem):
    return pltpu.CompilerParams(dimension_semantics=sem, vmem_limit_bytes=VMEM_LIMIT_BYTES)


def _dot(a, b):
    return jnp.dot(a, b, preferred_element_type=F32)


def _dot_nt(a, b):
    return lax.dot_general(a, b, (((1,), (1,)), ((), ())), preferred_element_type=F32)


def _dot_tn(a, b):
    return lax.dot_general(a, b, (((0,), (0,)), ((), ())), preferred_element_type=F32)


def _pick(n, prefs):
    for p in prefs:
        if n % p == 0:
            return p
    return n


def _ada_kernel(c_ref, w_ref, b_ref, o_ref):
    c = c_ref[...]
    s = jax.nn.silu(c).astype(BF16)
    o_ref[...] = _dot(s, w_ref[...].astype(BF16)) + b_ref[...]


def _ada_params(cond, ada_w, ada_b):
    depth, d, n6 = ada_w.shape
    nbp = cond.shape[0]
    tn = _pick(n6, (1536, 1024, 512, 256, 128))
    out = pl.pallas_call(
        _ada_kernel,
        out_shape=jax.ShapeDtypeStruct((depth, nbp, n6), F32),
        grid=(depth, n6 // tn),
        in_specs=[
            pl.BlockSpec((nbp, d), lambda l, j: (0, 0)),
            pl.BlockSpec((None, d, tn), lambda l, j: (l, 0, j)),
            pl.BlockSpec((None, 1, tn), lambda l, j: (l, 0, j)),
        ],
        out_specs=pl.BlockSpec((None, nbp, tn), lambda l, j: (l, 0, j)),
        compiler_params=_params(("arbitrary", "arbitrary")),
        name="ada_params",
    )(cond, ada_w, ada_b.reshape(depth, 1, n6))
    return out.reshape(depth, nbp, 6, 1, d).transpose(0, 2, 1, 3, 4)


def _mod_spec(d, layer, j):
    return pl.BlockSpec((None, None, None, 1, d), lambda b, i: (layer, j, b, 0, 0))


def _pair_specs(ns, ni, tm, w):
    return [pl.BlockSpec((None, tm, w), lambda b, i, *_: (jnp.minimum(b, ns - 1), jnp.where(b < ns, i, ni - 1), 0)),
            pl.BlockSpec((None, tm, w), lambda b, i, *_: (0, jnp.where(b < ns, 0, i), 0))]


def _pair_load(s_ref, p_ref, ns):
    return jnp.where(pl.program_id(0) >= ns, p_ref[...], s_ref[...])


def _pair_store(s_ref, p_ref, ns, val):
    @pl.when(pl.program_id(0) < ns)
    def _():
        s_ref[...] = val

    @pl.when(pl.program_id(0) >= ns)
    def _():
        p_ref[...] = val


def _norm_mod(x, nw, sh, sc):
    ms = jnp.mean(x * x, axis=-1, keepdims=True)
    h = x * lax.rsqrt(ms + EPS) * nw
    return h * (1.0 + sc) + sh


def _ret_inproj_kernel(xs_ref, xp_ref, nw_ref, sh_ref, sc_ref, w_ref, gn_ref, o_ref, *, ns, dk, tn, k_scale):
    x = _pair_load(xs_ref, xp_ref, ns)
    hb = _norm_mod(x, nw_ref[...], sh_ref[...], sc_ref[...]).astype(BF16)
    dv = 2 * dk
    hq, hv = RET_HEADS * dk, RET_HEADS * dv
    segments = ((0, dk, 0), (hq, dk, dk), (2 * hq, dv, 2 * dk), (2 * hq + hv, dv, 2 * dk + dv),
                (2 * hq + 2 * hv, dv, 2 * dk + 2 * dv))
    n = w_ref.shape[1]
    for j in range(n // tn):
        lo = j * tn
        seg = max(s for s in range(len(segments)) if segments[s][0] <= lo)
        start, width, off = segments[seg]
        acc = _dot(hb, w_ref[:, lo:lo + tn])
        if seg == 1:
            acc = acc * k_scale
        elif seg >= 3:
            acc = jax.nn.silu(acc) * gn_ref[:, lo - start:lo - start + tn]
        acc = acc.astype(BF16)
        head0 = (lo - start) // width
        for hh in range(tn // width):
            o_ref[head0 + hh, :, off:off + width] = acc[:, hh * width:(hh + 1) * width]


def _ret_inproj(y, mod, layer, norm_w, w_in_bf16, gn_w, dk):
    y_s, y_p = y
    ns, l, d = y_s.shape
    nb = ns + 1
    n = w_in_bf16.shape[1]
    hq = RET_HEADS * dk
    hv = 2 * hq
    tm = _pick(l, (512, 256, 128))
    tn = _pick(hq, (1024, 512, 256, 128))
    wph = n // RET_HEADS
    kern = functools.partial(_ret_inproj_kernel, ns=ns, dk=dk, tn=tn, k_scale=float(dk) ** -0.5)
    return pl.pallas_call(
        kern,
        out_shape=jax.ShapeDtypeStruct((nb, RET_HEADS, l, wph), BF16),
        grid=(nb, l // tm),
        in_specs=_pair_specs(ns, l // tm, tm, d) + [
            pl.BlockSpec((1, d), lambda b, i: (0, 0)),
            _mod_spec(d, layer, 0),
            _mod_spec(d, layer, 1),
            pl.BlockSpec((d, n), lambda b, i: (0, 0), pipeline_mode=pl.Buffered(1)),
            pl.BlockSpec((1, hv), lambda b, i: (0, 0)),
        ],
        out_specs=pl.BlockSpec((None, RET_HEADS, tm, wph), lambda b, i: (b, 0, i, 0)),
        compiler_params=_params(("arbitrary", "arbitrary")),
        name="ret_inproj",
    )(y_s, y_p, norm_w.reshape(1, d), mod, mod, w_in_bf16, gn_w.reshape(1, hv))


def _log_sigmoid(x):
    return jnp.minimum(x, 0.0) - jnp.log1p(jnp.exp(-jnp.abs(x)))


def _ret_kernel(*refs, ls, c, dk, has_init, emit_final):
    it = iter(refs)
    x_ref, df_ref, db_ref = (next(it) for _ in range(3))
    if has_init:
        s0f_ref, s0b_ref = next(it), next(it)
    y_ref = next(it)
    if emit_final:
        sf_out, sb_out = next(it), next(it)
    s_scr, sb_scr = next(it), next(it)

    dv = 2 * dk
    q_ref = x_ref.at[:, 0:dk]
    k_ref = x_ref.at[:, dk:2 * dk]
    v_ref = x_ref.at[:, 2 * dk:2 * dk + dv]
    gf_ref = x_ref.at[:, 2 * dk + dv:2 * dk + 2 * dv]
    gb_ref = x_ref.at[:, 2 * dk + 2 * dv:2 * dk + 3 * dv]
    nchunks = ls // c
    lg_f = _log_sigmoid(df_ref[...])[:, 0:1]
    lg_b = _log_sigmoid(db_ref[...])[:, 0:1]

    ri = lax.broadcasted_iota(I32, (c, c), 0)
    ci = lax.broadcasted_iota(I32, (c, c), 1)
    dif = (ri - ci).astype(F32)
    decay_f = jnp.where(dif >= 0, jnp.exp(jnp.maximum(dif, 0.0) * lg_f), 0.0).astype(BF16)
    decay_b = jnp.where(dif <= 0, jnp.exp(jnp.maximum(-dif, 0.0) * lg_b), 0.0).astype(BF16)
    pos = lax.broadcasted_iota(I32, (c, dk), 0).astype(F32)
    qdec_f = jnp.exp((pos + 1.0) * lg_f).astype(BF16)
    kdec_f = jnp.exp((c - 1.0 - pos) * lg_f).astype(BF16)
    qdec_b = jnp.exp((c - pos) * lg_b).astype(BF16)
    kdec_b = jnp.exp(pos * lg_b).astype(BF16)
    cdec_f = jnp.exp(c * lg_f)
    cdec_b = jnp.exp(c * lg_b)

    if has_init:
        s_scr[...] = s0b_ref[...]
    else:
        s_scr[...] = jnp.zeros_like(s_scr)

    def bwd_body(t, carry):
        n = nchunks - 1 - t
        r0 = pl.multiple_of(n * c, c)
        s = s_scr[...]
        sb_scr[n] = s.astype(BF16)
        kc = k_ref[pl.ds(r0, c), :]
        vc = v_ref[pl.ds(r0, c), :]
        s_scr[...] = s * cdec_b + _dot_tn(kc * kdec_b, vc)
        return carry

    lax.fori_loop(0, nchunks, bwd_body, 0, unroll=2 if nchunks % 2 == 0 else 1)
    if emit_final:
        sb_out[...] = s_scr[...]

    if has_init:
        s_scr[...] = s0f_ref[...]
    else:
        s_scr[...] = jnp.zeros_like(s_scr)

    def head_norm(o):
        mu = jnp.mean(o, axis=-1, keepdims=True)
        dlt = o - mu
        var = jnp.mean(dlt * dlt, axis=-1, keepdims=True)
        return dlt * lax.rsqrt(var + EPS)

    def fwd_body(n, carry):
        r0 = pl.multiple_of(n * c, c)
        rows = pl.ds(r0, c)
        qc = q_ref[rows, :]
        kc = k_ref[rows, :]
        vc = v_ref[rows, :]
        sc = _dot_nt(qc, kc).astype(BF16)
        s = s_scr[...]
        o_f = _dot(sc * decay_f, vc) + _dot(qc * qdec_f, s.astype(BF16))
        o_b = _dot(sc * decay_b, vc) + _dot(qc * qdec_b, sb_scr[n])
        s_scr[...] = s * cdec_f + _dot_tn(kc * kdec_f, vc)
        y_ref[rows, :] = (head_norm(o_f).astype(BF16) * gf_ref[rows, :]
                          + head_norm(o_b).astype(BF16) * gb_ref[rows, :])
        return carry

    lax.fori_loop(0, nchunks, fwd_body, 0, unroll=2 if nchunks % 2 == 0 else 1)
    if emit_final:
        sf_out[...] = s_scr[...]


def _retention(qkvg, nseq, row0, ls, dk, decay_f, decay_b, s0f=None, s0b=None, emit_final=False):
    h = RET_HEADS
    dv = 2 * dk
    c = _pick(ls, (256, 128))
    has_init = s0f is not None
    spr = qkvg.shape[2] // ls
    in_specs = [
        pl.BlockSpec((None, None, ls, qkvg.shape[3]), lambda s, hh: (row0 + s // spr, hh, s % spr, 0)),
        pl.BlockSpec((None, 1, 128), lambda s, hh: (hh, 0, 0)),
        pl.BlockSpec((None, 1, 128), lambda s, hh: (hh, 0, 0)),
    ]
    args = [qkvg,
            jnp.broadcast_to(decay_f.reshape(h, 1, 1), (h, 1, 128)),
            jnp.broadcast_to(decay_b.reshape(h, 1, 1), (h, 1, 128))]
    if has_init:
        in_specs += [pl.BlockSpec((None, None, dk, dv), lambda s, hh: (s, hh, 0, 0))] * 2
        args += [s0f, s0b]
    out_shape = [jax.ShapeDtypeStruct((nseq, ls, h * dv), BF16)]
    out_specs = [pl.BlockSpec((None, ls, dv), lambda s, hh: (s, 0, hh))]
    if emit_final:
        out_shape += [jax.ShapeDtypeStruct((nseq, h, dk, dv), F32)] * 2
        out_specs += [pl.BlockSpec((None, None, dk, dv), lambda s, hh: (s, hh, 0, 0))] * 2
    kern = functools.partial(_ret_kernel, ls=ls, c=c, dk=dk, has_init=has_init, emit_final=emit_final)
    return pl.pallas_call(
        kern,
        out_shape=out_shape,
        grid=(nseq, h),
        in_specs=in_specs,
        out_specs=out_specs,
        scratch_shapes=[pltpu.VMEM((dk, dv), F32), pltpu.VMEM((ls // c, dk, dv), BF16)],
        compiler_params=_params(("arbitrary", "arbitrary")),
        name="retention_init" if has_init else "retention_zero",
    )(*args)


def _qkv_rows(x, rows, nw_ref, sh_ref, sc_ref, w_ref, qn_ref, kn_ref, cos_ref, sa_ref, sb_ref,
              o_ref, kp_ref, vp_ref, *, ns, dh):
    hb = _norm_mod(x, nw_ref[...], sh_ref[...], sc_ref[...]).astype(BF16)
    nq = ATT_HEADS * dh
    nk = ATT_KV_HEADS * dh
    cos, sa, sb = cos_ref[rows, :], sa_ref[rows, :], sb_ref[rows, :]
    is_prompt = pl.program_id(0) >= ns

    def norm_rope(a, w):
        a = a * lax.rsqrt(jnp.mean(a * a, axis=-1, keepdims=True) + EPS) * w
        roped = a * cos + pltpu.roll(a, dh - dh // 4, 1) * sa + pltpu.roll(a, dh // 4, 1) * sb
        return a, roped

    q = _dot(hb, w_ref[:, :nq])
    for hh in range(ATT_HEADS):
        _, r = norm_rope(q[:, hh * dh:(hh + 1) * dh], qn_ref[...])
        o_ref[rows, hh * dh:(hh + 1) * dh] = r.astype(BF16)
    kvv = _dot(hb, w_ref[:, nq:])
    k_norm = []
    for hh in range(ATT_KV_HEADS):
        a, r = norm_rope(kvv[:, hh * dh:(hh + 1) * dh], kn_ref[...])
        o_ref[rows, nq + hh * dh:nq + (hh + 1) * dh] = r.astype(BF16)
        k_norm.append(a)
    o_ref[rows, nq + nk:] = kvv[:, nk:].astype(BF16)

    @pl.when(is_prompt)
    def _():
        for hh in range(ATT_KV_HEADS):
            kp_ref[rows, hh * dh:(hh + 1) * dh] = k_norm[hh]
        vp_ref[rows, :] = kvv[:, nk:]


def _rope_tables(l, dh, rotate):
    axis_dim = dh // 2
    half = axis_dim // 2
    t = jnp.arange(l)
    row = (t // GRID_W).astype(F32)
    col = (t % GRID_W).astype(F32)
    inv = ROPE_THETA ** (-jnp.arange(0, axis_dim, 2, dtype=F32) / axis_dim)
    d = jnp.arange(dh)
    pos = jnp.where((d // axis_dim)[None, :] == 0, row[:, None], col[:, None])
    ang = pos * inv[d % half][None, :]
    first = ((d % axis_dim) < half)[None, :]
    cos, sin = jnp.cos(ang), jnp.sin(ang)
    if not rotate:
        cos, sin = jnp.ones_like(cos), jnp.zeros_like(sin)
    return cos, jnp.where(first, -sin, 0.0), jnp.where(first, 0.0, sin)


def _qkv_operands(ns, l, d, tm, mod, layer, norm_w, w_bf16, qn, kn, dh):
    n = w_bf16.shape[1]
    nk = ATT_KV_HEADS * dh
    tabs = [jnp.stack([a, b]) for a, b in zip(_rope_tables(l, dh, True), _rope_tables(l, dh, False))]
    tab_spec = pl.BlockSpec((None, tm, dh), lambda b, i, *_: (jnp.where(b < ns, 0, 1), i, 0))
    prompt_spec = pl.BlockSpec((tm, nk), lambda b, i, *_: (jnp.where(b < ns, 0, i), 0))
    mod_spec = lambda j: pl.BlockSpec((None, None, None, 1, d), lambda b, i, *_: (layer, j, b, 0, 0))
    in_specs = [
        pl.BlockSpec((1, d), lambda b, i, *_: (0, 0)), mod_spec(0), mod_spec(1),
        pl.BlockSpec((d, n), lambda b, i, *_: (0, 0), pipeline_mode=pl.Buffered(1)),
        pl.BlockSpec((1, dh), lambda b, i, *_: (0, 0)),
        pl.BlockSpec((1, dh), lambda b, i, *_: (0, 0)),
        tab_spec, tab_spec, tab_spec,
    ]
    args = [norm_w.reshape(1, d), mod, mod, w_bf16, qn.reshape(1, dh), kn.reshape(1, dh), *tabs]
    out_shape = [jax.ShapeDtypeStruct((ns + 1, l, n), BF16), jax.ShapeDtypeStruct((l, nk), F32),
                 jax.ShapeDtypeStruct((l, nk), F32)]
    out_specs = [pl.BlockSpec((None, tm, n), lambda b, i, *_: (b, i, 0)), prompt_spec, prompt_spec]
    return in_specs, args, out_shape, out_specs


def _attn_kernel(*refs, dh, group, has_cache, scale):
    if has_cache:
        q_ref, k_ref, v_ref, kc_ref, vc_ref, o_ref, kall, vext = refs
        past = kc_ref.shape[0]
    else:
        q_ref, k_ref, v_ref, o_ref, kall, vext = refs
        past = 0

    @pl.when(pl.program_id(2) == 0)
    def _():
        if has_cache:
            kall[:past, :] = kc_ref[...].astype(BF16)
            vext[:past, :dh] = vc_ref[...].astype(BF16)
        kall[past:, :] = k_ref[...]
        vext[past:, :dh] = v_ref[...]
        vext[:, dh:] = jnp.ones((vext.shape[0], dh), BF16)

    c = scale * 1.4426950408889634
    tq = q_ref.shape[0]
    q = jnp.concatenate([q_ref[:, g * dh:(g + 1) * dh] for g in range(group)], axis=0)
    lk = kall.shape[0]
    ck = _pick(lk, (1152, 512, 256, 128))
    m = None
    for j in range(lk // ck):
        s = _dot_nt(q, kall[j * ck:(j + 1) * ck, :])
        mj = jnp.max(s, axis=-1, keepdims=True)
        vj = vext[j * ck:(j + 1) * ck, :]
        if m is None:
            m = mj
            o = _dot(jnp.exp2((s - m) * c).astype(BF16), vj)
        else:
            m_new = jnp.maximum(m, mj)
            o = o * jnp.exp2((m - m_new) * c) + _dot(jnp.exp2((s - m_new) * c).astype(BF16), vj)
            m = m_new
    for g in range(group):
        og = o[g * tq:(g + 1) * tq]
        o_ref[:, g * dh:(g + 1) * dh] = (og[:, :dh] / og[:, dh:dh + 1]).astype(BF16)


def _attention(qkv, nseq, seq0, ls, dh, cache_k=None, cache_v=None):
    group = ATT_HEADS // ATT_KV_HEADS
    has_cache = cache_k is not None
    tq = _pick(ls, (256, 128))
    kcol = ATT_HEADS
    vcol = ATT_HEADS + ATT_KV_HEADS
    in_specs = [
        pl.BlockSpec((None, tq, group * dh), lambda s, kh, i: (s + seq0, i, kh)),
        pl.BlockSpec((None, ls, dh), lambda s, kh, i: (s + seq0, 0, kcol + kh)),
        pl.BlockSpec((None, ls, dh), lambda s, kh, i: (s + seq0, 0, vcol + kh)),
    ]
    args = [qkv, qkv, qkv]
    past = 0
    if has_cache:
        past = cache_k.shape[1]
        in_specs += [pl.BlockSpec((None, past, dh), lambda s, kh, i: (s, 0, kh))] * 2
        args += [cache_k, cache_v]
    kern = functools.partial(_attn_kernel, dh=dh, group=group, has_cache=has_cache, scale=float(dh) ** -0.5)
    return pl.pallas_call(
        kern,
        out_shape=jax.ShapeDtypeStruct((nseq, ls, ATT_HEADS * dh), BF16),
        grid=(nseq, ATT_KV_HEADS, ls // tq),
        in_specs=in_specs,
        out_specs=pl.BlockSpec((None, tq, group * dh), lambda s, kh, i: (s, i, kh)),
        scratch_shapes=[pltpu.VMEM((past + ls, dh), BF16), pltpu.VMEM((past + ls, 2 * dh), BF16)],
        compiler_params=_params(("arbitrary", "arbitrary", "arbitrary")),
        name="attention_cache" if has_cache else "attention_self",
    )(*args)


def _pack_rows(h):
    half = h.shape[1] // 2
    u = lax.bitcast_convert_type(h.astype(BF16).astype(F32), U32)
    return (u[:, :half] >> 16) | (u[:, half:] & jnp.uint32(0xFFFF0000))


def _unpack_rows(p):
    lo = lax.bitcast_convert_type(p << 16, F32)
    hi = lax.bitcast_convert_type(p & jnp.uint32(0xFFFF0000), F32)
    return jnp.concatenate([lo, hi], axis=1).astype(BF16)


def _pack_pair(a, b):
    ua = lax.bitcast_convert_type(a.astype(BF16).astype(F32), U32)
    ub = lax.bitcast_convert_type(b.astype(BF16).astype(F32), U32)
    return (ua >> 16) | (ub & jnp.uint32(0xFFFF0000))


def _unpack_pair(p):
    return (lax.bitcast_convert_type(p << 16, F32),
            lax.bitcast_convert_type(p & jnp.uint32(0xFFFF0000), F32))


def _outproj_kernel(os_ref, op_ref, w_ref, ys_ref, yp_ref, g1_ref, nw_ref, sh_ref, sc_ref, rwh_ref, rwl_ref,
                    yos_ref, yop_ref, h_ref, lg_ref, *, ns):
    o = _pair_load(os_ref, op_ref, ns)
    y = _pair_load(ys_ref, yp_ref, ns) + g1_ref[...] * _dot(o, w_ref[...])
    _pair_store(yos_ref, yop_ref, ns, y)
    h = _norm_mod(y, nw_ref[...], sh_ref[...], sc_ref[...])
    h_ref[...] = _pack_rows(h)
    h_hi = h.astype(BF16)
    h_lo = (h - h_hi.astype(F32)).astype(BF16)
    rwh = rwh_ref[...]
    lg_ref[...] = _dot_nt(rwh, h_hi) + _dot_nt(rwl_ref[...], h_hi) + _dot_nt(rwh, h_lo)


def _outproj(o_s, o_p, w_bf16, y, mod, layer, norm_w, rw_hi, rw_lo):
    y_s, y_p = y
    ns, l, d = y_s.shape
    nb = ns + 1
    kdim = o_s.shape[2]
    ne = rw_hi.shape[0]
    tm = _pick(l, (512, 256, 128))
    ni = l // tm
    dp = jax.eval_shape(_pack_rows, jax.ShapeDtypeStruct((8, d), F32)).shape[1]
    outs = pl.pallas_call(
        functools.partial(_outproj_kernel, ns=ns),
        out_shape=[jax.ShapeDtypeStruct(y_s.shape, F32), jax.ShapeDtypeStruct(y_p.shape, F32),
                   jax.ShapeDtypeStruct((nb, l, dp), U32), jax.ShapeDtypeStruct((nb, ne, l), F32)],
        grid=(nb, ni),
        in_specs=_pair_specs(ns, ni, tm, kdim) + [
            pl.BlockSpec((kdim, d), lambda b, i: (0, 0), pipeline_mode=pl.Buffered(1)),
        ] + _pair_specs(ns, ni, tm, d) + [
            _mod_spec(d, layer, 2),
            pl.BlockSpec((1, d), lambda b, i: (0, 0)),
            _mod_spec(d, layer, 3),
            _mod_spec(d, layer, 4),
            pl.BlockSpec((ne, d), lambda b, i: (0, 0)),
            pl.BlockSpec((ne, d), lambda b, i: (0, 0)),
        ],
        out_specs=_pair_specs(ns, ni, tm, d) + [
            pl.BlockSpec((None, tm, dp), lambda b, i: (b, i, 0)),
            pl.BlockSpec((None, ne, tm), lambda b, i: (b, 0, i))],
        compiler_params=_params(("arbitrary", "arbitrary")),
        name="outproj_router",
    )(o_s, o_p, w_bf16, y_s, y_p, mod, norm_w.reshape(1, d), mod, mod, rw_hi, rw_lo)
    return (outs[0], outs[1]), outs[2], outs[3]


def _router_kernel(lg_ref, b_ref, bkt_ref, rank_ref, wt_ref, cnt_ref, carry_ref):
    ng = N_GROUPS
    sc = jax.nn.sigmoid(lg_ref[...])
    sel = sc + b_ref[...]
    sj = [sel[j * ng:(j + 1) * ng] for j in range(EXPERTS_PER_GROUP)]
    pj = [sc[j * ng:(j + 1) * ng] for j in range(EXPERTS_PER_GROUP)]
    hi1, lo1 = jnp.maximum(sj[0], sj[1]), jnp.minimum(sj[0], sj[1])
    hi2, lo2 = jnp.maximum(sj[2], sj[3]), jnp.minimum(sj[2], sj[3])
    top1 = jnp.maximum(hi1, hi2)
    top2 = jnp.maximum(jnp.minimum(hi1, hi2), jnp.maximum(lo1, lo2))
    gs = top1 + top2
    gmax = jnp.max(gs, axis=0, keepdims=True)
    gi = lax.broadcasted_iota(I32, gs.shape, 0)
    best = jnp.min(jnp.where(gs == gmax, gi, ng), axis=0, keepdims=True)
    onehot = gi == best
    neg = -jnp.inf
    vj = [jnp.max(jnp.where(onehot, a, neg), axis=0, keepdims=True) for a in sj]
    wj = [jnp.max(jnp.where(onehot, a, neg), axis=0, keepdims=True) for a in pj]

    def argmax4(vals):
        m, i = vals[0], jnp.zeros(vals[0].shape, I32)
        for j in range(1, EXPERTS_PER_GROUP):
            gt = vals[j] > m
            m = jnp.where(gt, vals[j], m)
            i = jnp.where(gt, j, i)
        return i

    i1 = argmax4(vj)
    i2 = argmax4([jnp.where(i1 == j, neg, vj[j]) for j in range(EXPERTS_PER_GROUP)])

    def take(vals, i):
        out = vals[0]
        for j in range(1, EXPERTS_PER_GROUP):
            out = jnp.where(i == j, vals[j], out)
        return out

    w1, w2 = take(wj, i1), take(wj, i2)
    tot = w1 + w2
    w1, w2 = w1 / tot, w2 / tot

    lo, hi = jnp.minimum(i1, i2), jnp.maximum(i1, i2)
    pair = jnp.where(lo == 0, hi - 1, jnp.where(lo == 1, jnp.where(hi == 3, 3, 4), 5))
    slot_a = jnp.where(pair < 3, 0, jnp.where(pair < 5, 1, 3))
    bucket = best * N_PAIRS + pair
    a_first = i1 == slot_a
    w_a = jnp.where(a_first, w1, w2)
    w_b = jnp.where(a_first, w2, w1)

    tl = bucket.shape[1]
    nbk = carry_ref.shape[0]

    @pl.when(jnp.logical_and(pl.program_id(0) == 0, pl.program_id(1) == 0))
    def _():
        carry_ref[...] = jnp.zeros_like(carry_ref)

    onehot_b = lax.broadcasted_iota(I32, (nbk, tl), 0) == bucket
    upper = (lax.broadcasted_iota(I32, (tl, tl), 0) <= lax.broadcasted_iota(I32, (tl, tl), 1))
    prefix = _dot(onehot_b.astype(BF16), upper.astype(BF16))
    carry = carry_ref[...]
    rank = jnp.sum(jnp.where(onehot_b, prefix - 1.0 + carry, 0.0), axis=0, keepdims=True)
    carry = carry + prefix[:, tl - 1:tl]
    carry_ref[...] = carry
    bkt_ref[...] = bucket
    rank_ref[...] = rank.astype(I32)
    cnt_ref[...] = jnp.broadcast_to(carry, cnt_ref.shape)
    rows = lax.broadcasted_iota(I32, (wt_ref.shape[1], tl), 0)
    wt_ref[...] = jnp.where(rows == 0, w_a, jnp.where(rows == 1, w_b, 0.0)).T


def _router(logits, bias_perm):
    nb, ne, l = logits.shape
    tl = _pick(l, (512, 256, 128))
    nbk = N_GROUPS * N_PAIRS
    return pl.pallas_call(
        _router_kernel,
        out_shape=[jax.ShapeDtypeStruct((nb, 1, l), I32), jax.ShapeDtypeStruct((nb, 1, l), I32),
                   jax.ShapeDtypeStruct((nb, l, 128), F32), jax.ShapeDtypeStruct((nbk, 128), F32)],
        grid=(nb, l // tl),
        in_specs=[pl.BlockSpec((None, ne, tl), lambda b, i: (b, 0, i)),
                  pl.BlockSpec((ne, 1), lambda b, i: (0, 0))],
        out_specs=[pl.BlockSpec((None, 1, tl), lambda b, i: (b, 0, i)),
                   pl.BlockSpec((None, 1, tl), lambda b, i: (b, 0, i)),
                   pl.BlockSpec((None, tl, 128), lambda b, i: (b, i, 0)),
                   pl.BlockSpec((nbk, 128), lambda b, i: (0, 0))],
        scratch_shapes=[pltpu.VMEM((nbk, 1), F32)],
        compiler_params=_params(("arbitrary", "arbitrary")),
        name="router_rank",
    )(logits, bias_perm.reshape(ne, 1))


def _moe_kernel(ea_ref, eb_ref, nu_ref, x_ref, wga_ref, wua_ref, wda_ref, wgb_ref, wub_ref, wdb_ref, o_ref):
    del ea_ref, eb_ref
    i = pl.program_id(0)

    @pl.when(i < nu_ref[0])
    def _():
        x = _unpack_rows(x_ref[...])

        def expert(wg, wu, wd):
            hid = jax.nn.silu(_dot(x, wg[...])) * _dot(x, wu[...])
            return _dot(hid.astype(BF16), wd[...])

        o_ref[...] = _pack_pair(expert(wga_ref, wua_ref, wda_ref), expert(wgb_ref, wub_ref, wdb_ref))

    @pl.when(i >= nu_ref[0])
    def _():
        o_ref[...] = jnp.zeros_like(o_ref)


def _moe_experts(xs, blk_a, blk_b, n_used, wg, wu, wd, layer, bm):
    p, dp = xs.shape
    _, _, d, de = wg.shape
    nblk = p // bm
    row = jax.ShapeDtypeStruct((8, d), F32)
    dq = jax.eval_shape(_pack_pair, row, row).shape[1]
    w_in = lambda sel: pl.BlockSpec((None, None, d, de), lambda i, ea, eb, nu: (layer, (ea, eb)[sel][i], 0, 0))
    w_out = lambda sel: pl.BlockSpec((None, None, de, d), lambda i, ea, eb, nu: (layer, (ea, eb)[sel][i], 0, 0))
    grid_spec = pltpu.PrefetchScalarGridSpec(
        num_scalar_prefetch=3,
        grid=(nblk,),
        in_specs=[pl.BlockSpec((bm, dp), lambda i, ea, eb, nu: (i, 0)),
                  w_in(0), w_in(0), w_out(0), w_in(1), w_in(1), w_out(1)],
        out_specs=pl.BlockSpec((bm, dq), lambda i, ea, eb, nu: (i, 0)),
    )
    return pl.pallas_call(
        _moe_kernel,
        out_shape=jax.ShapeDtypeStruct((p, dq), U32),
        grid_spec=grid_spec,
        compiler_params=_params(("arbitrary",)),
        name="moe_experts",
    )(blk_a, blk_b, n_used, xs, wg, wu, wd, wg, wu, wd)


def _row_plan(bucket, rank, counts, bm):
    t = bucket.size
    nbk = N_GROUPS * N_PAIRS
    cnt = counts[:, 0].astype(I32)
    padded = (cnt + bm - 1) // bm * bm
    pends = jnp.cumsum(padded)
    pstarts = pends - padded
    onehot = bucket.reshape(t, 1) == jnp.arange(nbk, dtype=I32)[None, :]
    dest = jnp.sum(jnp.where(onehot, pstarts[None, :], 0), axis=1) + rank.reshape(t)
    p = (t + bm - 1) // bm * bm + nbk * bm
    nblk = p // bm
    blk_bucket = jnp.clip(jnp.searchsorted(pends, jnp.arange(nblk, dtype=I32) * bm, side='right'), 0, nbk - 1)
    grp, pair = blk_bucket // N_PAIRS, blk_bucket % N_PAIRS
    blk_a = grp * EXPERTS_PER_GROUP + jnp.asarray(_PAIR_A, I32)[pair]
    blk_b = grp * EXPERTS_PER_GROUP + jnp.asarray(_PAIR_B, I32)[pair]
    n_used = (pends[-1:] // bm).astype(I32)
    return dest.astype(I32), blk_a.astype(I32), blk_b.astype(I32), n_used, p


def _scatter_kernel(dest_ref, src_ref, init_hbm, out_hbm, sem, *, rows):
    del init_hbm
    base = pl.program_id(0) * rows
    for r in range(rows):
        pltpu.make_async_copy(src_ref.at[r], out_hbm.at[dest_ref[base + r]], sem).start(priority=r % 2)
    pltpu.make_async_copy(src_ref, out_hbm.at[pl.ds(0, rows)], sem).wait()


def _scatter_rows(h_rows, dest, p):
    t, dp = h_rows.shape
    rows = _pick(t, (512, 256, 128))
    grid_spec = pltpu.PrefetchScalarGridSpec(
        num_scalar_prefetch=1,
        grid=(t // rows,),
        in_specs=[pl.BlockSpec((rows, dp), lambda i, dst: (i, 0)), pl.BlockSpec(memory_space=pl.ANY)],
        out_specs=pl.BlockSpec(memory_space=pl.ANY),
        scratch_shapes=[pltpu.SemaphoreType.DMA],
    )
    return pl.pallas_call(
        functools.partial(_scatter_kernel, rows=rows),
        out_shape=jax.ShapeDtypeStruct((p, dp), h_rows.dtype),
        grid_spec=grid_spec,
        input_output_aliases={2: 0},
        compiler_params=_params(("arbitrary",)),
        name="moe_scatter_rows",
    )(dest, h_rows, jnp.zeros((p, dp), h_rows.dtype))


def _combine_kernel(dest_ref, ys_ref, yp_ref, g_ref, wt_ref, m_hbm, *rest, ns, th, nsteps, qkv_dh):
    if qkv_dh:
        qkv_in, rest = rest[:9], rest[9:]
        os_ref, op_ref, qkv_ref, kp_ref, vp_ref, mbuf0, mbuf1, sem = rest
    else:
        os_ref, op_ref, mbuf0, mbuf1, sem = rest
    step = pl.program_id(0) * pl.num_programs(1) + pl.program_id(1)
    bufs = (mbuf0, mbuf1)

    def gather(first_row, k):
        for r in range(th):
            pltpu.make_async_copy(m_hbm.at[dest_ref[first_row + r]], bufs[k].at[r],
                                  sem.at[k]).start(priority=r % 2)

    def wait(k):
        pltpu.make_async_copy(m_hbm.at[pl.ds(0, th)], bufs[k], sem.at[k]).wait()

    @pl.when(step == 0)
    def _():
        gather(0, 0)

    is_prompt = pl.program_id(0) >= ns
    g = g_ref[...]
    row0 = step * (2 * th)
    for k in range(2):
        rows = slice(k * th, (k + 1) * th)
        wait(k)
        m_a, m_b = _unpack_pair(bufs[k][...])
        w = wt_ref[rows, :]
        y = jnp.where(is_prompt, yp_ref[rows, :], ys_ref[rows, :])
        val = y + g * (w[:, 0:1] * m_a + w[:, 1:2] * m_b)

        @pl.when(jnp.logical_not(is_prompt))
        def _():
            os_ref[rows, :] = val

        @pl.when(is_prompt)
        def _():
            op_ref[rows, :] = val

        nxt = jnp.minimum(row0 + (k + 1) * th, (nsteps * 2 - 1) * th)
        gather(nxt, 1 - k)
        if qkv_dh:
            _qkv_rows(val, rows, *qkv_in, qkv_ref, kp_ref, vp_ref, ns=ns, dh=qkv_dh)

    @pl.when(step == nsteps - 1)
    def _():
        wait(0)


def _combine(y, mod, layer, moe_rows, dest, wt, qkv=None):
    y_s, y_p = y
    ns, l, d = y_s.shape
    nb = ns + 1
    th = _pick(l // 2, (256, 128, 64) if qkv is not None else (512, 256, 128, 64))
    tm = 2 * th
    ni = l // tm
    in_specs = _pair_specs(ns, ni, tm, d) + [
        pl.BlockSpec((None, None, None, 1, d), lambda b, i, dst: (layer, 5, b, 0, 0)),
        pl.BlockSpec((None, tm, 128), lambda b, i, dst: (b, i, 0)),
        pl.BlockSpec(memory_space=pl.ANY),
    ]
    args = [dest, y_s, y_p, mod, wt, moe_rows]
    out_shape = [jax.ShapeDtypeStruct(y_s.shape, F32), jax.ShapeDtypeStruct(y_p.shape, F32)]
    out_specs = _pair_specs(ns, ni, tm, d)
    qkv_dh = 0
    if qkv is not None:
        norm_w, w_bf16, qn, kn, qkv_dh = qkv
        q_specs, q_args, q_shapes, q_out_specs = _qkv_operands(ns, l, d, tm, mod, layer + 1, norm_w, w_bf16,
                                                               qn, kn, qkv_dh)
        in_specs += q_specs
        args += q_args
        out_shape += q_shapes
        out_specs += q_out_specs
    grid_spec = pltpu.PrefetchScalarGridSpec(
        num_scalar_prefetch=1,
        grid=(nb, ni),
        in_specs=in_specs,
        out_specs=out_specs,
        scratch_shapes=[pltpu.VMEM((th, moe_rows.shape[1]), moe_rows.dtype),
                        pltpu.VMEM((th, moe_rows.shape[1]), moe_rows.dtype),
                        pltpu.SemaphoreType.DMA((2,))],
    )
    outs = pl.pallas_call(
        functools.partial(_combine_kernel, ns=ns, th=th, nsteps=nb * ni, qkv_dh=qkv_dh),
        out_shape=out_shape,
        grid_spec=grid_spec,
        compiler_params=_params(("arbitrary", "arbitrary")),
        name="moe_combine_qkv" if qkv_dh else "moe_combine",
    )(*args)
    return (outs[0], outs[1]), (outs[2:] if qkv_dh else None)


def kernel(x_prompt, x_sample, c, state_ret_fwd, state_ret_bwd, cache_attn_k, cache_attn_v, c_ctx,
           ada_w, ada_b, norm1_w, norm2_w, ret_w_in, ret_w_out, ret_gn_w, ret_decay_fwd, ret_decay_bwd,
           attn_w_qkv, attn_w_o, attn_q_norm, attn_k_norm, router_w, router_b,
           moe_w_gate, moe_w_up, moe_w_down):
    n_prompt, seq, d = x_prompt.shape
    ns, l, _ = x_sample.shape
    assert n_prompt * seq == l, "prompt tokens must fill exactly one sample-length row"
    depth = ada_w.shape[0]
    nb = ns + 1
    dk = d // RET_HEADS
    dv = 2 * dk
    dh = d // ATT_HEADS
    bm = MOE_ROW_BLOCK

    y = (x_sample, x_prompt.reshape(1, l, d))
    nbp = (nb + 7) // 8 * 8
    cond = jnp.zeros((nbp, d), F32).at[:ns].set(c).at[ns].set(c_ctx)
    mod = _ada_params(cond, ada_w, ada_b)

    perm = (jnp.arange(N_EXPERTS) % N_GROUPS) * EXPERTS_PER_GROUP + jnp.arange(N_EXPERTS) // N_GROUPS
    rw_t = router_w.T[perm]
    rw_hi = rw_t.astype(BF16)
    rw_lo = (rw_t - rw_hi.astype(F32)).astype(BF16)
    rb_perm = router_b[perm]
    wg_bf16, wu_bf16, wd_bf16 = moe_w_gate.astype(BF16), moe_w_up.astype(BF16), moe_w_down.astype(BF16)

    ret_f, ret_b, k_new, v_new = [], [], [], []
    qkv_next = None
    for i in range(depth):
        if i % 2 == 0:
            r = i // 2
            qkvg = _ret_inproj(y, mod, i, norm1_w[i], ret_w_in[r].astype(BF16), ret_gn_w[r], dk)
            o_s = _retention(qkvg, ns, 0, l, dk, ret_decay_fwd[r], ret_decay_bwd[r],
                             s0f=state_ret_fwd[:, r], s0b=state_ret_bwd[:, r])
            o_p, s_f, s_b = _retention(qkvg, n_prompt, ns, seq, dk,
                                       ret_decay_fwd[r], ret_decay_bwd[r], emit_final=True)
            o_s = o_s[0] if isinstance(o_s, (list, tuple)) else o_s
            ret_f.append(s_f)
            ret_b.append(s_b)
            w_o = ret_w_out[r].astype(BF16)
        else:
            a = i // 2
            qkv, k_p, v_p = qkv_next
            past = cache_attn_k.shape[2]
            o_s = _attention(qkv, ns, 0, l, dh,
                             cache_k=cache_attn_k[:, a].reshape(ns, past, ATT_KV_HEADS * dh),
                             cache_v=cache_attn_v[:, a].reshape(ns, past, ATT_KV_HEADS * dh))
            o_p = _attention(qkv.reshape(nb * n_prompt, seq, qkv.shape[2]), n_prompt, ns * n_prompt, seq, dh)
            k_new.append(k_p.reshape(n_prompt, seq, ATT_KV_HEADS, dh))
            v_new.append(v_p.reshape(n_prompt, seq, ATT_KV_HEADS, dh))
            w_o = attn_w_o[a].astype(BF16)
        o_p = o_p.reshape(1, l, o_p.shape[-1])
        y, h2, logits = _outproj(o_s, o_p, w_o, y, mod, i, norm2_w[i], rw_hi, rw_lo)
        bucket, rank, wt, counts = _router(logits, rb_perm)
        dest, blk_a, blk_b, n_used, p = _row_plan(bucket, rank, counts, bm)
        xs = _scatter_rows(h2.reshape(nb * l, h2.shape[2]), dest, p)
        moe_rows = _moe_experts(xs, blk_a, blk_b, n_used, wg_bf16, wu_bf16, wd_bf16, i, bm)
        fuse = None
        if i + 1 < depth and (i + 1) % 2 == 1:
            a = (i + 1) // 2
            fuse = (norm1_w[i + 1], attn_w_qkv[a].astype(BF16), attn_q_norm[a], attn_k_norm[a], dh)
        y, qkv_next = _combine(y, mod, i, moe_rows, dest, wt, qkv=fuse)

    y_s, y_p = y
    return (y_p.reshape(n_prompt, seq, d), y_s, jnp.stack(ret_f, axis=1), jnp.stack(ret_b, axis=1),
            jnp.stack(k_new, axis=1), jnp.stack(v_new, axis=1))
```

```python
import functools

import jax
import jax.numpy as jnp
from jax import lax
from jax.experimental import pallas as pl
from jax.experimental.pallas import tpu as pltpu

F32 = jnp.float32
BF16 = jnp.bfloat16
I32 = jnp.int32
U32 = jnp.uint32

EPS = 1e-6
GRID_W = 64
RET_HEADS = 4
ATT_HEADS = 8
ATT_KV_HEADS = 2
ROPE_THETA = 10000.0
N_EXPERTS = 32
N_GROUPS = 8
EXPERTS_PER_GROUP = N_EXPERTS // N_GROUPS
TOP_K = 2
_PAIR_A = (0, 0, 0, 1, 1, 3)
_PAIR_B = (1, 2, 3, 3, 2, 2)
N_PAIRS = len(_PAIR_A)
MOE_ROW_BLOCK = 256

V7X_VMEM_BYTES = 64 * 1024 * 1024
VMEM_LIMIT_BYTES = 56 * 1024 * 1024


def _params(sem):
    return pltpu.CompilerParams(dimension_semantics=sem, vmem_limit_bytes=VMEM_LIMIT_BYTES)


def _dot(a, b):
    return jnp.dot(a, b, preferred_element_type=F32)


def _dot_nt(a, b):
    return lax.dot_general(a, b, (((1,), (1,)), ((), ())), preferred_element_type=F32)


def _dot_tn(a, b):
    return lax.dot_general(a, b, (((0,), (0,)), ((), ())), preferred_element_type=F32)


def _pick(n, prefs):
    for p in prefs:
        if n % p == 0:
            return p
    return n


def _ada_kernel(c_ref, w_ref, b_ref, o_ref):
    c = c_ref[...]
    s = jax.nn.silu(c).astype(BF16)
    o_ref[...] = _dot(s, w_ref[...].astype(BF16)) + b_ref[...]


def _ada_params(cond, ada_w, ada_b):
    depth, d, n6 = ada_w.shape
    nbp = cond.shape[0]
    tn = _pick(n6, (1536, 1024, 512, 256, 128))
    out = pl.pallas_call(
        _ada_kernel,
        out_shape=jax.ShapeDtypeStruct((depth, nbp, n6), F32),
        grid=(depth, n6 // tn),
        in_specs=[
            pl.BlockSpec((nbp, d), lambda l, j: (0, 0)),
            pl.BlockSpec((None, d, tn), lambda l, j: (l, 0, j)),
            pl.BlockSpec((None, 1, tn), lambda l, j: (l, 0, j)),
        ],
        out_specs=pl.BlockSpec((None, nbp, tn), lambda l, j: (l, 0, j)),
        compiler_params=_params(("arbitrary", "arbitrary")),
        name="ada_params",
    )(cond, ada_w, ada_b.reshape(depth, 1, n6))
    return out.reshape(depth, nbp, 6, 1, d).transpose(0, 2, 1, 3, 4)


def _mod_spec(d, layer, j):
    return pl.BlockSpec((None, None, None, 1, d), lambda b, i: (layer, j, b, 0, 0))


def _pair_specs(ns, ni, tm, w):
    return [pl.BlockSpec((None, tm, w), lambda b, i, *_: (jnp.minimum(b, ns - 1), jnp.where(b < ns, i, ni - 1), 0)),
            pl.BlockSpec((None, tm, w), lambda b, i, *_: (0, jnp.where(b < ns, 0, i), 0))]


def _pair_load(s_ref, p_ref, ns):
    return jnp.where(pl.program_id(0) >= ns, p_ref[...], s_ref[...])


def _pair_store(s_ref, p_ref, ns, val):
    @pl.when(pl.program_id(0) < ns)
    def _():
        s_ref[...] = val

    @pl.when(pl.program_id(0) >= ns)
    def _():
        p_ref[...] = val


def _norm_mod(x, nw, sh, sc):
    ms = jnp.mean(x * x, axis=-1, keepdims=True)
    h = x * lax.rsqrt(ms + EPS) * nw
    return h * (1.0 + sc) + sh


def _ret_inproj_kernel(xs_ref, xp_ref, nw_ref, sh_ref, sc_ref, w_ref, gn_ref, o_ref, *, ns, dk, tn, k_scale):
    x = _pair_load(xs_ref, xp_ref, ns)
    hb = _norm_mod(x, nw_ref[...], sh_ref[...], sc_ref[...]).astype(BF16)
    dv = 2 * dk
    hq, hv = RET_HEADS * dk, RET_HEADS * dv
    segments = ((0, dk, 0), (hq, dk, dk), (2 * hq, dv, 2 * dk), (2 * hq + hv, dv, 2 * dk + dv),
                (2 * hq + 2 * hv, dv, 2 * dk + 2 * dv))
    n = w_ref.shape[1]
    for j in range(n // tn):
        lo = j * tn
        seg = max(s for s in range(len(segments)) if segments[s][0] <= lo)
        start, width, off = segments[seg]
        acc = _dot(hb, w_ref[:, lo:lo + tn])
        if seg == 1:
            acc = acc * k_scale
        elif seg >= 3:
            acc = jax.nn.silu(acc) * gn_ref[:, lo - start:lo - start + tn]
        acc = acc.astype(BF16)
        head0 = (lo - start) // width
        for hh in range(tn // width):
            o_ref[head0 + hh, :, off:off + width] = acc[:, hh * width:(hh + 1) * width]


def _ret_inproj(y, mod, layer, norm_w, w_in_bf16, gn_w, dk):
    y_s, y_p = y
    ns, l, d = y_s.shape
    nb = ns + 1
    n = w_in_bf16.shape[1]
    hq = RET_HEADS * dk
    hv = 2 * hq
    tm = _pick(l, (512, 256, 128))
    tn = _pick(hq, (1024, 512, 256, 128))
    wph = n // RET_HEADS
    kern = functools.partial(_ret_inproj_kernel, ns=ns, dk=dk, tn=tn, k_scale=float(dk) ** -0.5)
    return pl.pallas_call(
        kern,
        out_shape=jax.ShapeDtypeStruct((nb, RET_HEADS, l, wph), BF16),
        grid=(nb, l // tm),
        in_specs=_pair_specs(ns, l // tm, tm, d) + [
            pl.BlockSpec((1, d), lambda b, i: (0, 0)),
            _mod_spec(d, layer, 0),
            _mod_spec(d, layer, 1),
            pl.BlockSpec((d, n), lambda b, i: (0, 0), pipeline_mode=pl.Buffered(1)),
            pl.BlockSpec((1, hv), lambda b, i: (0, 0)),
        ],
        out_specs=pl.BlockSpec((None, RET_HEADS, tm, wph), lambda b, i: (b, 0, i, 0)),
        compiler_params=_params(("arbitrary", "arbitrary")),
        name="ret_inproj",
    )(y_s, y_p, norm_w.reshape(1, d), mod, mod, w_in_bf16, gn_w.reshape(1, hv))


def _log_sigmoid(x):
    return jnp.minimum(x, 0.0) - jnp.log1p(jnp.exp(-jnp.abs(x)))


def _ret_kernel(*refs, ls, c, dk, has_init, emit_final):
    it = iter(refs)
    x_ref, df_ref, db_ref = (next(it) for _ in range(3))
    if has_init:
        s0f_ref, s0b_ref = next(it), next(it)
    y_ref = next(it)
    if emit_final:
        sf_out, sb_out = next(it), next(it)
    s_scr, sb_scr = next(it), next(it)

    dv = 2 * dk
    q_ref = x_ref.at[:, 0:dk]
    k_ref = x_ref.at[:, dk:2 * dk]
    v_ref = x_ref.at[:, 2 * dk:2 * dk + dv]
    gf_ref = x_ref.at[:, 2 * dk + dv:2 * dk + 2 * dv]
    gb_ref = x_ref.at[:, 2 * dk + 2 * dv:2 * dk + 3 * dv]
    nchunks = ls // c
    lg_f = _log_sigmoid(df_ref[...])[:, 0:1]
    lg_b = _log_sigmoid(db_ref[...])[:, 0:1]

    ri = lax.broadcasted_iota(I32, (c, c), 0)
    ci = lax.broadcasted_iota(I32, (c, c), 1)
    dif = (ri - ci).astype(F32)
    decay_f = jnp.where(dif >= 0, jnp.exp(jnp.maximum(dif, 0.0) * lg_f), 0.0).astype(BF16)
    decay_b = jnp.where(dif <= 0, jnp.exp(jnp.maximum(-dif, 0.0) * lg_b), 0.0).astype(BF16)
    pos = lax.broadcasted_iota(I32, (c, dk), 0).astype(F32)
    qdec_f = jnp.exp((pos + 1.0) * lg_f).astype(BF16)
    kdec_f = jnp.exp((c - 1.0 - pos) * lg_f).astype(BF16)
    qdec_b = jnp.exp((c - pos) * lg_b).astype(BF16)
    kdec_b = jnp.exp(pos * lg_b).astype(BF16)
    cdec_f = jnp.exp(c * lg_f)
    cdec_b = jnp.exp(c * lg_b)

    if has_init:
        s_scr[...] = s0b_ref[...]
    else:
        s_scr[...] = jnp.zeros_like(s_scr)

    def bwd_body(t, carry):
        n = nchunks - 1 - t
        r0 = pl.multiple_of(n * c, c)
        s = s_scr[...]
        sb_scr[n] = s.astype(BF16)
        kc = k_ref[pl.ds(r0, c), :]
        vc = v_ref[pl.ds(r0, c), :]
        s_scr[...] = s * cdec_b + _dot_tn(kc * kdec_b, vc)
        return carry

    lax.fori_loop(0, nchunks, bwd_body, 0, unroll=2 if nchunks % 2 == 0 else 1)
    if emit_final:
        sb_out[...] = s_scr[...]

    if has_init:
        s_scr[...] = s0f_ref[...]
    else:
        s_scr[...] = jnp.zeros_like(s_scr)

    def head_norm(o):
        mu = jnp.mean(o, axis=-1, keepdims=True)
        dlt = o - mu
        var = jnp.mean(dlt * dlt, axis=-1, keepdims=True)
        return dlt * lax.rsqrt(var + EPS)

    def fwd_body(n, carry):
        r0 = pl.multiple_of(n * c, c)
        rows = pl.ds(r0, c)
        qc = q_ref[rows, :]
        kc = k_ref[rows, :]
        vc = v_ref[rows, :]
        sc = _dot_nt(qc, kc).astype(BF16)
        s = s_scr[...]
        o_f = _dot(sc * decay_f, vc) + _dot(qc * qdec_f, s.astype(BF16))
        o_b = _dot(sc * decay_b, vc) + _dot(qc * qdec_b, sb_scr[n])
        s_scr[...] = s * cdec_f + _dot_tn(kc * kdec_f, vc)
        y_ref[rows, :] = (head_norm(o_f).astype(BF16) * gf_ref[rows, :]
                          + head_norm(o_b).astype(BF16) * gb_ref[rows, :])
        return carry

    lax.fori_loop(0, nchunks, fwd_body, 0, unroll=2 if nchunks % 2 == 0 else 1)
    if emit_final:
        sf_out[...] = s_scr[...]


def _retention(qkvg, nseq, row0, ls, dk, decay_f, decay_b, s0f=None, s0b=None, emit_final=False):
    h = RET_HEADS
    dv = 2 * dk
    c = _pick(ls, (256, 128))
    has_init = s0f is not None
    spr = qkvg.shape[2] // ls
    in_specs = [
        pl.BlockSpec((None, None, ls, qkvg.shape[3]), lambda s, hh: (row0 + s // spr, hh, s % spr, 0)),
        pl.BlockSpec((None, 1, 128), lambda s, hh: (hh, 0, 0)),
        pl.BlockSpec((None, 1, 128), lambda s, hh: (hh, 0, 0)),
    ]
    args = [qkvg,
            jnp.broadcast_to(decay_f.reshape(h, 1, 1), (h, 1, 128)),
            jnp.broadcast_to(decay_b.reshape(h, 1, 1), (h, 1, 128))]
    if has_init:
        in_specs += [pl.BlockSpec((None, None, dk, dv), lambda s, hh: (s, hh, 0, 0))] * 2
        args += [s0f, s0b]
    out_shape = [jax.ShapeDtypeStruct((nseq, ls, h * dv), BF16)]
    out_specs = [pl.BlockSpec((None, ls, dv), lambda s, hh: (s, 0, hh))]
    if emit_final:
        out_shape += [jax.ShapeDtypeStruct((nseq, h, dk, dv), F32)] * 2
        out_specs += [pl.BlockSpec((None, None, dk, dv), lambda s, hh: (s, hh, 0, 0))] * 2
    kern = functools.partial(_ret_kernel, ls=ls, c=c, dk=dk, has_init=has_init, emit_final=emit_final)
    return pl.pallas_call(
        kern,
        out_shape=out_shape,
        grid=(nseq, h),
        in_specs=in_specs,
        out_specs=out_specs,
        scratch_shapes=[pltpu.VMEM((dk, dv), F32), pltpu.VMEM((ls // c, dk, dv), BF16)],
        compiler_params=_params(("arbitrary", "arbitrary")),
        name="retention_init" if has_init else "retention_zero",
    )(*args)


def _qkv_kernel(xs_ref, xp_ref, nw_ref, sh_ref, sc_ref, w_ref, qn_ref, kn_ref, cos_ref, sa_ref, sb_ref,
                o_ref, kp_ref, vp_ref, *, ns, dh):
    nq = ATT_HEADS * dh
    nk = ATT_KV_HEADS * dh
    is_prompt = pl.program_id(0) >= ns
    tm = xs_ref.shape[0]
    nsub = 2 if tm % 256 == 0 else 1
    ts = tm // nsub
    cache_rows = []
    for sub in range(nsub):
        rows = slice(sub * ts, (sub + 1) * ts)
        x = jnp.where(is_prompt, xp_ref[rows, :], xs_ref[rows, :])
        hb = _norm_mod(x, nw_ref[...], sh_ref[...], sc_ref[...]).astype(BF16)
        cos, sa, sb = cos_ref[rows, :], sa_ref[rows, :], sb_ref[rows, :]

        def norm_rope(a, w, cos=cos, sa=sa, sb=sb):
            a = a * lax.rsqrt(jnp.mean(a * a, axis=-1, keepdims=True) + EPS) * w
            roped = a * cos + pltpu.roll(a, dh - dh // 4, 1) * sa + pltpu.roll(a, dh // 4, 1) * sb
            return a, roped

        q = _dot(hb, w_ref[:, :nq])
        for hh in range(ATT_HEADS):
            _, r = norm_rope(q[:, hh * dh:(hh + 1) * dh], qn_ref[...])
            o_ref[rows, hh * dh:(hh + 1) * dh] = r.astype(BF16)
        kvv = _dot(hb, w_ref[:, nq:])
        k_norm = []
        for hh in range(ATT_KV_HEADS):
            a, r = norm_rope(kvv[:, hh * dh:(hh + 1) * dh], kn_ref[...])
            o_ref[rows, nq + hh * dh:nq + (hh + 1) * dh] = r.astype(BF16)
            k_norm.append(a)
        o_ref[rows, nq + nk:] = kvv[:, nk:].astype(BF16)
        cache_rows.append((rows, k_norm, kvv[:, nk:]))

    @pl.when(is_prompt)
    def _():
        for rows, k_norm, v in cache_rows:
            for hh in range(ATT_KV_HEADS):
                kp_ref[rows, hh * dh:(hh + 1) * dh] = k_norm[hh]
            vp_ref[rows, :] = v


def _rope_tables(l, dh, rotate):
    axis_dim = dh // 2
    half = axis_dim // 2
    t = jnp.arange(l)
    row = (t // GRID_W).astype(F32)
    col = (t % GRID_W).astype(F32)
    inv = ROPE_THETA ** (-jnp.arange(0, axis_dim, 2, dtype=F32) / axis_dim)
    d = jnp.arange(dh)
    pos = jnp.where((d // axis_dim)[None, :] == 0, row[:, None], col[:, None])
    ang = pos * inv[d % half][None, :]
    first = ((d % axis_dim) < half)[None, :]
    cos, sin = jnp.cos(ang), jnp.sin(ang)
    if not rotate:
        cos, sin = jnp.ones_like(cos), jnp.zeros_like(sin)
    return cos, jnp.where(first, -sin, 0.0), jnp.where(first, 0.0, sin)


def _qkv_proj(y, mod, layer, norm_w, w_bf16, qn, kn, dh):
    y_s, y_p = y
    ns, l, d = y_s.shape
    nb = ns + 1
    n = w_bf16.shape[1]
    nk = ATT_KV_HEADS * dh
    tm = _pick(l, (512, 256, 128))
    tabs = [jnp.stack([a, b]) for a, b in zip(_rope_tables(l, dh, True), _rope_tables(l, dh, False))]
    tab_spec = pl.BlockSpec((None, tm, dh), lambda b, i: (jnp.where(b < ns, 0, 1), i, 0))
    prompt_spec = pl.BlockSpec((tm, nk), lambda b, i: (jnp.where(b < ns, 0, i), 0))
    return pl.pallas_call(
        functools.partial(_qkv_kernel, ns=ns, dh=dh),
        out_shape=[jax.ShapeDtypeStruct((nb, l, n), BF16), jax.ShapeDtypeStruct((l, nk), F32),
                   jax.ShapeDtypeStruct((l, nk), F32)],
        grid=(nb, l // tm),
        in_specs=_pair_specs(ns, l // tm, tm, d) + [
            pl.BlockSpec((1, d), lambda b, i: (0, 0)),
            _mod_spec(d, layer, 0),
            _mod_spec(d, layer, 1),
            pl.BlockSpec((d, n), lambda b, i: (0, 0), pipeline_mode=pl.Buffered(1)),
            pl.BlockSpec((1, dh), lambda b, i: (0, 0)),
            pl.BlockSpec((1, dh), lambda b, i: (0, 0)),
            tab_spec, tab_spec, tab_spec,
        ],
        out_specs=[pl.BlockSpec((None, tm, n), lambda b, i: (b, i, 0)), prompt_spec, prompt_spec],
        compiler_params=_params(("arbitrary", "arbitrary")),
        name="qkv_proj",
    )(y_s, y_p, norm_w.reshape(1, d), mod, mod, w_bf16, qn.reshape(1, dh), kn.reshape(1, dh), *tabs)


def _attn_kernel(*refs, dh, group, has_cache, scale):
    if has_cache:
        q_ref, k_ref, v_ref, kc_ref, vc_ref, o_ref, kall, vext = refs
        past = kc_ref.shape[0]
    else:
        q_ref, k_ref, v_ref, o_ref, kall, vext = refs
        past = 0

    @pl.when(pl.program_id(2) == 0)
    def _():
        if has_cache:
            kall[:past, :] = kc_ref[...].astype(BF16)
            vext[:past, :dh] = vc_ref[...].astype(BF16)
        kall[past:, :] = k_ref[...]
        vext[past:, :dh] = v_ref[...]
        vext[:, dh:] = jnp.ones((vext.shape[0], dh), BF16)

    c = scale * 1.4426950408889634
    tq = q_ref.shape[0]
    q = jnp.concatenate([q_ref[:, g * dh:(g + 1) * dh] for g in range(group)], axis=0)
    lk = kall.shape[0]
    ck = _pick(lk, (1152, 512, 256, 128))
    m = None
    for j in range(lk // ck):
        s = _dot_nt(q, kall[j * ck:(j + 1) * ck, :])
        mj = jnp.max(s, axis=-1, keepdims=True)
        vj = vext[j * ck:(j + 1) * ck, :]
        if m is None:
            m = mj
            o = _dot(jnp.exp2((s - m) * c).astype(BF16), vj)
        else:
            m_new = jnp.maximum(m, mj)
            o = o * jnp.exp2((m - m_new) * c) + _dot(jnp.exp2((s - m_new) * c).astype(BF16), vj)
            m = m_new
    for g in range(group):
        og = o[g * tq:(g + 1) * tq]
        o_ref[:, g * dh:(g + 1) * dh] = (og[:, :dh] / og[:, dh:dh + 1]).astype(BF16)


def _attention(qkv, nseq, seq0, ls, dh, cache_k=None, cache_v=None):
    group = ATT_HEADS // ATT_KV_HEADS
    has_cache = cache_k is not None
    tq = _pick(ls, (256, 128))
    kcol = ATT_HEADS
    vcol = ATT_HEADS + ATT_KV_HEADS
    in_specs = [
        pl.BlockSpec((None, tq, group * dh), lambda s, kh, i: (s + seq0, i, kh)),
        pl.BlockSpec((None, ls, dh), lambda s, kh, i: (s + seq0, 0, kcol + kh)),
        pl.BlockSpec((None, ls, dh), lambda s, kh, i: (s + seq0, 0, vcol + kh)),
    ]
    args = [qkv, qkv, qkv]
    past = 0
    if has_cache:
        past = cache_k.shape[1]
        in_specs += [pl.BlockSpec((None, past, dh), lambda s, kh, i: (s, 0, kh))] * 2
        args += [cache_k, cache_v]
    kern = functools.partial(_attn_kernel, dh=dh, group=group, has_cache=has_cache, scale=float(dh) ** -0.5)
    return pl.pallas_call(
        kern,
        out_shape=jax.ShapeDtypeStruct((nseq, ls, ATT_HEADS * dh), BF16),
        grid=(nseq, ATT_KV_HEADS, ls // tq),
        in_specs=in_specs,
        out_specs=pl.BlockSpec((None, tq, group * dh), lambda s, kh, i: (s, i, kh)),
        scratch_shapes=[pltpu.VMEM((past + ls, dh), BF16), pltpu.VMEM((past + ls, 2 * dh), BF16)],
        compiler_params=_params(("arbitrary", "arbitrary", "arbitrary")),
        name="attention_cache" if has_cache else "attention_self",
    )(*args)


def _pack_rows(h):
    half = h.shape[1] // 2
    u = lax.bitcast_convert_type(h.astype(BF16).astype(F32), U32)
    return (u[:, :half] >> 16) | (u[:, half:] & jnp.uint32(0xFFFF0000))


def _unpack_rows(p):
    lo = lax.bitcast_convert_type(p << 16, F32)
    hi = lax.bitcast_convert_type(p & jnp.uint32(0xFFFF0000), F32)
    return jnp.concatenate([lo, hi], axis=1).astype(BF16)


ROW_EXTRA = 128


def _outproj_kernel(os_ref, op_ref, w_ref, ys_ref, yp_ref, g1_ref, nw_ref, sh_ref, sc_ref, rwh_ref, rwl_ref,
                    yos_ref, yop_ref, h_ref, lg_ref, *, ns):
    o = _pair_load(os_ref, op_ref, ns)
    y = _pair_load(ys_ref, yp_ref, ns) + g1_ref[...] * _dot(o, w_ref[...])
    h = _norm_mod(y, nw_ref[...], sh_ref[...], sc_ref[...])
    h_ref[...] = _pack_rows(h)
    h_hi = h.astype(BF16)
    h_lo = (h - h_hi.astype(F32)).astype(BF16)
    rwh = rwh_ref[...]
    lg_ref[...] = _dot_nt(rwh, h_hi) + _dot_nt(rwl_ref[...], h_hi) + _dot_nt(rwh, h_lo)
    _pair_store(yos_ref, yop_ref, ns, y)


def _outproj(o_s, o_p, w_bf16, y, mod, layer, norm_w, rw_hi, rw_lo):
    y_s, y_p = y
    ns, l, d = y_s.shape
    nb = ns + 1
    kdim = o_s.shape[2]
    ne = rw_hi.shape[0]
    tm = _pick(l, (512, 256, 128))
    ni = l // tm
    dp = jax.eval_shape(_pack_rows, jax.ShapeDtypeStruct((8, d), F32)).shape[1]
    outs = pl.pallas_call(
        functools.partial(_outproj_kernel, ns=ns),
        out_shape=[jax.ShapeDtypeStruct(y_s.shape, F32), jax.ShapeDtypeStruct(y_p.shape, F32),
                   jax.ShapeDtypeStruct((nb, l, dp), U32), jax.ShapeDtypeStruct((nb, ne, l), F32)],
        grid=(nb, ni),
        in_specs=_pair_specs(ns, ni, tm, kdim) + [
            pl.BlockSpec((kdim, d), lambda b, i: (0, 0), pipeline_mode=pl.Buffered(1)),
        ] + _pair_specs(ns, ni, tm, d) + [
            _mod_spec(d, layer, 2),
            pl.BlockSpec((1, d), lambda b, i: (0, 0)),
            _mod_spec(d, layer, 3),
            _mod_spec(d, layer, 4),
            pl.BlockSpec((ne, d), lambda b, i: (0, 0)),
            pl.BlockSpec((ne, d), lambda b, i: (0, 0)),
        ],
        out_specs=_pair_specs(ns, ni, tm, d) + [
            pl.BlockSpec((None, tm, dp), lambda b, i: (b, i, 0)),
            pl.BlockSpec((None, ne, tm), lambda b, i: (b, 0, i))],
        compiler_params=_params(("arbitrary", "arbitrary")),
        name="outproj_router",
    )(o_s, o_p, w_bf16, y_s, y_p, mod, norm_w.reshape(1, d), mod, mod, rw_hi, rw_lo)
    return (outs[0], outs[1]), outs[2], outs[3]


def _router_kernel(lg_ref, b_ref, bkt_ref, rank_ref, wt_ref, cnt_ref, carry_ref):
    ng = N_GROUPS
    sc = jax.nn.sigmoid(lg_ref[...])
    sel = sc + b_ref[...]
    sj = [sel[j * ng:(j + 1) * ng] for j in range(EXPERTS_PER_GROUP)]
    pj = [sc[j * ng:(j + 1) * ng] for j in range(EXPERTS_PER_GROUP)]
    hi1, lo1 = jnp.maximum(sj[0], sj[1]), jnp.minimum(sj[0], sj[1])
    hi2, lo2 = jnp.maximum(sj[2], sj[3]), jnp.minimum(sj[2], sj[3])
    top1 = jnp.maximum(hi1, hi2)
    top2 = jnp.maximum(jnp.minimum(hi1, hi2), jnp.maximum(lo1, lo2))
    gs = top1 + top2
    gmax = jnp.max(gs, axis=0, keepdims=True)
    gi = lax.broadcasted_iota(I32, gs.shape, 0)
    best = jnp.min(jnp.where(gs == gmax, gi, ng), axis=0, keepdims=True)
    onehot = gi == best
    neg = -jnp.inf
    vj = [jnp.max(jnp.where(onehot, a, neg), axis=0, keepdims=True) for a in sj]
    wj = [jnp.max(jnp.where(onehot, a, neg), axis=0, keepdims=True) for a in pj]

    def argmax4(vals):
        m, i = vals[0], jnp.zeros(vals[0].shape, I32)
        for j in range(1, EXPERTS_PER_GROUP):
            gt = vals[j] > m
            m = jnp.where(gt, vals[j], m)
            i = jnp.where(gt, j, i)
        return i

    i1 = argmax4(vj)
    i2 = argmax4([jnp.where(i1 == j, neg, vj[j]) for j in range(EXPERTS_PER_GROUP)])

    def take(vals, i):
        out = vals[0]
        for j in range(1, EXPERTS_PER_GROUP):
            out = jnp.where(i == j, vals[j], out)
        return out

    w1, w2 = take(wj, i1), take(wj, i2)
    tot = w1 + w2
    w1, w2 = w1 / tot, w2 / tot

    lo, hi = jnp.minimum(i1, i2), jnp.maximum(i1, i2)
    pair = jnp.where(lo == 0, hi - 1, jnp.where(lo == 1, jnp.where(hi == 3, 3, 4), 5))
    slot_a = jnp.where(pair < 3, 0, jnp.where(pair < 5, 1, 3))
    bucket = best * N_PAIRS + pair
    a_first = i1 == slot_a
    w_a = jnp.where(a_first, w1, w2)
    w_b = jnp.where(a_first, w2, w1)

    tl = bucket.shape[1]
    nbk = carry_ref.shape[0]

    @pl.when(jnp.logical_and(pl.program_id(0) == 0, pl.program_id(1) == 0))
    def _():
        carry_ref[...] = jnp.zeros_like(carry_ref)

    onehot_b = lax.broadcasted_iota(I32, (nbk, tl), 0) == bucket
    upper = (lax.broadcasted_iota(I32, (tl, tl), 0) <= lax.broadcasted_iota(I32, (tl, tl), 1))
    prefix = _dot(onehot_b.astype(BF16), upper.astype(BF16))
    carry = carry_ref[...]
    rank = jnp.sum(jnp.where(onehot_b, prefix - 1.0 + carry, 0.0), axis=0, keepdims=True)
    carry = carry + prefix[:, tl - 1:tl]
    carry_ref[...] = carry
    bkt_ref[...] = bucket
    rank_ref[...] = rank.astype(I32)
    cnt_ref[...] = jnp.broadcast_to(carry, cnt_ref.shape)
    rows = lax.broadcasted_iota(I32, (wt_ref.shape[1], tl), 0)
    wt_ref[...] = jnp.where(rows == 0, w_a, jnp.where(rows == 1, w_b, 0.0)).T


def _router(logits, bias_perm):
    nb, ne, l = logits.shape
    tl = _pick(l, (512, 256, 128))
    nbk = N_GROUPS * N_PAIRS
    return pl.pallas_call(
        _router_kernel,
        out_shape=[jax.ShapeDtypeStruct((nb, 1, l), I32), jax.ShapeDtypeStruct((nb, 1, l), I32),
                   jax.ShapeDtypeStruct((nb, l, ROW_EXTRA), F32), jax.ShapeDtypeStruct((nbk, 128), F32)],
        grid=(nb, l // tl),
        in_specs=[pl.BlockSpec((None, ne, tl), lambda b, i: (b, 0, i)),
                  pl.BlockSpec((ne, 1), lambda b, i: (0, 0))],
        out_specs=[pl.BlockSpec((None, 1, tl), lambda b, i: (b, 0, i)),
                   pl.BlockSpec((None, 1, tl), lambda b, i: (b, 0, i)),
                   pl.BlockSpec((None, tl, ROW_EXTRA), lambda b, i: (b, i, 0)),
                   pl.BlockSpec((nbk, 128), lambda b, i: (0, 0))],
        scratch_shapes=[pltpu.VMEM((nbk, 1), F32)],
        compiler_params=_params(("arbitrary", "arbitrary")),
        name="router_rank",
    )(logits, bias_perm.reshape(ne, 1))


def _moe_kernel(ea_ref, eb_ref, nu_ref, x_ref, wga_ref, wua_ref, wda_ref, wgb_ref, wub_ref, wdb_ref, o_ref):
    del ea_ref, eb_ref
    i = pl.program_id(0)

    dq = o_ref.shape[1]

    @pl.when(i < nu_ref[0])
    def _():
        x = _unpack_rows(x_ref[:, :dq])
        w = lax.bitcast_convert_type(x_ref[:, dq:], F32)

        def expert(wg, wu, wd):
            hid = jax.nn.silu(_dot(x, wg[...])) * _dot(x, wu[...])
            return _dot(hid.astype(BF16), wd[...])

        o_ref[...] = _pack_rows(w[:, 0:1] * expert(wga_ref, wua_ref, wda_ref)
                                + w[:, 1:2] * expert(wgb_ref, wub_ref, wdb_ref))

    @pl.when(i >= nu_ref[0])
    def _():
        o_ref[...] = jnp.zeros_like(o_ref)


def _moe_experts(xs, blk_a, blk_b, n_used, wg, wu, wd, layer, bm):
    p, dp = xs.shape
    _, _, d, de = wg.shape
    nblk = p // bm
    dq = dp - ROW_EXTRA
    w_in = lambda sel: pl.BlockSpec((None, None, d, de), lambda i, ea, eb, nu: (layer, (ea, eb)[sel][i], 0, 0))
    w_out = lambda sel: pl.BlockSpec((None, None, de, d), lambda i, ea, eb, nu: (layer, (ea, eb)[sel][i], 0, 0))
    grid_spec = pltpu.PrefetchScalarGridSpec(
        num_scalar_prefetch=3,
        grid=(nblk,),
        in_specs=[pl.BlockSpec((bm, dp), lambda i, ea, eb, nu: (i, 0)),
                  w_in(0), w_in(0), w_out(0), w_in(1), w_in(1), w_out(1)],
        out_specs=pl.BlockSpec((bm, dq), lambda i, ea, eb, nu: (i, 0)),
    )
    return pl.pallas_call(
        _moe_kernel,
        out_shape=jax.ShapeDtypeStruct((p, dq), U32),
        grid_spec=grid_spec,
        compiler_params=_params(("arbitrary",)),
        name="moe_experts",
    )(blk_a, blk_b, n_used, xs, wg, wu, wd, wg, wu, wd)


def _row_plan(bucket, rank, counts, bm):
    t = bucket.size
    nbk = N_GROUPS * N_PAIRS
    cnt = counts[:, 0].astype(I32)
    padded = (cnt + bm - 1) // bm * bm
    pends = jnp.cumsum(padded)
    pstarts = pends - padded
    onehot = bucket.reshape(t, 1) == jnp.arange(nbk, dtype=I32)[None, :]
    dest = jnp.sum(jnp.where(onehot, pstarts[None, :], 0), axis=1) + rank.reshape(t)
    p = (t + bm - 1) // bm * bm + nbk * bm
    nblk = p // bm
    blk_bucket = jnp.clip(jnp.searchsorted(pends, jnp.arange(nblk, dtype=I32) * bm, side='right'), 0, nbk - 1)
    grp, pair = blk_bucket // N_PAIRS, blk_bucket % N_PAIRS
    blk_a = grp * EXPERTS_PER_GROUP + jnp.asarray(_PAIR_A, I32)[pair]
    blk_b = grp * EXPERTS_PER_GROUP + jnp.asarray(_PAIR_B, I32)[pair]
    n_used = (pends[-1:] // bm).astype(I32)
    return dest.astype(I32), blk_a.astype(I32), blk_b.astype(I32), n_used, p


def _scatter_kernel(dest_ref, src_ref, wt_ref, init_hbm, out_hbm, rowbuf, sem, *, rows):
    del init_hbm
    base = pl.program_id(0) * rows
    dp = src_ref.shape[1]
    rowbuf[:, :dp] = src_ref[...]
    rowbuf[:, dp:] = lax.bitcast_convert_type(wt_ref[...], U32)
    for r in range(rows):
        pltpu.make_async_copy(rowbuf.at[r], out_hbm.at[dest_ref[base + r]], sem).start(priority=r % 2)
    pltpu.make_async_copy(rowbuf, out_hbm.at[pl.ds(0, rows)], sem).wait()


def _scatter_rows(h_rows, wt_rows, dest, p):
    t, dp = h_rows.shape
    rows = _pick(t, (512, 256, 128))
    grid_spec = pltpu.PrefetchScalarGridSpec(
        num_scalar_prefetch=1,
        grid=(t // rows,),
        in_specs=[pl.BlockSpec((rows, dp), lambda i, dst: (i, 0)),
                  pl.BlockSpec((rows, ROW_EXTRA), lambda i, dst: (i, 0)),
                  pl.BlockSpec(memory_space=pl.ANY)],
        out_specs=pl.BlockSpec(memory_space=pl.ANY),
        scratch_shapes=[pltpu.VMEM((rows, dp + ROW_EXTRA), U32), pltpu.SemaphoreType.DMA],
    )
    return pl.pallas_call(
        functools.partial(_scatter_kernel, rows=rows),
        out_shape=jax.ShapeDtypeStruct((p, dp + ROW_EXTRA), U32),
        grid_spec=grid_spec,
        input_output_aliases={3: 0},
        compiler_params=_params(("arbitrary",)),
        name="moe_scatter_rows",
    )(dest, h_rows, wt_rows, jnp.zeros((p, dp + ROW_EXTRA), U32))


def _combine_kernel(dest_ref, ys_ref, yp_ref, g_ref, m_hbm, os_ref, op_ref, mbuf0, mbuf1, sem,
                    *, ns, th, nsteps):
    step = pl.program_id(0) * pl.num_programs(1) + pl.program_id(1)
    bufs = (mbuf0, mbuf1)

    def gather(first_row, k):
        for r in range(th):
            pltpu.make_async_copy(m_hbm.at[dest_ref[first_row + r]], bufs[k].at[r],
                                  sem.at[k]).start(priority=r % 2)

    def wait(k):
        pltpu.make_async_copy(m_hbm.at[pl.ds(0, th)], bufs[k], sem.at[k]).wait()

    @pl.when(step == 0)
    def _():
        gather(0, 0)

    is_prompt = pl.program_id(0) >= ns
    g = g_ref[...]
    row0 = step * (2 * th)
    for k in range(2):
        rows = slice(k * th, (k + 1) * th)
        wait(k)
        nxt = jnp.minimum(row0 + (k + 1) * th, (nsteps * 2 - 1) * th)
        gather(nxt, 1 - k)
        y = jnp.where(is_prompt, yp_ref[rows, :], ys_ref[rows, :])
        val = y + g * _unpack_rows(bufs[k][...]).astype(F32)

        @pl.when(jnp.logical_not(is_prompt))
        def _():
            os_ref[rows, :] = val

        @pl.when(is_prompt)
        def _():
            op_ref[rows, :] = val

    @pl.when(step == nsteps - 1)
    def _():
        wait(0)


def _combine(y, mod, layer, moe_rows, dest):
    y_s, y_p = y
    ns, l, d = y_s.shape
    nb = ns + 1
    th = _pick(l // 2, (512, 256, 128, 64))
    tm = 2 * th
    ni = l // tm
    grid_spec = pltpu.PrefetchScalarGridSpec(
        num_scalar_prefetch=1,
        grid=(nb, ni),
        in_specs=_pair_specs(ns, ni, tm, d) + [
            pl.BlockSpec((None, None, None, 1, d), lambda b, i, dst: (layer, 5, b, 0, 0)),
            pl.BlockSpec(memory_space=pl.ANY),
        ],
        out_specs=_pair_specs(ns, ni, tm, d),
        scratch_shapes=[pltpu.VMEM((th, moe_rows.shape[1]), moe_rows.dtype),
                        pltpu.VMEM((th, moe_rows.shape[1]), moe_rows.dtype),
                        pltpu.SemaphoreType.DMA((2,))],
    )
    outs = pl.pallas_call(
        functools.partial(_combine_kernel, ns=ns, th=th, nsteps=nb * ni),
        out_shape=[jax.ShapeDtypeStruct(y_s.shape, F32), jax.ShapeDtypeStruct(y_p.shape, F32)],
        grid_spec=grid_spec,
        compiler_params=_params(("arbitrary", "arbitrary")),
        name="moe_combine",
    )(dest, y_s, y_p, mod, moe_rows)
    return (outs[0], outs[1])


def kernel(x_prompt, x_sample, c, state_ret_fwd, state_ret_bwd, cache_attn_k, cache_attn_v, c_ctx,
           ada_w, ada_b, norm1_w, norm2_w, ret_w_in, ret_w_out, ret_gn_w, ret_decay_fwd, ret_decay_bwd,
           attn_w_qkv, attn_w_o, attn_q_norm, attn_k_norm, router_w, router_b,
           moe_w_gate, moe_w_up, moe_w_down):
    n_prompt, seq, d = x_prompt.shape
    ns, l, _ = x_sample.shape
    assert n_prompt * seq == l, "prompt tokens must fill exactly one sample-length row"
    depth = ada_w.shape[0]
    nb = ns + 1
    dk = d // RET_HEADS
    dv = 2 * dk
    dh = d // ATT_HEADS
    bm = MOE_ROW_BLOCK

    y = (x_sample, x_prompt.reshape(1, l, d))
    nbp = (nb + 7) // 8 * 8
    cond = jnp.zeros((nbp, d), F32).at[:ns].set(c).at[ns].set(c_ctx)
    mod = _ada_params(cond, ada_w, ada_b)

    perm = (jnp.arange(N_EXPERTS) % N_GROUPS) * EXPERTS_PER_GROUP + jnp.arange(N_EXPERTS) // N_GROUPS
    rw_t = router_w.T[perm]
    rw_hi = rw_t.astype(BF16)
    rw_lo = (rw_t - rw_hi.astype(F32)).astype(BF16)
    rb_perm = router_b[perm]
    wg_bf16, wu_bf16, wd_bf16 = moe_w_gate.astype(BF16), moe_w_up.astype(BF16), moe_w_down.astype(BF16)

    ret_f, ret_b, k_new, v_new = [], [], [], []
    for i in range(depth):
        if i % 2 == 0:
            r = i // 2
            qkvg = _ret_inproj(y, mod, i, norm1_w[i], ret_w_in[r].astype(BF16), ret_gn_w[r], dk)
            o_s = _retention(qkvg, ns, 0, l, dk, ret_decay_fwd[r], ret_decay_bwd[r],
                             s0f=state_ret_fwd[:, r], s0b=state_ret_bwd[:, r])
            o_p, s_f, s_b = _retention(qkvg, n_prompt, ns, seq, dk,
                                       ret_decay_fwd[r], ret_decay_bwd[r], emit_final=True)
            o_s = o_s[0] if isinstance(o_s, (list, tuple)) else o_s
            ret_f.append(s_f)
            ret_b.append(s_b)
            w_o = ret_w_out[r].astype(BF16)
        else:
            a = i // 2
            qkv, k_p, v_p = _qkv_proj(y, mod, i, norm1_w[i], attn_w_qkv[a].astype(BF16),
                                      attn_q_norm[a], attn_k_norm[a], dh)
            past = cache_attn_k.shape[2]
            o_s = _attention(qkv, ns, 0, l, dh,
                             cache_k=cache_attn_k[:, a].reshape(ns, past, ATT_KV_HEADS * dh),
                             cache_v=cache_attn_v[:, a].reshape(ns, past, ATT_KV_HEADS * dh))
            o_p = _attention(qkv.reshape(nb * n_prompt, seq, qkv.shape[2]), n_prompt, ns * n_prompt, seq, dh)
            k_new.append(k_p.reshape(n_prompt, seq, ATT_KV_HEADS, dh))
            v_new.append(v_p.reshape(n_prompt, seq, ATT_KV_HEADS, dh))
            w_o = attn_w_o[a].astype(BF16)
        o_p = o_p.reshape(1, l, o_p.shape[-1])
        y, h2, logits = _outproj(o_s, o_p, w_o, y, mod, i, norm2_w[i], rw_hi, rw_lo)
        bucket, rank, wt, counts = _router(logits, rb_perm)
        dest, blk_a, blk_b, n_used, p = _row_plan(bucket, rank, counts, bm)
        xs = _scatter_rows(h2.reshape(nb * l, h2.shape[2]), wt.reshape(nb * l, wt.shape[2]), dest, p)
        moe_rows = _moe_experts(xs, blk_a, blk_b, n_used, wg_bf16, wu_bf16, wd_bf16, i, bm)
        y = _combine(y, mod, i, moe_rows, dest)

    y_s, y_p = y
    return (y_p.reshape(n_prompt, seq, d), y_s, jnp.stack(ret_f, axis=1), jnp.stack(ret_b, axis=1),
            jnp.stack(k_new, axis=1), jnp.stack(v_new, axis=1))
```

```python
import functools

import jax
import jax.numpy as jnp
from jax import lax
from jax.experimental import pallas as pl
from jax.experimental.pallas import tpu as pltpu

F32 = jnp.float32
BF16 = jnp.bfloat16
I32 = jnp.int32
U32 = jnp.uint32

EPS = 1e-6
GRID_W = 64
RET_HEADS = 4
ATT_HEADS = 8
ATT_KV_HEADS = 2
ROPE_THETA = 10000.0
N_EXPERTS = 32
N_GROUPS = 8
EXPERTS_PER_GROUP = N_EXPERTS // N_GROUPS
TOP_K = 2
_PAIR_A = (0, 0, 0, 1, 1, 3)
_PAIR_B = (1, 2, 3, 3, 2, 2)
N_PAIRS = len(_PAIR_A)
MOE_ROW_BLOCK = 256

V7X_VMEM_BYTES = 64 * 1024 * 1024
VMEM_LIMIT_BYTES = 56 * 1024 * 1024


def _params(sem):
    return pltpu.CompilerParams(dimension_semantics=sem, vmem_limit_bytes=VMEM_LIMIT_BYTES)


def _dot(a, b):
    return jnp.dot(a, b, preferred_element_type=F32)


def _dot_nt(a, b):
    return lax.dot_general(a, b, (((1,), (1,)), ((), ())), preferred_element_type=F32)


def _dot_tn(a, b):
    return lax.dot_general(a, b, (((0,), (0,)), ((), ())), preferred_element_type=F32)


def _pick(n, prefs):
    for p in prefs:
        if n % p == 0:
            return p
    return n


def _ada_kernel(c_ref, w_ref, b_ref, o_ref):
    c = c_ref[...]
    s = jax.nn.silu(c).astype(BF16)
    o_ref[...] = _dot(s, w_ref[...].astype(BF16)) + b_ref[...]


def _ada_params(cond, ada_w, ada_b):
    depth, d, n6 = ada_w.shape
    nbp = cond.shape[0]
    tn = _pick(n6, (1536, 1024, 512, 256, 128))
    out = pl.pallas_call(
        _ada_kernel,
        out_shape=jax.ShapeDtypeStruct((depth, nbp, n6), F32),
        grid=(depth, n6 // tn),
        in_specs=[
            pl.BlockSpec((nbp, d), lambda l, j: (0, 0)),
            pl.BlockSpec((None, d, tn), lambda l, j: (l, 0, j)),
            pl.BlockSpec((None, 1, tn), lambda l, j: (l, 0, j)),
        ],
        out_specs=pl.BlockSpec((None, nbp, tn), lambda l, j: (l, 0, j)),
        compiler_params=_params(("arbitrary", "arbitrary")),
        name="ada_params",
    )(cond, ada_w, ada_b.reshape(depth, 1, n6))
    return out.reshape(depth, nbp, 6, 1, d).transpose(0, 2, 1, 3, 4)


def _mod_spec(d, layer, j):
    return pl.BlockSpec((None, None, None, 1, d), lambda b, i: (layer, j, b, 0, 0))


def _pair_specs(ns, ni, tm, w):
    return [pl.BlockSpec((None, tm, w), lambda b, i, *_: (jnp.minimum(b, ns - 1), jnp.where(b < ns, i, ni - 1), 0)),
            pl.BlockSpec((None, tm, w), lambda b, i, *_: (0, jnp.where(b < ns, 0, i), 0))]


def _pair_load(s_ref, p_ref, ns):
    return jnp.where(pl.program_id(0) >= ns, p_ref[...], s_ref[...])


def _pair_store(s_ref, p_ref, ns, val):
    @pl.when(pl.program_id(0) < ns)
    def _():
        s_ref[...] = val

    @pl.when(pl.program_id(0) >= ns)
    def _():
        p_ref[...] = val


def _norm_mod(x, nw, sh, sc):
    ms = jnp.mean(x * x, axis=-1, keepdims=True)
    h = x * lax.rsqrt(ms + EPS) * nw
    return h * (1.0 + sc) + sh


def _ret_inproj_kernel(xs_ref, xp_ref, nw_ref, sh_ref, sc_ref, w_ref, gn_ref, o_ref, *, ns, dk, tn, k_scale):
    x = _pair_load(xs_ref, xp_ref, ns)
    hb = _norm_mod(x, nw_ref[...], sh_ref[...], sc_ref[...]).astype(BF16)
    dv = 2 * dk
    hq, hv = RET_HEADS * dk, RET_HEADS * dv
    segments = ((0, dk, 0), (hq, dk, dk), (2 * hq, dv, 2 * dk), (2 * hq + hv, dv, 2 * dk + dv),
                (2 * hq + 2 * hv, dv, 2 * dk + 2 * dv))
    n = w_ref.shape[1]
    for j in range(n // tn):
        lo = j * tn
        seg = max(s for s in range(len(segments)) if segments[s][0] <= lo)
        start, width, off = segments[seg]
        acc = _dot(hb, w_ref[:, lo:lo + tn])
        if seg == 1:
            acc = acc * k_scale
        elif seg >= 3:
            acc = jax.nn.silu(acc) * gn_ref[:, lo - start:lo - start + tn]
        acc = acc.astype(BF16)
        head0 = (lo - start) // width
        for hh in range(tn // width):
            o_ref[head0 + hh, :, off:off + width] = acc[:, hh * width:(hh + 1) * width]


def _ret_inproj(y, mod, layer, norm_w, w_in_bf16, gn_w, dk):
    y_s, y_p = y
    ns, l, d = y_s.shape
    nb = ns + 1
    n = w_in_bf16.shape[1]
    hq = RET_HEADS * dk
    hv = 2 * hq
    tm = _pick(l, (512, 256, 128))
    tn = _pick(hq, (1024, 512, 256, 128))
    wph = n // RET_HEADS
    kern = functools.partial(_ret_inproj_kernel, ns=ns, dk=dk, tn=tn, k_scale=float(dk) ** -0.5)
    return pl.pallas_call(
        kern,
        out_shape=jax.ShapeDtypeStruct((nb, RET_HEADS, l, wph), BF16),
        grid=(nb, l // tm),
        in_specs=_pair_specs(ns, l // tm, tm, d) + [
            pl.BlockSpec((1, d), lambda b, i: (0, 0)),
            _mod_spec(d, layer, 0),
            _mod_spec(d, layer, 1),
            pl.BlockSpec((d, n), lambda b, i: (0, 0), pipeline_mode=pl.Buffered(1)),
            pl.BlockSpec((1, hv), lambda b, i: (0, 0)),
        ],
        out_specs=pl.BlockSpec((None, RET_HEADS, tm, wph), lambda b, i: (b, 0, i, 0)),
        compiler_params=_params(("arbitrary", "arbitrary")),
        name="ret_inproj",
    )(y_s, y_p, norm_w.reshape(1, d), mod, mod, w_in_bf16, gn_w.reshape(1, hv))


def _log_sigmoid(x):
    return jnp.minimum(x, 0.0) - jnp.log1p(jnp.exp(-jnp.abs(x)))


def _ret_kernel(*refs, ls, c, dk, has_init, emit_final):
    it = iter(refs)
    x_ref, df_ref, db_ref = (next(it) for _ in range(3))
    if has_init:
        s0f_ref, s0b_ref = next(it), next(it)
    y_ref = next(it)
    if emit_final:
        sf_out, sb_out = next(it), next(it)
    s_scr, sb_scr = next(it), next(it)

    dv = 2 * dk
    q_ref = x_ref.at[:, 0:dk]
    k_ref = x_ref.at[:, dk:2 * dk]
    v_ref = x_ref.at[:, 2 * dk:2 * dk + dv]
    gf_ref = x_ref.at[:, 2 * dk + dv:2 * dk + 2 * dv]
    gb_ref = x_ref.at[:, 2 * dk + 2 * dv:2 * dk + 3 * dv]
    nchunks = ls // c
    lg_f = _log_sigmoid(df_ref[...])[:, 0:1]
    lg_b = _log_sigmoid(db_ref[...])[:, 0:1]

    ri = lax.broadcasted_iota(I32, (c, c), 0)
    ci = lax.broadcasted_iota(I32, (c, c), 1)
    dif = (ri - ci).astype(F32)
    decay_f = jnp.where(dif >= 0, jnp.exp(jnp.maximum(dif, 0.0) * lg_f), 0.0).astype(BF16)
    decay_b = jnp.where(dif <= 0, jnp.exp(jnp.maximum(-dif, 0.0) * lg_b), 0.0).astype(BF16)
    pos = lax.broadcasted_iota(I32, (c, dk), 0).astype(F32)
    qdec_f = jnp.exp((pos + 1.0) * lg_f).astype(BF16)
    kdec_f = jnp.exp((c - 1.0 - pos) * lg_f).astype(BF16)
    qdec_b = jnp.exp((c - pos) * lg_b).astype(BF16)
    kdec_b = jnp.exp(pos * lg_b).astype(BF16)
    cdec_f = jnp.exp(c * lg_f)
    cdec_b = jnp.exp(c * lg_b)

    if has_init:
        s_scr[...] = s0b_ref[...]
    else:
        s_scr[...] = jnp.zeros_like(s_scr)

    def bwd_body(t, carry):
        n = nchunks - 1 - t
        r0 = pl.multiple_of(n * c, c)
        s = s_scr[...]
        sb_scr[n] = s.astype(BF16)
        kc = k_ref[pl.ds(r0, c), :]
        vc = v_ref[pl.ds(r0, c), :]
        s_scr[...] = s * cdec_b + _dot_tn(kc * kdec_b, vc)
        return carry

    lax.fori_loop(0, nchunks, bwd_body, 0, unroll=2 if nchunks % 2 == 0 else 1)
    if emit_final:
        sb_out[...] = s_scr[...]

    if has_init:
        s_scr[...] = s0f_ref[...]
    else:
        s_scr[...] = jnp.zeros_like(s_scr)

    def head_norm(o):
        mu = jnp.mean(o, axis=-1, keepdims=True)
        dlt = o - mu
        var = jnp.mean(dlt * dlt, axis=-1, keepdims=True)
        return dlt * lax.rsqrt(var + EPS)

    def fwd_body(n, carry):
        r0 = pl.multiple_of(n * c, c)
        rows = pl.ds(r0, c)
        qc = q_ref[rows, :]
        kc = k_ref[rows, :]
        vc = v_ref[rows, :]
        sc = _dot_nt(qc, kc).astype(BF16)
        s = s_scr[...]
        o_f = _dot(sc * decay_f, vc) + _dot(qc * qdec_f, s.astype(BF16))
        o_b = _dot(sc * decay_b, vc) + _dot(qc * qdec_b, sb_scr[n])
        s_scr[...] = s * cdec_f + _dot_tn(kc * kdec_f, vc)
        y_ref[rows, :] = (head_norm(o_f).astype(BF16) * gf_ref[rows, :]
                          + head_norm(o_b).astype(BF16) * gb_ref[rows, :])
        return carry

    lax.fori_loop(0, nchunks, fwd_body, 0, unroll=2 if nchunks % 2 == 0 else 1)
    if emit_final:
        sf_out[...] = s_scr[...]


def _retention(qkvg, nseq, row0, ls, dk, decay_f, decay_b, s0f=None, s0b=None, emit_final=False):
    h = RET_HEADS
    dv = 2 * dk
    c = _pick(ls, (256, 128))
    has_init = s0f is not None
    spr = qkvg.shape[2] // ls
    in_specs = [
        pl.BlockSpec((None, None, ls, qkvg.shape[3]), lambda s, hh: (row0 + s // spr, hh, s % spr, 0)),
        pl.BlockSpec((None, 1, 128), lambda s, hh: (hh, 0, 0)),
        pl.BlockSpec((None, 1, 128), lambda s, hh: (hh, 0, 0)),
    ]
    args = [qkvg,
            jnp.broadcast_to(decay_f.reshape(h, 1, 1), (h, 1, 128)),
            jnp.broadcast_to(decay_b.reshape(h, 1, 1), (h, 1, 128))]
    if has_init:
        in_specs += [pl.BlockSpec((None, None, dk, dv), lambda s, hh: (s, hh, 0, 0))] * 2
        args += [s0f, s0b]
    out_shape = [jax.ShapeDtypeStruct((nseq, ls, h * dv), BF16)]
    out_specs = [pl.BlockSpec((None, ls, dv), lambda s, hh: (s, 0, hh))]
    if emit_final:
        out_shape += [jax.ShapeDtypeStruct((nseq, h, dk, dv), F32)] * 2
        out_specs += [pl.BlockSpec((None, None, dk, dv), lambda s, hh: (s, hh, 0, 0))] * 2
    kern = functools.partial(_ret_kernel, ls=ls, c=c, dk=dk, has_init=has_init, emit_final=emit_final)
    return pl.pallas_call(
        kern,
        out_shape=out_shape,
        grid=(nseq, h),
        in_specs=in_specs,
        out_specs=out_specs,
        scratch_shapes=[pltpu.VMEM((dk, dv), F32), pltpu.VMEM((ls // c, dk, dv), BF16)],
        compiler_params=_params(("arbitrary", "arbitrary")),
        name="retention_init" if has_init else "retention_zero",
    )(*args)


def _qkv_kernel(xs_ref, xp_ref, nw_ref, sh_ref, sc_ref, w_ref, qn_ref, kn_ref, cos_ref, sa_ref, sb_ref,
                o_ref, kp_ref, vp_ref, *, ns, dh):
    nq = ATT_HEADS * dh
    nk = ATT_KV_HEADS * dh
    is_prompt = pl.program_id(0) >= ns
    tm = xs_ref.shape[0]
    nsub = 2 if tm % 256 == 0 else 1
    ts = tm // nsub
    cache_rows = []
    for sub in range(nsub):
        rows = slice(sub * ts, (sub + 1) * ts)
        x = jnp.where(is_prompt, xp_ref[rows, :], xs_ref[rows, :])
        hb = _norm_mod(x, nw_ref[...], sh_ref[...], sc_ref[...]).astype(BF16)
        cos, sa, sb = cos_ref[rows, :], sa_ref[rows, :], sb_ref[rows, :]

        def norm_rope(a, w, cos=cos, sa=sa, sb=sb):
            a = a * lax.rsqrt(jnp.mean(a * a, axis=-1, keepdims=True) + EPS) * w
            roped = a * cos + pltpu.roll(a, dh - dh // 4, 1) * sa + pltpu.roll(a, dh // 4, 1) * sb
            return a, roped

        q = _dot(hb, w_ref[:, :nq])
        for hh in range(ATT_HEADS):
            _, r = norm_rope(q[:, hh * dh:(hh + 1) * dh], qn_ref[...])
            o_ref[rows, hh * dh:(hh + 1) * dh] = r.astype(BF16)
        kvv = _dot(hb, w_ref[:, nq:])
        k_norm = []
        for hh in range(ATT_KV_HEADS):
            a, r = norm_rope(kvv[:, hh * dh:(hh + 1) * dh], kn_ref[...])
            o_ref[rows, nq + hh * dh:nq + (hh + 1) * dh] = r.astype(BF16)
            k_norm.append(a)
        o_ref[rows, nq + nk:] = kvv[:, nk:].astype(BF16)
        cache_rows.append((rows, k_norm, kvv[:, nk:]))

    @pl.when(is_prompt)
    def _():
        for rows, k_norm, v in cache_rows:
            for hh in range(ATT_KV_HEADS):
                kp_ref[rows, hh * dh:(hh + 1) * dh] = k_norm[hh]
            vp_ref[rows, :] = v


def _rope_tables(l, dh, rotate):
    axis_dim = dh // 2
    half = axis_dim // 2
    t = jnp.arange(l)
    row = (t // GRID_W).astype(F32)
    col = (t % GRID_W).astype(F32)
    inv = ROPE_THETA ** (-jnp.arange(0, axis_dim, 2, dtype=F32) / axis_dim)
    d = jnp.arange(dh)
    pos = jnp.where((d // axis_dim)[None, :] == 0, row[:, None], col[:, None])
    ang = pos * inv[d % half][None, :]
    first = ((d % axis_dim) < half)[None, :]
    cos, sin = jnp.cos(ang), jnp.sin(ang)
    if not rotate:
        cos, sin = jnp.ones_like(cos), jnp.zeros_like(sin)
    return cos, jnp.where(first, -sin, 0.0), jnp.where(first, 0.0, sin)


def _qkv_proj(y, mod, layer, norm_w, w_bf16, qn, kn, dh):
    y_s, y_p = y
    ns, l, d = y_s.shape
    nb = ns + 1
    n = w_bf16.shape[1]
    nk = ATT_KV_HEADS * dh
    tm = _pick(l, (512, 256, 128))
    tabs = [jnp.stack([a, b]) for a, b in zip(_rope_tables(l, dh, True), _rope_tables(l, dh, False))]
    tab_spec = pl.BlockSpec((None, tm, dh), lambda b, i: (jnp.where(b < ns, 0, 1), i, 0))
    prompt_spec = pl.BlockSpec((tm, nk), lambda b, i: (jnp.where(b < ns, 0, i), 0))
    return pl.pallas_call(
        functools.partial(_qkv_kernel, ns=ns, dh=dh),
        out_shape=[jax.ShapeDtypeStruct((nb, l, n), BF16), jax.ShapeDtypeStruct((l, nk), F32),
                   jax.ShapeDtypeStruct((l, nk), F32)],
        grid=(nb, l // tm),
        in_specs=_pair_specs(ns, l // tm, tm, d) + [
            pl.BlockSpec((1, d), lambda b, i: (0, 0)),
            _mod_spec(d, layer, 0),
            _mod_spec(d, layer, 1),
            pl.BlockSpec((d, n), lambda b, i: (0, 0), pipeline_mode=pl.Buffered(1)),
            pl.BlockSpec((1, dh), lambda b, i: (0, 0)),
            pl.BlockSpec((1, dh), lambda b, i: (0, 0)),
            tab_spec, tab_spec, tab_spec,
        ],
        out_specs=[pl.BlockSpec((None, tm, n), lambda b, i: (b, i, 0)), prompt_spec, prompt_spec],
        compiler_params=_params(("arbitrary", "arbitrary")),
        name="qkv_proj",
    )(y_s, y_p, norm_w.reshape(1, d), mod, mod, w_bf16, qn.reshape(1, dh), kn.reshape(1, dh), *tabs)


def _attn_kernel(*refs, dh, group, has_cache, scale):
    if has_cache:
        q_ref, k_ref, v_ref, kc_ref, vc_ref, o_ref, kall, vext = refs
        past = kc_ref.shape[0]
    else:
        q_ref, k_ref, v_ref, o_ref, kall, vext = refs
        past = 0

    @pl.when(pl.program_id(2) == 0)
    def _():
        if has_cache:
            kall[:past, :] = kc_ref[...].astype(BF16)
            vext[:past, :dh] = vc_ref[...].astype(BF16)
        kall[past:, :] = k_ref[...]
        vext[past:, :dh] = v_ref[...]
        vext[:, dh:] = jnp.ones((vext.shape[0], dh), BF16)

    c = scale * 1.4426950408889634
    tq = q_ref.shape[0]
    q = jnp.concatenate([q_ref[:, g * dh:(g + 1) * dh] for g in range(group)], axis=0)
    lk = kall.shape[0]
    ck = _pick(lk, (1152, 512, 256, 128))
    m = None
    for j in range(lk // ck):
        s = _dot_nt(q, kall[j * ck:(j + 1) * ck, :])
        mj = jnp.max(s, axis=-1, keepdims=True)
        vj = vext[j * ck:(j + 1) * ck, :]
        if m is None:
            m = mj
            o = _dot(jnp.exp2((s - m) * c).astype(BF16), vj)
        else:
            m_new = jnp.maximum(m, mj)
            o = o * jnp.exp2((m - m_new) * c) + _dot(jnp.exp2((s - m_new) * c).astype(BF16), vj)
            m = m_new
    for g in range(group):
        og = o[g * tq:(g + 1) * tq]
        o_ref[:, g * dh:(g + 1) * dh] = (og[:, :dh] / og[:, dh:dh + 1]).astype(BF16)


def _attention(qkv, nseq, seq0, ls, dh, cache_k=None, cache_v=None):
    group = ATT_HEADS // ATT_KV_HEADS
    has_cache = cache_k is not None
    tq = _pick(ls, (256, 128))
    kcol = ATT_HEADS
    vcol = ATT_HEADS + ATT_KV_HEADS
    in_specs = [
        pl.BlockSpec((None, tq, group * dh), lambda s, kh, i: (s + seq0, i, kh)),
        pl.BlockSpec((None, ls, dh), lambda s, kh, i: (s + seq0, 0, kcol + kh)),
        pl.BlockSpec((None, ls, dh), lambda s, kh, i: (s + seq0, 0, vcol + kh)),
    ]
    args = [qkv, qkv, qkv]
    past = 0
    if has_cache:
        past = cache_k.shape[1]
        in_specs += [pl.BlockSpec((None, past, dh), lambda s, kh, i: (s, 0, kh))] * 2
        args += [cache_k, cache_v]
    kern = functools.partial(_attn_kernel, dh=dh, group=group, has_cache=has_cache, scale=float(dh) ** -0.5)
    return pl.pallas_call(
        kern,
        out_shape=jax.ShapeDtypeStruct((nseq, ls, ATT_HEADS * dh), BF16),
        grid=(nseq, ATT_KV_HEADS, ls // tq),
        in_specs=in_specs,
        out_specs=pl.BlockSpec((None, tq, group * dh), lambda s, kh, i: (s, i, kh)),
        scratch_shapes=[pltpu.VMEM((past + ls, dh), BF16), pltpu.VMEM((past + ls, 2 * dh), BF16)],
        compiler_params=_params(("arbitrary", "arbitrary", "arbitrary")),
        name="attention_cache" if has_cache else "attention_self",
    )(*args)


def _pack_rows(h):
    half = h.shape[1] // 2
    u = lax.bitcast_convert_type(h.astype(BF16).astype(F32), U32)
    return (u[:, :half] >> 16) | (u[:, half:] & jnp.uint32(0xFFFF0000))


def _unpack_rows(p):
    lo = lax.bitcast_convert_type(p << 16, F32)
    hi = lax.bitcast_convert_type(p & jnp.uint32(0xFFFF0000), F32)
    return jnp.concatenate([lo, hi], axis=1).astype(BF16)


ROW_EXTRA = 128


def _outproj_kernel(os_ref, op_ref, w_ref, ys_ref, yp_ref, g1_ref, nw_ref, sh_ref, sc_ref, rwh_ref, rwl_ref,
                    yos_ref, yop_ref, h_ref, lg_ref, *, ns):
    o = _pair_load(os_ref, op_ref, ns)
    y = _pair_load(ys_ref, yp_ref, ns) + g1_ref[...] * _dot(o, w_ref[...])
    _pair_store(yos_ref, yop_ref, ns, y)
    h = _norm_mod(y, nw_ref[...], sh_ref[...], sc_ref[...])
    h_ref[...] = _pack_rows(h)
    h_hi = h.astype(BF16)
    h_lo = (h - h_hi.astype(F32)).astype(BF16)
    rwh = rwh_ref[...]
    lg_ref[...] = _dot_nt(rwh, h_hi) + _dot_nt(rwl_ref[...], h_hi) + _dot_nt(rwh, h_lo)


def _outproj(o_s, o_p, w_bf16, y, mod, layer, norm_w, rw_hi, rw_lo):
    y_s, y_p = y
    ns, l, d = y_s.shape
    nb = ns + 1
    kdim = o_s.shape[2]
    ne = rw_hi.shape[0]
    tm = _pick(l, (512, 256, 128))
    ni = l // tm
    dp = jax.eval_shape(_pack_rows, jax.ShapeDtypeStruct((8, d), F32)).shape[1]
    outs = pl.pallas_call(
        functools.partial(_outproj_kernel, ns=ns),
        out_shape=[jax.ShapeDtypeStruct(y_s.shape, F32), jax.ShapeDtypeStruct(y_p.shape, F32),
                   jax.ShapeDtypeStruct((nb, l, dp), U32), jax.ShapeDtypeStruct((nb, ne, l), F32)],
        grid=(nb, ni),
        in_specs=_pair_specs(ns, ni, tm, kdim) + [
            pl.BlockSpec((kdim, d), lambda b, i: (0, 0), pipeline_mode=pl.Buffered(1)),
        ] + _pair_specs(ns, ni, tm, d) + [
            _mod_spec(d, layer, 2),
            pl.BlockSpec((1, d), lambda b, i: (0, 0)),
            _mod_spec(d, layer, 3),
            _mod_spec(d, layer, 4),
            pl.BlockSpec((ne, d), lambda b, i: (0, 0)),
            pl.BlockSpec((ne, d), lambda b, i: (0, 0)),
        ],
        out_specs=_pair_specs(ns, ni, tm, d) + [
            pl.BlockSpec((None, tm, dp), lambda b, i: (b, i, 0)),
            pl.BlockSpec((None, ne, tm), lambda b, i: (b, 0, i))],
        compiler_params=_params(("arbitrary", "arbitrary")),
        name="outproj_router",
    )(o_s, o_p, w_bf16, y_s, y_p, mod, norm_w.reshape(1, d), mod, mod, rw_hi, rw_lo)
    return (outs[0], outs[1]), outs[2], outs[3]


def _router_kernel(lg_ref, b_ref, bkt_ref, rank_ref, wt_ref, cnt_ref, carry_ref):
    ng = N_GROUPS
    sc = jax.nn.sigmoid(lg_ref[...])
    sel = sc + b_ref[...]
    sj = [sel[j * ng:(j + 1) * ng] for j in range(EXPERTS_PER_GROUP)]
    pj = [sc[j * ng:(j + 1) * ng] for j in range(EXPERTS_PER_GROUP)]
    hi1, lo1 = jnp.maximum(sj[0], sj[1]), jnp.minimum(sj[0], sj[1])
    hi2, lo2 = jnp.maximum(sj[2], sj[3]), jnp.minimum(sj[2], sj[3])
    top1 = jnp.maximum(hi1, hi2)
    top2 = jnp.maximum(jnp.minimum(hi1, hi2), jnp.maximum(lo1, lo2))
    gs = top1 + top2
    gmax = jnp.max(gs, axis=0, keepdims=True)
    gi = lax.broadcasted_iota(I32, gs.shape, 0)
    best = jnp.min(jnp.where(gs == gmax, gi, ng), axis=0, keepdims=True)
    onehot = gi == best
    neg = -jnp.inf
    vj = [jnp.max(jnp.where(onehot, a, neg), axis=0, keepdims=True) for a in sj]
    wj = [jnp.max(jnp.where(onehot, a, neg), axis=0, keepdims=True) for a in pj]

    def argmax4(vals):
        m, i = vals[0], jnp.zeros(vals[0].shape, I32)
        for j in range(1, EXPERTS_PER_GROUP):
            gt = vals[j] > m
            m = jnp.where(gt, vals[j], m)
            i = jnp.where(gt, j, i)
        return i

    i1 = argmax4(vj)
    i2 = argmax4([jnp.where(i1 == j, neg, vj[j]) for j in range(EXPERTS_PER_GROUP)])

    def take(vals, i):
        out = vals[0]
        for j in range(1, EXPERTS_PER_GROUP):
            out = jnp.where(i == j, vals[j], out)
        return out

    w1, w2 = take(wj, i1), take(wj, i2)
    tot = w1 + w2
    w1, w2 = w1 / tot, w2 / tot

    lo, hi = jnp.minimum(i1, i2), jnp.maximum(i1, i2)
    pair = jnp.where(lo == 0, hi - 1, jnp.where(lo == 1, jnp.where(hi == 3, 3, 4), 5))
    slot_a = jnp.where(pair < 3, 0, jnp.where(pair < 5, 1, 3))
    bucket = best * N_PAIRS + pair
    a_first = i1 == slot_a
    w_a = jnp.where(a_first, w1, w2)
    w_b = jnp.where(a_first, w2, w1)

    tl = bucket.shape[1]
    nbk = carry_ref.shape[0]

    @pl.when(jnp.logical_and(pl.program_id(0) == 0, pl.program_id(1) == 0))
    def _():
        carry_ref[...] = jnp.zeros_like(carry_ref)

    onehot_b = lax.broadcasted_iota(I32, (nbk, tl), 0) == bucket
    upper = (lax.broadcasted_iota(I32, (tl, tl), 0) <= lax.broadcasted_iota(I32, (tl, tl), 1))
    prefix = _dot(onehot_b.astype(BF16), upper.astype(BF16))
    carry = carry_ref[...]
    rank = jnp.sum(jnp.where(onehot_b, prefix - 1.0 + carry, 0.0), axis=0, keepdims=True)
    carry = carry + prefix[:, tl - 1:tl]
    carry_ref[...] = carry
    bkt_ref[...] = bucket
    rank_ref[...] = rank.astype(I32)
    cnt_ref[...] = jnp.broadcast_to(carry, cnt_ref.shape)
    rows = lax.broadcasted_iota(I32, (wt_ref.shape[1], tl), 0)
    wt_ref[...] = jnp.where(rows == 0, w_a, jnp.where(rows == 1, w_b, 0.0)).T


def _router(logits, bias_perm):
    nb, ne, l = logits.shape
    tl = _pick(l, (512, 256, 128))
    nbk = N_GROUPS * N_PAIRS
    return pl.pallas_call(
        _router_kernel,
        out_shape=[jax.ShapeDtypeStruct((nb, 1, l), I32), jax.ShapeDtypeStruct((nb, 1, l), I32),
                   jax.ShapeDtypeStruct((nb, l, ROW_EXTRA), F32), jax.ShapeDtypeStruct((nbk, 128), F32)],
        grid=(nb, l // tl),
        in_specs=[pl.BlockSpec((None, ne, tl), lambda b, i: (b, 0, i)),
                  pl.BlockSpec((ne, 1), lambda b, i: (0, 0))],
        out_specs=[pl.BlockSpec((None, 1, tl), lambda b, i: (b, 0, i)),
                   pl.BlockSpec((None, 1, tl), lambda b, i: (b, 0, i)),
                   pl.BlockSpec((None, tl, ROW_EXTRA), lambda b, i: (b, i, 0)),
                   pl.BlockSpec((nbk, 128), lambda b, i: (0, 0))],
        scratch_shapes=[pltpu.VMEM((nbk, 1), F32)],
        compiler_params=_params(("arbitrary", "arbitrary")),
        name="router_rank",
    )(logits, bias_perm.reshape(ne, 1))


def _moe_kernel(ea_ref, eb_ref, nu_ref, x_ref, wga_ref, wua_ref, wda_ref, wgb_ref, wub_ref, wdb_ref, o_ref):
    del ea_ref, eb_ref
    i = pl.program_id(0)

    dq = o_ref.shape[1]

    @pl.when(i < nu_ref[0])
    def _():
        x = _unpack_rows(x_ref[:, :dq])
        w = lax.bitcast_convert_type(x_ref[:, dq:], F32)

        def expert(wg, wu, wd):
            hid = jax.nn.silu(_dot(x, wg[...])) * _dot(x, wu[...])
            return _dot(hid.astype(BF16), wd[...])

        o_ref[...] = _pack_rows(w[:, 0:1] * expert(wga_ref, wua_ref, wda_ref)
                                + w[:, 1:2] * expert(wgb_ref, wub_ref, wdb_ref))

    @pl.when(i >= nu_ref[0])
    def _():
        o_ref[...] = jnp.zeros_like(o_ref)


def _moe_experts(xs, blk_a, blk_b, n_used, wg, wu, wd, layer, bm):
    p, dp = xs.shape
    _, _, d, de = wg.shape
    nblk = p // bm
    dq = dp - ROW_EXTRA
    w_in = lambda sel: pl.BlockSpec((None, None, d, de), lambda i, ea, eb, nu: (layer, (ea, eb)[sel][i], 0, 0))
    w_out = lambda sel: pl.BlockSpec((None, None, de, d), lambda i, ea, eb, nu: (layer, (ea, eb)[sel][i], 0, 0))
    grid_spec = pltpu.PrefetchScalarGridSpec(
        num_scalar_prefetch=3,
        grid=(nblk,),
        in_specs=[pl.BlockSpec((bm, dp), lambda i, ea, eb, nu: (i, 0)),
                  w_in(0), w_in(0), w_out(0), w_in(1), w_in(1), w_out(1)],
        out_specs=pl.BlockSpec((bm, dq), lambda i, ea, eb, nu: (i, 0)),
    )
    return pl.pallas_call(
        _moe_kernel,
        out_shape=jax.ShapeDtypeStruct((p, dq), U32),
        grid_spec=grid_spec,
        compiler_params=_params(("arbitrary",)),
        name="moe_experts",
    )(blk_a, blk_b, n_used, xs, wg, wu, wd, wg, wu, wd)


def _row_plan(bucket, rank, counts, bm):
    t = bucket.size
    nbk = N_GROUPS * N_PAIRS
    cnt = counts[:, 0].astype(I32)
    padded = (cnt + bm - 1) // bm * bm
    pends = jnp.cumsum(padded)
    pstarts = pends - padded
    onehot = bucket.reshape(t, 1) == jnp.arange(nbk, dtype=I32)[None, :]
    dest = jnp.sum(jnp.where(onehot, pstarts[None, :], 0), axis=1) + rank.reshape(t)
    p = (t + bm - 1) // bm * bm + nbk * bm
    nblk = p // bm
    blk_bucket = jnp.clip(jnp.searchsorted(pends, jnp.arange(nblk, dtype=I32) * bm, side='right'), 0, nbk - 1)
    grp, pair = blk_bucket // N_PAIRS, blk_bucket % N_PAIRS
    blk_a = grp * EXPERTS_PER_GROUP + jnp.asarray(_PAIR_A, I32)[pair]
    blk_b = grp * EXPERTS_PER_GROUP + jnp.asarray(_PAIR_B, I32)[pair]
    n_used = (pends[-1:] // bm).astype(I32)
    return dest.astype(I32), blk_a.astype(I32), blk_b.astype(I32), n_used, p


def _scatter_kernel(dest_ref, src_ref, wt_ref, init_hbm, out_hbm, rowbuf, sem, *, rows):
    del init_hbm
    base = pl.program_id(0) * rows
    dp = src_ref.shape[1]
    rowbuf[:, :dp] = src_ref[...]
    rowbuf[:, dp:] = lax.bitcast_convert_type(wt_ref[...], U32)
    for r in range(rows):
        pltpu.make_async_copy(rowbuf.at[r], out_hbm.at[dest_ref[base + r]], sem).start(priority=r % 2)
    pltpu.make_async_copy(rowbuf, out_hbm.at[pl.ds(0, rows)], sem).wait()


def _scatter_rows(h_rows, wt_rows, dest, p):
    t, dp = h_rows.shape
    rows = _pick(t, (512, 256, 128))
    grid_spec = pltpu.PrefetchScalarGridSpec(
        num_scalar_prefetch=1,
        grid=(t // rows,),
        in_specs=[pl.BlockSpec((rows, dp), lambda i, dst: (i, 0)),
                  pl.BlockSpec((rows, ROW_EXTRA), lambda i, dst: (i, 0)),
                  pl.BlockSpec(memory_space=pl.ANY)],
        out_specs=pl.BlockSpec(memory_space=pl.ANY),
        scratch_shapes=[pltpu.VMEM((rows, dp + ROW_EXTRA), U32), pltpu.SemaphoreType.DMA],
    )
    return pl.pallas_call(
        functools.partial(_scatter_kernel, rows=rows),
        out_shape=jax.ShapeDtypeStruct((p, dp + ROW_EXTRA), U32),
        grid_spec=grid_spec,
        input_output_aliases={3: 0},
        compiler_params=_params(("arbitrary",)),
        name="moe_scatter_rows",
    )(dest, h_rows, wt_rows, jnp.zeros((p, dp + ROW_EXTRA), U32))


def _combine_kernel(dest_ref, ys_ref, yp_ref, g_ref, m_hbm, os_ref, op_ref, mbuf0, mbuf1, sem,
                    *, ns, th, nsteps):
    step = pl.program_id(0) * pl.num_programs(1) + pl.program_id(1)
    bufs = (mbuf0, mbuf1)

    def gather(first_row, k):
        for r in range(th):
            pltpu.make_async_copy(m_hbm.at[dest_ref[first_row + r]], bufs[k].at[r],
                                  sem.at[k]).start(priority=r % 2)

    def wait(k):
        pltpu.make_async_copy(m_hbm.at[pl.ds(0, th)], bufs[k], sem.at[k]).wait()

    @pl.when(step == 0)
    def _():
        gather(0, 0)

    is_prompt = pl.program_id(0) >= ns
    g = g_ref[...]
    row0 = step * (2 * th)
    for k in range(2):
        rows = slice(k * th, (k + 1) * th)
        wait(k)
        nxt = jnp.minimum(row0 + (k + 1) * th, (nsteps * 2 - 1) * th)
        gather(nxt, 1 - k)
        y = jnp.where(is_prompt, yp_ref[rows, :], ys_ref[rows, :])
        val = y + g * _unpack_rows(bufs[k][...]).astype(F32)

        @pl.when(jnp.logical_not(is_prompt))
        def _():
            os_ref[rows, :] = val

        @pl.when(is_prompt)
        def _():
            op_ref[rows, :] = val

    @pl.when(step == nsteps - 1)
    def _():
        wait(0)


def _combine(y, mod, layer, moe_rows, dest):
    y_s, y_p = y
    ns, l, d = y_s.shape
    nb = ns + 1
    th = _pick(l // 2, (512, 256, 128, 64))
    tm = 2 * th
    ni = l // tm
    grid_spec = pltpu.PrefetchScalarGridSpec(
        num_scalar_prefetch=1,
        grid=(nb, ni),
        in_specs=_pair_specs(ns, ni, tm, d) + [
            pl.BlockSpec((None, None, None, 1, d), lambda b, i, dst: (layer, 5, b, 0, 0)),
            pl.BlockSpec(memory_space=pl.ANY),
        ],
        out_specs=_pair_specs(ns, ni, tm, d),
        scratch_shapes=[pltpu.VMEM((th, moe_rows.shape[1]), moe_rows.dtype),
                        pltpu.VMEM((th, moe_rows.shape[1]), moe_rows.dtype),
                        pltpu.SemaphoreType.DMA((2,))],
    )
    outs = pl.pallas_call(
        functools.partial(_combine_kernel, ns=ns, th=th, nsteps=nb * ni),
        out_shape=[jax.ShapeDtypeStruct(y_s.shape, F32), jax.ShapeDtypeStruct(y_p.shape, F32)],
        grid_spec=grid_spec,
        compiler_params=_params(("arbitrary", "arbitrary")),
        name="moe_combine",
    )(dest, y_s, y_p, mod, moe_rows)
    return (outs[0], outs[1])


def kernel(x_prompt, x_sample, c, state_ret_fwd, state_ret_bwd, cache_attn_k, cache_attn_v, c_ctx,
           ada_w, ada_b, norm1_w, norm2_w, ret_w_in, ret_w_out, ret_gn_w, ret_decay_fwd, ret_decay_bwd,
           attn_w_qkv, attn_w_o, attn_q_norm, attn_k_norm, router_w, router_b,
           moe_w_gate, moe_w_up, moe_w_down):
    n_prompt, seq, d = x_prompt.shape
    ns, l, _ = x_sample.shape
    assert n_prompt * seq == l, "prompt tokens must fill exactly one sample-length row"
    depth = ada_w.shape[0]
    nb = ns + 1
    dk = d // RET_HEADS
    dv = 2 * dk
    dh = d // ATT_HEADS
    bm = MOE_ROW_BLOCK

    y = (x_sample, x_prompt.reshape(1, l, d))
    nbp = (nb + 7) // 8 * 8
    cond = jnp.zeros((nbp, d), F32).at[:ns].set(c).at[ns].set(c_ctx)
    mod = _ada_params(cond, ada_w, ada_b)

    perm = (jnp.arange(N_EXPERTS) % N_GROUPS) * EXPERTS_PER_GROUP + jnp.arange(N_EXPERTS) // N_GROUPS
    rw_t = router_w.T[perm]
    rw_hi = rw_t.astype(BF16)
    rw_lo = (rw_t - rw_hi.astype(F32)).astype(BF16)
    rb_perm = router_b[perm]
    wg_bf16, wu_bf16, wd_bf16 = moe_w_gate.astype(BF16), moe_w_up.astype(BF16), moe_w_down.astype(BF16)

    ret_f, ret_b, k_new, v_new = [], [], [], []
    for i in range(depth):
        if i % 2 == 0:
            r = i // 2
            qkvg = _ret_inproj(y, mod, i, norm1_w[i], ret_w_in[r].astype(BF16), ret_gn_w[r], dk)
            o_s = _retention(qkvg, ns, 0, l, dk, ret_decay_fwd[r], ret_decay_bwd[r],
                             s0f=state_ret_fwd[:, r], s0b=state_ret_bwd[:, r])
            o_p, s_f, s_b = _retention(qkvg, n_prompt, ns, seq, dk,
                                       ret_decay_fwd[r], ret_decay_bwd[r], emit_final=True)
            o_s = o_s[0] if isinstance(o_s, (list, tuple)) else o_s
            ret_f.append(s_f)
            ret_b.append(s_b)
            w_o = ret_w_out[r].astype(BF16)
        else:
            a = i // 2
            qkv, k_p, v_p = _qkv_proj(y, mod, i, norm1_w[i], attn_w_qkv[a].astype(BF16),
                                      attn_q_norm[a], attn_k_norm[a], dh)
            past = cache_attn_k.shape[2]
            o_s = _attention(qkv, ns, 0, l, dh,
                             cache_k=cache_attn_k[:, a].reshape(ns, past, ATT_KV_HEADS * dh),
                             cache_v=cache_attn_v[:, a].reshape(ns, past, ATT_KV_HEADS * dh))
            o_p = _attention(qkv.reshape(nb * n_prompt, seq, qkv.shape[2]), n_prompt, ns * n_prompt, seq, dh)
            k_new.append(k_p.reshape(n_prompt, seq, ATT_KV_HEADS, dh))
            v_new.append(v_p.reshape(n_prompt, seq, ATT_KV_HEADS, dh))
            w_o = attn_w_o[a].astype(BF16)
        o_p = o_p.reshape(1, l, o_p.shape[-1])
        y, h2, logits = _outproj(o_s, o_p, w_o, y, mod, i, norm2_w[i], rw_hi, rw_lo)
        bucket, rank, wt, counts = _router(logits, rb_perm)
        dest, blk_a, blk_b, n_used, p = _row_plan(bucket, rank, counts, bm)
        xs = _scatter_rows(h2.reshape(nb * l, h2.shape[2]), wt.reshape(nb * l, wt.shape[2]), dest, p)
        moe_rows = _moe_experts(xs, blk_a, blk_b, n_used, wg_bf16, wu_bf16, wd_bf16, i, bm)
        y = _combine(y, mod, i, moe_rows, dest)

    y_s, y_p = y
    return (y_p.reshape(n_prompt, seq, d), y_s, jnp.stack(ret_f, axis=1), jnp.stack(ret_b, axis=1),
            jnp.stack(k_new, axis=1), jnp.stack(v_new, axis=1))
```

```python
import functools

import jax
import jax.numpy as jnp
from jax import lax
from jax.experimental import pallas as pl
from jax.experimental.pallas import tpu as pltpu

F32 = jnp.float32
BF16 = jnp.bfloat16
I32 = jnp.int32
U32 = jnp.uint32

EPS = 1e-6
GRID_W = 64
RET_HEADS = 4
ATT_HEADS = 8
ATT_KV_HEADS = 2
ROPE_THETA = 10000.0
N_EXPERTS = 32
N_GROUPS = 8
EXPERTS_PER_GROUP = N_EXPERTS // N_GROUPS
_PAIR_A = (0, 0, 0, 1, 1, 3)
_PAIR_B = (1, 2, 3, 3, 2, 2)
N_PAIRS = len(_PAIR_A)
MOE_ROW_BLOCK = 256

V7X_VMEM_BYTES = 64 * 1024 * 1024
VMEM_LIMIT_BYTES = V7X_VMEM_BYTES - V7X_VMEM_BYTES // 8


def _params(sem):
    return pltpu.CompilerParams(dimension_semantics=sem, vmem_limit_bytes=VMEM_LIMIT_BYTES)


def _dot(a, b):
    return jnp.dot(a, b, preferred_element_type=F32)


def _dot_nt(a, b):
    return lax.dot_general(a, b, (((1,), (1,)), ((), ())), preferred_element_type=F32)


def _dot_tn(a, b):
    return lax.dot_general(a, b, (((0,), (0,)), ((), ())), preferred_element_type=F32)


def _pick(n, prefs):
    for p in prefs:
        if n % p == 0:
            return p
    return n


def _ada_kernel(c_ref, w_ref, b_ref, o_ref):
    c = c_ref[...]
    s = jax.nn.silu(c).astype(BF16)
    o_ref[...] = _dot(s, w_ref[...].astype(BF16)) + b_ref[...]


def _ada_params(cond, ada_w, ada_b):
    depth, d, n6 = ada_w.shape
    nbp = cond.shape[0]
    tn = _pick(n6, (1536, 1024, 512, 256, 128))
    out = pl.pallas_call(
        _ada_kernel,
        out_shape=jax.ShapeDtypeStruct((depth, nbp, n6), F32),
        grid=(depth, n6 // tn),
        in_specs=[
            pl.BlockSpec((nbp, d), lambda l, j: (0, 0)),
            pl.BlockSpec((None, d, tn), lambda l, j: (l, 0, j)),
            pl.BlockSpec((None, 1, tn), lambda l, j: (l, 0, j)),
        ],
        out_specs=pl.BlockSpec((None, nbp, tn), lambda l, j: (l, 0, j)),
        compiler_params=_params(("arbitrary", "arbitrary")),
        name="ada_params",
    )(cond, ada_w, ada_b.reshape(depth, 1, n6))
    return out.reshape(depth, nbp, 6, 1, d).transpose(0, 2, 1, 3, 4)


def _mod_spec(d, layer, j):
    return pl.BlockSpec((None, None, None, 1, d), lambda b, i: (layer, j, b, 0, 0))


def _pair_specs(ns, ni, tm, w):
    return [pl.BlockSpec((None, tm, w), lambda b, i, *_: (jnp.minimum(b, ns - 1), jnp.where(b < ns, i, ni - 1), 0)),
            pl.BlockSpec((None, tm, w), lambda b, i, *_: (0, jnp.where(b < ns, 0, i), 0))]


def _pair_load(s_ref, p_ref, ns):
    return jnp.where(pl.program_id(0) >= ns, p_ref[...], s_ref[...])


def _pair_store(s_ref, p_ref, ns, val):
    @pl.when(pl.program_id(0) < ns)
    def _():
        s_ref[...] = val

    @pl.when(pl.program_id(0) >= ns)
    def _():
        p_ref[...] = val


def _norm_mod(x, nw, sh, sc):
    ms = jnp.mean(x * x, axis=-1, keepdims=True)
    h = x * lax.rsqrt(ms + EPS) * nw
    return h * (1.0 + sc) + sh


def _ret_inproj_kernel(xs_ref, xp_ref, nw_ref, sh_ref, sc_ref, w_ref, gn_ref, o_ref, *, ns, dk, tn, k_scale):
    x = _pair_load(xs_ref, xp_ref, ns)
    hb = _norm_mod(x, nw_ref[...], sh_ref[...], sc_ref[...]).astype(BF16)
    dv = 2 * dk
    hq, hv = RET_HEADS * dk, RET_HEADS * dv
    segments = ((0, dk, 0), (hq, dk, dk), (2 * hq, dv, 2 * dk), (2 * hq + hv, dv, 2 * dk + dv),
                (2 * hq + 2 * hv, dv, 2 * dk + 2 * dv))
    n = w_ref.shape[1]
    for j in range(n // tn):
        lo = j * tn
        seg = max(s for s in range(len(segments)) if segments[s][0] <= lo)
        start, width, off = segments[seg]
        acc = _dot(hb, w_ref[:, lo:lo + tn])
        if seg == 1:
            acc = acc * k_scale
        elif seg >= 3:
            acc = jax.nn.silu(acc) * gn_ref[:, lo - start:lo - start + tn]
        acc = acc.astype(BF16)
        head0 = (lo - start) // width
        for hh in range(tn // width):
            o_ref[head0 + hh, :, off:off + width] = acc[:, hh * width:(hh + 1) * width]


def _ret_inproj(y, mod, layer, norm_w, w_in_bf16, gn_w, dk):
    y_s, y_p = y
    ns, l, d = y_s.shape
    nb = ns + 1
    n = w_in_bf16.shape[1]
    hq = RET_HEADS * dk
    hv = 2 * hq
    tm = _pick(l, (512, 256, 128))
    tn = _pick(hq, (1024, 512, 256, 128))
    wph = n // RET_HEADS
    kern = functools.partial(_ret_inproj_kernel, ns=ns, dk=dk, tn=tn, k_scale=float(dk) ** -0.5)
    return pl.pallas_call(
        kern,
        out_shape=jax.ShapeDtypeStruct((nb, RET_HEADS, l, wph), BF16),
        grid=(nb, l // tm),
        in_specs=_pair_specs(ns, l // tm, tm, d) + [
            pl.BlockSpec((1, d), lambda b, i: (0, 0)),
            _mod_spec(d, layer, 0),
            _mod_spec(d, layer, 1),
            pl.BlockSpec((d, n), lambda b, i: (0, 0), pipeline_mode=pl.Buffered(1)),
            pl.BlockSpec((1, hv), lambda b, i: (0, 0)),
        ],
        out_specs=pl.BlockSpec((None, RET_HEADS, tm, wph), lambda b, i: (b, 0, i, 0)),
        compiler_params=_params(("arbitrary", "arbitrary")),
        name="ret_inproj",
    )(y_s, y_p, norm_w.reshape(1, d), mod, mod, w_in_bf16, gn_w.reshape(1, hv))


def _log_sigmoid(x):
    return jnp.minimum(x, 0.0) - jnp.log1p(jnp.exp(-jnp.abs(x)))


def _ret_kernel(*refs, ls, c, dk, has_init, emit_final):
    it = iter(refs)
    x_ref, df_ref, db_ref = (next(it) for _ in range(3))
    if has_init:
        s0f_ref, s0b_ref = next(it), next(it)
    y_ref = next(it)
    if emit_final:
        sf_out, sb_out = next(it), next(it)
    s_scr, sb_scr = next(it), next(it)

    dv = 2 * dk
    q_ref = x_ref.at[:, 0:dk]
    k_ref = x_ref.at[:, dk:2 * dk]
    v_ref = x_ref.at[:, 2 * dk:2 * dk + dv]
    gf_ref = x_ref.at[:, 2 * dk + dv:2 * dk + 2 * dv]
    gb_ref = x_ref.at[:, 2 * dk + 2 * dv:2 * dk + 3 * dv]
    nchunks = ls // c
    lg_f = _log_sigmoid(df_ref[...])[:, 0:1]
    lg_b = _log_sigmoid(db_ref[...])[:, 0:1]

    ri = lax.broadcasted_iota(I32, (c, c), 0)
    ci = lax.broadcasted_iota(I32, (c, c), 1)
    dif = (ri - ci).astype(F32)
    decay_f = jnp.where(dif >= 0, jnp.exp(jnp.maximum(dif, 0.0) * lg_f), 0.0).astype(BF16)
    decay_b = jnp.where(dif <= 0, jnp.exp(jnp.maximum(-dif, 0.0) * lg_b), 0.0).astype(BF16)
    pos = lax.broadcasted_iota(I32, (c, dk), 0).astype(F32)
    qdec_f = jnp.exp((pos + 1.0) * lg_f).astype(BF16)
    kdec_f = jnp.exp((c - 1.0 - pos) * lg_f).astype(BF16)
    qdec_b = jnp.exp((c - pos) * lg_b).astype(BF16)
    kdec_b = jnp.exp(pos * lg_b).astype(BF16)
    cdec_f = jnp.exp(c * lg_f)
    cdec_b = jnp.exp(c * lg_b)

    if has_init:
        s_scr[...] = s0b_ref[...]
    else:
        s_scr[...] = jnp.zeros_like(s_scr)

    def bwd_body(t, carry):
        n = nchunks - 1 - t
        r0 = pl.multiple_of(n * c, c)
        s = s_scr[...]
        sb_scr[n] = s.astype(BF16)
        kc = k_ref[pl.ds(r0, c), :]
        vc = v_ref[pl.ds(r0, c), :]
        s_scr[...] = s * cdec_b + _dot_tn(kc * kdec_b, vc)
        return carry

    lax.fori_loop(0, nchunks, bwd_body, 0, unroll=2 if nchunks % 2 == 0 else 1)
    if emit_final:
        sb_out[...] = s_scr[...]

    if has_init:
        s_scr[...] = s0f_ref[...]
    else:
        s_scr[...] = jnp.zeros_like(s_scr)

    def head_norm(o):
        mu = jnp.mean(o, axis=-1, keepdims=True)
        dlt = o - mu
        var = jnp.mean(dlt * dlt, axis=-1, keepdims=True)
        return dlt * lax.rsqrt(var + EPS)

    def fwd_body(n, carry):
        r0 = pl.multiple_of(n * c, c)
        rows = pl.ds(r0, c)
        qc = q_ref[rows, :]
        kc = k_ref[rows, :]
        vc = v_ref[rows, :]
        sc = _dot_nt(qc, kc).astype(BF16)
        s = s_scr[...]
        o_f = _dot(sc * decay_f, vc) + _dot(qc * qdec_f, s.astype(BF16))
        o_b = _dot(sc * decay_b, vc) + _dot(qc * qdec_b, sb_scr[n])
        s_scr[...] = s * cdec_f + _dot_tn(kc * kdec_f, vc)
        y_ref[rows, :] = (head_norm(o_f).astype(BF16) * gf_ref[rows, :]
                          + head_norm(o_b).astype(BF16) * gb_ref[rows, :])
        return carry

    lax.fori_loop(0, nchunks, fwd_body, 0, unroll=2 if nchunks % 2 == 0 else 1)
    if emit_final:
        sf_out[...] = s_scr[...]


def _retention(qkvg, nseq, row0, ls, dk, decay_f, decay_b, s0f=None, s0b=None, emit_final=False):
    h = RET_HEADS
    dv = 2 * dk
    c = _pick(ls, (256, 128))
    has_init = s0f is not None
    spr = qkvg.shape[2] // ls
    in_specs = [
        pl.BlockSpec((None, None, ls, qkvg.shape[3]), lambda s, hh: (row0 + s // spr, hh, s % spr, 0)),
        pl.BlockSpec((None, 1, 128), lambda s, hh: (hh, 0, 0)),
        pl.BlockSpec((None, 1, 128), lambda s, hh: (hh, 0, 0)),
    ]
    args = [qkvg,
            jnp.broadcast_to(decay_f.reshape(h, 1, 1), (h, 1, 128)),
            jnp.broadcast_to(decay_b.reshape(h, 1, 1), (h, 1, 128))]
    if has_init:
        in_specs += [pl.BlockSpec((None, None, dk, dv), lambda s, hh: (s, hh, 0, 0))] * 2
        args += [s0f, s0b]
    out_shape = [jax.ShapeDtypeStruct((nseq, ls, h * dv), BF16)]
    out_specs = [pl.BlockSpec((None, ls, dv), lambda s, hh: (s, 0, hh))]
    if emit_final:
        out_shape += [jax.ShapeDtypeStruct((nseq, h, dk, dv), F32)] * 2
        out_specs += [pl.BlockSpec((None, None, dk, dv), lambda s, hh: (s, hh, 0, 0))] * 2
    kern = functools.partial(_ret_kernel, ls=ls, c=c, dk=dk, has_init=has_init, emit_final=emit_final)
    return pl.pallas_call(
        kern,
        out_shape=out_shape,
        grid=(nseq, h),
        in_specs=in_specs,
        out_specs=out_specs,
        scratch_shapes=[pltpu.VMEM((dk, dv), F32), pltpu.VMEM((ls // c, dk, dv), BF16)],
        compiler_params=_params(("arbitrary", "arbitrary")),
        name="retention_init" if has_init else "retention_zero",
    )(*args)


def _qkv_kernel(xs_ref, xp_ref, nw_ref, sh_ref, sc_ref, w_ref, qn_ref, kn_ref, cos_ref, sa_ref, sb_ref,
                o_ref, kp_ref, vp_ref, *, ns, dh):
    nq = ATT_HEADS * dh
    nk = ATT_KV_HEADS * dh
    is_prompt = pl.program_id(0) >= ns
    tm = xs_ref.shape[0]
    nsub = 2 if tm % 256 == 0 else 1
    ts = tm // nsub
    cache_rows = []
    for sub in range(nsub):
        rows = slice(sub * ts, (sub + 1) * ts)
        x = jnp.where(is_prompt, xp_ref[rows, :], xs_ref[rows, :])
        hb = _norm_mod(x, nw_ref[...], sh_ref[...], sc_ref[...]).astype(BF16)
        cos, sa, sb = cos_ref[rows, :], sa_ref[rows, :], sb_ref[rows, :]

        def norm_rope(a, w, cos=cos, sa=sa, sb=sb):
            a = a * lax.rsqrt(jnp.mean(a * a, axis=-1, keepdims=True) + EPS) * w
            roped = a * cos + pltpu.roll(a, dh - dh // 4, 1) * sa + pltpu.roll(a, dh // 4, 1) * sb
            return a, roped

        q = _dot(hb, w_ref[:, :nq])
        for hh in range(ATT_HEADS):
            _, r = norm_rope(q[:, hh * dh:(hh + 1) * dh], qn_ref[...])
            o_ref[rows, hh * dh:(hh + 1) * dh] = r.astype(BF16)
        kvv = _dot(hb, w_ref[:, nq:])
        k_norm = []
        for hh in range(ATT_KV_HEADS):
            a, r = norm_rope(kvv[:, hh * dh:(hh + 1) * dh], kn_ref[...])
            o_ref[rows, nq + hh * dh:nq + (hh + 1) * dh] = r.astype(BF16)
            k_norm.append(a)
        o_ref[rows, nq + nk:] = kvv[:, nk:].astype(BF16)
        cache_rows.append((rows, k_norm, kvv[:, nk:]))

    @pl.when(is_prompt)
    def _():
        for rows, k_norm, v in cache_rows:
            for hh in range(ATT_KV_HEADS):
                kp_ref[rows, hh * dh:(hh + 1) * dh] = k_norm[hh]
            vp_ref[rows, :] = v


def _rope_tables(l, dh, rotate):
    axis_dim = dh // 2
    half = axis_dim // 2
    t = jnp.arange(l)
    row = (t // GRID_W).astype(F32)
    col = (t % GRID_W).astype(F32)
    inv = ROPE_THETA ** (-jnp.arange(0, axis_dim, 2, dtype=F32) / axis_dim)
    d = jnp.arange(dh)
    pos = jnp.where((d // axis_dim)[None, :] == 0, row[:, None], col[:, None])
    ang = pos * inv[d % half][None, :]
    first = ((d % axis_dim) < half)[None, :]
    cos, sin = jnp.cos(ang), jnp.sin(ang)
    if not rotate:
        cos, sin = jnp.ones_like(cos), jnp.zeros_like(sin)
    return cos, jnp.where(first, -sin, 0.0), jnp.where(first, 0.0, sin)


def _qkv_proj(y, mod, layer, norm_w, w_bf16, qn, kn, dh):
    y_s, y_p = y
    ns, l, d = y_s.shape
    nb = ns + 1
    n = w_bf16.shape[1]
    nk = ATT_KV_HEADS * dh
    tm = _pick(l, (512, 256, 128))
    tabs = [jnp.stack([a, b]) for a, b in zip(_rope_tables(l, dh, True), _rope_tables(l, dh, False))]
    tab_spec = pl.BlockSpec((None, tm, dh), lambda b, i: (jnp.where(b < ns, 0, 1), i, 0))
    prompt_spec = pl.BlockSpec((tm, nk), lambda b, i: (jnp.where(b < ns, 0, i), 0))
    return pl.pallas_call(
        functools.partial(_qkv_kernel, ns=ns, dh=dh),
        out_shape=[jax.ShapeDtypeStruct((nb, l, n), BF16), jax.ShapeDtypeStruct((l, nk), F32),
                   jax.ShapeDtypeStruct((l, nk), F32)],
        grid=(nb, l // tm),
        in_specs=_pair_specs(ns, l // tm, tm, d) + [
            pl.BlockSpec((1, d), lambda b, i: (0, 0)),
            _mod_spec(d, layer, 0),
            _mod_spec(d, layer, 1),
            pl.BlockSpec((d, n), lambda b, i: (0, 0), pipeline_mode=pl.Buffered(1)),
            pl.BlockSpec((1, dh), lambda b, i: (0, 0)),
            pl.BlockSpec((1, dh), lambda b, i: (0, 0)),
            tab_spec, tab_spec, tab_spec,
        ],
        out_specs=[pl.BlockSpec((None, tm, n), lambda b, i: (b, i, 0)), prompt_spec, prompt_spec],
        compiler_params=_params(("arbitrary", "arbitrary")),
        name="qkv_proj",
    )(y_s, y_p, norm_w.reshape(1, d), mod, mod, w_bf16, qn.reshape(1, dh), kn.reshape(1, dh), *tabs)


def _attn_kernel(*refs, dh, group, has_cache, scale):
    if has_cache:
        q_ref, k_ref, v_ref, kc_ref, vc_ref, o_ref, kall, vext = refs
        past = kc_ref.shape[0]
    else:
        q_ref, k_ref, v_ref, o_ref, kall, vext = refs
        past = 0

    @pl.when(pl.program_id(2) == 0)
    def _():
        if has_cache:
            kall[:past, :] = kc_ref[...].astype(BF16)
            vext[:past, :dh] = vc_ref[...].astype(BF16)
        kall[past:, :] = k_ref[...]
        vext[past:, :dh] = v_ref[...]
        vext[:, dh:] = jnp.ones((vext.shape[0], dh), BF16)

    c = scale * 1.4426950408889634
    tq = q_ref.shape[0]
    q = jnp.concatenate([q_ref[:, g * dh:(g + 1) * dh] for g in range(group)], axis=0)
    lk = kall.shape[0]
    ck = _pick(lk, (1152, 512, 256, 128))
    m = None
    for j in range(lk // ck):
        s = _dot_nt(q, kall[j * ck:(j + 1) * ck, :])
        mj = jnp.max(s, axis=-1, keepdims=True)
        vj = vext[j * ck:(j + 1) * ck, :]
        if m is None:
            m = mj
            o = _dot(jnp.exp2((s - m) * c).astype(BF16), vj)
        else:
            m_new = jnp.maximum(m, mj)
            o = o * jnp.exp2((m - m_new) * c) + _dot(jnp.exp2((s - m_new) * c).astype(BF16), vj)
            m = m_new
    for g in range(group):
        og = o[g * tq:(g + 1) * tq]
        o_ref[:, g * dh:(g + 1) * dh] = (og[:, :dh] / og[:, dh:dh + 1]).astype(BF16)


def _attention(qkv, nseq, seq0, ls, dh, cache_k=None, cache_v=None):
    group = ATT_HEADS // ATT_KV_HEADS
    has_cache = cache_k is not None
    tq = _pick(ls, (256, 128))
    kcol = ATT_HEADS
    vcol = ATT_HEADS + ATT_KV_HEADS
    in_specs = [
        pl.BlockSpec((None, tq, group * dh), lambda s, kh, i: (s + seq0, i, kh)),
        pl.BlockSpec((None, ls, dh), lambda s, kh, i: (s + seq0, 0, kcol + kh)),
        pl.BlockSpec((None, ls, dh), lambda s, kh, i: (s + seq0, 0, vcol + kh)),
    ]
    args = [qkv, qkv, qkv]
    past = 0
    if has_cache:
        past = cache_k.shape[1]
        in_specs += [pl.BlockSpec((None, past, dh), lambda s, kh, i: (s, 0, kh))] * 2
        args += [cache_k, cache_v]
    kern = functools.partial(_attn_kernel, dh=dh, group=group, has_cache=has_cache, scale=float(dh) ** -0.5)
    return pl.pallas_call(
        kern,
        out_shape=jax.ShapeDtypeStruct((nseq, ls, ATT_HEADS * dh), BF16),
        grid=(nseq, ATT_KV_HEADS, ls // tq),
        in_specs=in_specs,
        out_specs=pl.BlockSpec((None, tq, group * dh), lambda s, kh, i: (s, i, kh)),
        scratch_shapes=[pltpu.VMEM((past + ls, dh), BF16), pltpu.VMEM((past + ls, 2 * dh), BF16)],
        compiler_params=_params(("arbitrary", "arbitrary", "arbitrary")),
        name="attention_cache" if has_cache else "attention_self",
    )(*args)


def _pack_rows(h):
    half = h.shape[1] // 2
    u = lax.bitcast_convert_type(h.astype(BF16).astype(F32), U32)
    return (u[:, :half] >> 16) | (u[:, half:] & jnp.uint32(0xFFFF0000))


def _unpack_rows(p):
    lo = lax.bitcast_convert_type(p << 16, F32)
    hi = lax.bitcast_convert_type(p & jnp.uint32(0xFFFF0000), F32)
    return jnp.concatenate([lo, hi], axis=1).astype(BF16)


ROW_EXTRA = 128


def _outproj_kernel(os_ref, op_ref, w_ref, ys_ref, yp_ref, g1_ref, nw_ref, sh_ref, sc_ref, rwh_ref, rwl_ref,
                    yos_ref, yop_ref, h_ref, lg_ref, *, ns):
    o = _pair_load(os_ref, op_ref, ns)
    y = _pair_load(ys_ref, yp_ref, ns) + g1_ref[...] * _dot(o, w_ref[...])
    _pair_store(yos_ref, yop_ref, ns, y)
    h = _norm_mod(y, nw_ref[...], sh_ref[...], sc_ref[...])
    h_ref[...] = _pack_rows(h)
    h_hi = h.astype(BF16)
    h_lo = (h - h_hi.astype(F32)).astype(BF16)
    rwh = rwh_ref[...]
    lg_ref[...] = _dot_nt(rwh, h_hi) + _dot_nt(rwl_ref[...], h_hi) + _dot_nt(rwh, h_lo)


def _outproj(o_s, o_p, w_bf16, y, mod, layer, norm_w, rw_hi, rw_lo):
    y_s, y_p = y
    ns, l, d = y_s.shape
    nb = ns + 1
    kdim = o_s.shape[2]
    ne = rw_hi.shape[0]
    tm = _pick(l, (512, 256, 128))
    ni = l // tm
    dp = jax.eval_shape(_pack_rows, jax.ShapeDtypeStruct((8, d), F32)).shape[1]
    outs = pl.pallas_call(
        functools.partial(_outproj_kernel, ns=ns),
        out_shape=[jax.ShapeDtypeStruct(y_s.shape, F32), jax.ShapeDtypeStruct(y_p.shape, F32),
                   jax.ShapeDtypeStruct((nb, l, dp), U32), jax.ShapeDtypeStruct((nb, ne, l), F32)],
        grid=(nb, ni),
        in_specs=_pair_specs(ns, ni, tm, kdim) + [
            pl.BlockSpec((kdim, d), lambda b, i: (0, 0), pipeline_mode=pl.Buffered(1)),
        ] + _pair_specs(ns, ni, tm, d) + [
            _mod_spec(d, layer, 2),
            pl.BlockSpec((1, d), lambda b, i: (0, 0)),
            _mod_spec(d, layer, 3),
            _mod_spec(d, layer, 4),
            pl.BlockSpec((ne, d), lambda b, i: (0, 0)),
            pl.BlockSpec((ne, d), lambda b, i: (0, 0)),
        ],
        out_specs=_pair_specs(ns, ni, tm, d) + [
            pl.BlockSpec((None, tm, dp), lambda b, i: (b, i, 0)),
            pl.BlockSpec((None, ne, tm), lambda b, i: (b, 0, i))],
        compiler_params=_params(("arbitrary", "arbitrary")),
        name="outproj_router",
    )(o_s, o_p, w_bf16, y_s, y_p, mod, norm_w.reshape(1, d), mod, mod, rw_hi, rw_lo)
    return (outs[0], outs[1]), outs[2], outs[3]


def _router_kernel(lg_ref, b_ref, bkt_ref, rank_ref, wt_ref, cnt_ref, carry_ref):
    ng = N_GROUPS
    sc = jax.nn.sigmoid(lg_ref[...])
    sel = sc + b_ref[...]
    sj = [sel[j * ng:(j + 1) * ng] for j in range(EXPERTS_PER_GROUP)]
    pj = [sc[j * ng:(j + 1) * ng] for j in range(EXPERTS_PER_GROUP)]
    hi1, lo1 = jnp.maximum(sj[0], sj[1]), jnp.minimum(sj[0], sj[1])
    hi2, lo2 = jnp.maximum(sj[2], sj[3]), jnp.minimum(sj[2], sj[3])
    top1 = jnp.maximum(hi1, hi2)
    top2 = jnp.maximum(jnp.minimum(hi1, hi2), jnp.maximum(lo1, lo2))
    gs = top1 + top2
    gmax = jnp.max(gs, axis=0, keepdims=True)
    gi = lax.broadcasted_iota(I32, gs.shape, 0)
    best = jnp.min(jnp.where(gs == gmax, gi, ng), axis=0, keepdims=True)
    onehot = gi == best
    neg = -jnp.inf
    vj = [jnp.max(jnp.where(onehot, a, neg), axis=0, keepdims=True) for a in sj]
    wj = [jnp.max(jnp.where(onehot, a, neg), axis=0, keepdims=True) for a in pj]

    def argmax4(vals):
        m, i = vals[0], jnp.zeros(vals[0].shape, I32)
        for j in range(1, EXPERTS_PER_GROUP):
            gt = vals[j] > m
            m = jnp.where(gt, vals[j], m)
            i = jnp.where(gt, j, i)
        return i

    i1 = argmax4(vj)
    i2 = argmax4([jnp.where(i1 == j, neg, vj[j]) for j in range(EXPERTS_PER_GROUP)])

    def take(vals, i):
        out = vals[0]
        for j in range(1, EXPERTS_PER_GROUP):
            out = jnp.where(i == j, vals[j], out)
        return out

    w1, w2 = take(wj, i1), take(wj, i2)
    tot = w1 + w2
    w1, w2 = w1 / tot, w2 / tot

    lo, hi = jnp.minimum(i1, i2), jnp.maximum(i1, i2)
    pair = jnp.where(lo == 0, hi - 1, jnp.where(lo == 1, jnp.where(hi == 3, 3, 4), 5))
    slot_a = jnp.where(pair < 3, 0, jnp.where(pair < 5, 1, 3))
    bucket = best * N_PAIRS + pair
    a_first = i1 == slot_a
    w_a = jnp.where(a_first, w1, w2)
    w_b = jnp.where(a_first, w2, w1)

    tl = bucket.shape[1]
    nbk = carry_ref.shape[0]

    @pl.when(jnp.logical_and(pl.program_id(0) == 0, pl.program_id(1) == 0))
    def _():
        carry_ref[...] = jnp.zeros_like(carry_ref)

    onehot_b = lax.broadcasted_iota(I32, (nbk, tl), 0) == bucket
    upper = (lax.broadcasted_iota(I32, (tl, tl), 0) <= lax.broadcasted_iota(I32, (tl, tl), 1))
    prefix = _dot(onehot_b.astype(BF16), upper.astype(BF16))
    carry = carry_ref[...]
    rank = jnp.sum(jnp.where(onehot_b, prefix - 1.0 + carry, 0.0), axis=0, keepdims=True)
    carry = carry + prefix[:, tl - 1:tl]
    carry_ref[...] = carry
    bkt_ref[...] = bucket
    rank_ref[...] = rank.astype(I32)
    cnt_ref[...] = jnp.broadcast_to(carry, cnt_ref.shape)
    rows = lax.broadcasted_iota(I32, (wt_ref.shape[1], tl), 0)
    wt_ref[...] = jnp.where(rows == 0, w_a, jnp.where(rows == 1, w_b, 0.0)).T


def _router(logits, bias_perm):
    nb, ne, l = logits.shape
    tl = _pick(l, (512, 256, 128))
    nbk = N_GROUPS * N_PAIRS
    return pl.pallas_call(
        _router_kernel,
        out_shape=[jax.ShapeDtypeStruct((nb, 1, l), I32), jax.ShapeDtypeStruct((nb, 1, l), I32),
                   jax.ShapeDtypeStruct((nb, l, ROW_EXTRA), F32), jax.ShapeDtypeStruct((nbk, 128), F32)],
        grid=(nb, l // tl),
        in_specs=[pl.BlockSpec((None, ne, tl), lambda b, i: (b, 0, i)),
                  pl.BlockSpec((ne, 1), lambda b, i: (0, 0))],
        out_specs=[pl.BlockSpec((None, 1, tl), lambda b, i: (b, 0, i)),
                   pl.BlockSpec((None, 1, tl), lambda b, i: (b, 0, i)),
                   pl.BlockSpec((None, tl, ROW_EXTRA), lambda b, i: (b, i, 0)),
                   pl.BlockSpec((nbk, 128), lambda b, i: (0, 0))],
        scratch_shapes=[pltpu.VMEM((nbk, 1), F32)],
        compiler_params=_params(("arbitrary", "arbitrary")),
        name="router_rank",
    )(logits, bias_perm.reshape(ne, 1))


def _moe_kernel(ea_ref, eb_ref, nu_ref, x_ref, wga_ref, wua_ref, wda_ref, wgb_ref, wub_ref, wdb_ref, o_ref):
    del ea_ref, eb_ref
    i = pl.program_id(0)

    dq = o_ref.shape[1]

    @pl.when(i < nu_ref[0])
    def _():
        x = _unpack_rows(x_ref[:, :dq])
        w = lax.bitcast_convert_type(x_ref[:, dq:], F32)

        def expert(wg, wu, wd):
            hid = jax.nn.silu(_dot(x, wg[...])) * _dot(x, wu[...])
            return _dot(hid.astype(BF16), wd[...])

        o_ref[...] = _pack_rows(w[:, 0:1] * expert(wga_ref, wua_ref, wda_ref)
                                + w[:, 1:2] * expert(wgb_ref, wub_ref, wdb_ref))

    @pl.when(i >= nu_ref[0])
    def _():
        o_ref[...] = jnp.zeros_like(o_ref)


def _moe_experts(xs, blk_a, blk_b, n_used, wg, wu, wd, layer, bm):
    p, dp = xs.shape
    _, _, d, de = wg.shape
    nblk = p // bm
    dq = dp - ROW_EXTRA
    w_in = lambda sel: pl.BlockSpec((None, None, d, de), lambda i, ea, eb, nu: (layer, (ea, eb)[sel][i], 0, 0))
    w_out = lambda sel: pl.BlockSpec((None, None, de, d), lambda i, ea, eb, nu: (layer, (ea, eb)[sel][i], 0, 0))
    grid_spec = pltpu.PrefetchScalarGridSpec(
        num_scalar_prefetch=3,
        grid=(nblk,),
        in_specs=[pl.BlockSpec((bm, dp), lambda i, ea, eb, nu: (i, 0)),
                  w_in(0), w_in(0), w_out(0), w_in(1), w_in(1), w_out(1)],
        out_specs=pl.BlockSpec((bm, dq), lambda i, ea, eb, nu: (i, 0)),
    )
    return pl.pallas_call(
        _moe_kernel,
        out_shape=jax.ShapeDtypeStruct((p, dq), U32),
        grid_spec=grid_spec,
        compiler_params=_params(("arbitrary",)),
        name="moe_experts",
    )(blk_a, blk_b, n_used, xs, wg, wu, wd, wg, wu, wd)


def _row_plan(bucket, rank, counts, bm):
    t = bucket.size
    nbk = N_GROUPS * N_PAIRS
    cnt = counts[:, 0].astype(I32)
    padded = (cnt + bm - 1) // bm * bm
    pends = jnp.cumsum(padded)
    pstarts = pends - padded
    onehot = bucket.reshape(t, 1) == jnp.arange(nbk, dtype=I32)[None, :]
    dest = jnp.sum(jnp.where(onehot, pstarts[None, :], 0), axis=1) + rank.reshape(t)
    p = (t + bm - 1) // bm * bm + nbk * bm
    nblk = p // bm
    blk_bucket = jnp.clip(jnp.searchsorted(pends, jnp.arange(nblk, dtype=I32) * bm, side='right'), 0, nbk - 1)
    grp, pair = blk_bucket // N_PAIRS, blk_bucket % N_PAIRS
    blk_a = grp * EXPERTS_PER_GROUP + jnp.asarray(_PAIR_A, I32)[pair]
    blk_b = grp * EXPERTS_PER_GROUP + jnp.asarray(_PAIR_B, I32)[pair]
    n_used = (pends[-1:] // bm).astype(I32)
    return dest.astype(I32), blk_a.astype(I32), blk_b.astype(I32), n_used, p


def _scatter_kernel(dest_ref, src_ref, wt_ref, init_hbm, out_hbm, rowbuf, sem, *, rows):
    del init_hbm
    base = pl.program_id(0) * rows
    dp = src_ref.shape[1]
    rowbuf[:, :dp] = src_ref[...]
    rowbuf[:, dp:] = lax.bitcast_convert_type(wt_ref[...], U32)
    for r in range(rows):
        pltpu.make_async_copy(rowbuf.at[r], out_hbm.at[dest_ref[base + r]], sem).start(priority=r % 2)
    pltpu.make_async_copy(rowbuf, out_hbm.at[pl.ds(0, rows)], sem).wait()


def _scatter_rows(h_rows, wt_rows, dest, p):
    t, dp = h_rows.shape
    rows = _pick(t, (512, 256, 128))
    grid_spec = pltpu.PrefetchScalarGridSpec(
        num_scalar_prefetch=1,
        grid=(t // rows,),
        in_specs=[pl.BlockSpec((rows, dp), lambda i, dst: (i, 0)),
                  pl.BlockSpec((rows, ROW_EXTRA), lambda i, dst: (i, 0)),
                  pl.BlockSpec(memory_space=pl.ANY)],
        out_specs=pl.BlockSpec(memory_space=pl.ANY),
        scratch_shapes=[pltpu.VMEM((rows, dp + ROW_EXTRA), U32), pltpu.SemaphoreType.DMA],
    )
    return pl.pallas_call(
        functools.partial(_scatter_kernel, rows=rows),
        out_shape=jax.ShapeDtypeStruct((p, dp + ROW_EXTRA), U32),
        grid_spec=grid_spec,
        input_output_aliases={3: 0},
        compiler_params=_params(("arbitrary",)),
        name="moe_scatter_rows",
    )(dest, h_rows, wt_rows, jnp.zeros((p, dp + ROW_EXTRA), U32))


def _combine_kernel(dest_ref, ys_ref, yp_ref, g_ref, m_hbm, os_ref, op_ref, mbuf0, mbuf1, sem,
                    *, ns, th, nsteps):
    step = pl.program_id(0) * pl.num_programs(1) + pl.program_id(1)
    bufs = (mbuf0, mbuf1)

    def gather(first_row, k):
        for r in range(th):
            pltpu.make_async_copy(m_hbm.at[dest_ref[first_row + r]], bufs[k].at[r],
                                  sem.at[k]).start(priority=r % 2)

    def wait(k):
        pltpu.make_async_copy(m_hbm.at[pl.ds(0, th)], bufs[k], sem.at[k]).wait()

    @pl.when(step == 0)
    def _():
        gather(0, 0)

    is_prompt = pl.program_id(0) >= ns
    g = g_ref[...]
    row0 = step * (2 * th)
    for k in range(2):
        rows = slice(k * th, (k + 1) * th)
        wait(k)
        nxt = jnp.minimum(row0 + (k + 1) * th, (nsteps * 2 - 1) * th)
        gather(nxt, 1 - k)
        y = jnp.where(is_prompt, yp_ref[rows, :], ys_ref[rows, :])
        val = y + g * _unpack_rows(bufs[k][...]).astype(F32)

        @pl.when(jnp.logical_not(is_prompt))
        def _():
            os_ref[rows, :] = val

        @pl.when(is_prompt)
        def _():
            op_ref[rows, :] = val

    @pl.when(step == nsteps - 1)
    def _():
        wait(0)


def _combine(y, mod, layer, moe_rows, dest):
    y_s, y_p = y
    ns, l, d = y_s.shape
    nb = ns + 1
    th = _pick(l // 2, (512, 256, 128, 64))
    tm = 2 * th
    ni = l // tm
    grid_spec = pltpu.PrefetchScalarGridSpec(
        num_scalar_prefetch=1,
        grid=(nb, ni),
        in_specs=_pair_specs(ns, ni, tm, d) + [
            pl.BlockSpec((None, None, None, 1, d), lambda b, i, dst: (layer, 5, b, 0, 0)),
            pl.BlockSpec(memory_space=pl.ANY),
        ],
        out_specs=_pair_specs(ns, ni, tm, d),
        scratch_shapes=[pltpu.VMEM((th, moe_rows.shape[1]), moe_rows.dtype),
                        pltpu.VMEM((th, moe_rows.shape[1]), moe_rows.dtype),
                        pltpu.SemaphoreType.DMA((2,))],
    )
    outs = pl.pallas_call(
        functools.partial(_combine_kernel, ns=ns, th=th, nsteps=nb * ni),
        out_shape=[jax.ShapeDtypeStruct(y_s.shape, F32), jax.ShapeDtypeStruct(y_p.shape, F32)],
        grid_spec=grid_spec,
        compiler_params=_params(("arbitrary", "arbitrary")),
        name="moe_combine",
    )(dest, y_s, y_p, mod, moe_rows)
    return (outs[0], outs[1])


def kernel(x_prompt, x_sample, c, state_ret_fwd, state_ret_bwd, cache_attn_k, cache_attn_v, c_ctx,
           ada_w, ada_b, norm1_w, norm2_w, ret_w_in, ret_w_out, ret_gn_w, ret_decay_fwd, ret_decay_bwd,
           attn_w_qkv, attn_w_o, attn_q_norm, attn_k_norm, router_w, router_b,
           moe_w_gate, moe_w_up, moe_w_down):
    n_prompt, seq, d = x_prompt.shape
    ns, l, _ = x_sample.shape
    assert n_prompt * seq == l, "prompt tokens must fill exactly one sample-length row"
    depth = ada_w.shape[0]
    nb = ns + 1
    dk = d // RET_HEADS
    dh = d // ATT_HEADS
    bm = MOE_ROW_BLOCK

    y = (x_sample, x_prompt.reshape(1, l, d))
    nbp = (nb + 7) // 8 * 8
    cond = jnp.zeros((nbp, d), F32).at[:ns].set(c).at[ns].set(c_ctx)
    mod = _ada_params(cond, ada_w, ada_b)

    perm = (jnp.arange(N_EXPERTS) % N_GROUPS) * EXPERTS_PER_GROUP + jnp.arange(N_EXPERTS) // N_GROUPS
    rw_t = router_w.T[perm]
    rw_hi = rw_t.astype(BF16)
    rw_lo = (rw_t - rw_hi.astype(F32)).astype(BF16)
    rb_perm = router_b[perm]
    wg_bf16, wu_bf16, wd_bf16 = moe_w_gate.astype(BF16), moe_w_up.astype(BF16), moe_w_down.astype(BF16)

    ret_f, ret_b, k_new, v_new = [], [], [], []
    for i in range(depth):
        if i % 2 == 0:
            r = i // 2
            qkvg = _ret_inproj(y, mod, i, norm1_w[i], ret_w_in[r].astype(BF16), ret_gn_w[r], dk)
            (o_s,) = _retention(qkvg, ns, 0, l, dk, ret_decay_fwd[r], ret_decay_bwd[r],
                                s0f=state_ret_fwd[:, r], s0b=state_ret_bwd[:, r])
            o_p, s_f, s_b = _retention(qkvg, n_prompt, ns, seq, dk,
                                       ret_decay_fwd[r], ret_decay_bwd[r], emit_final=True)
            ret_f.append(s_f)
            ret_b.append(s_b)
            w_o = ret_w_out[r].astype(BF16)
        else:
            a = i // 2
            qkv, k_p, v_p = _qkv_proj(y, mod, i, norm1_w[i], attn_w_qkv[a].astype(BF16),
                                      attn_q_norm[a], attn_k_norm[a], dh)
            past = cache_attn_k.shape[2]
            o_s = _attention(qkv, ns, 0, l, dh,
                             cache_k=cache_attn_k[:, a].reshape(ns, past, ATT_KV_HEADS * dh),
                             cache_v=cache_attn_v[:, a].reshape(ns, past, ATT_KV_HEADS * dh))
            o_p = _attention(qkv.reshape(nb * n_prompt, seq, qkv.shape[2]), n_prompt, ns * n_prompt, seq, dh)
            k_new.append(k_p.reshape(n_prompt, seq, ATT_KV_HEADS, dh))
            v_new.append(v_p.reshape(n_prompt, seq, ATT_KV_HEADS, dh))
            w_o = attn_w_o[a].astype(BF16)
        o_p = o_p.reshape(1, l, o_p.shape[-1])
        y, h2, logits = _outproj(o_s, o_p, w_o, y, mod, i, norm2_w[i], rw_hi, rw_lo)
        bucket, rank, wt, counts = _router(logits, rb_perm)
        dest, blk_a, blk_b, n_used, p = _row_plan(bucket, rank, counts, bm)
        xs = _scatter_rows(h2.reshape(nb * l, h2.shape[2]), wt.reshape(nb * l, wt.shape[2]), dest, p)
        moe_rows = _moe_experts(xs, blk_a, blk_b, n_used, wg_bf16, wu_bf16, wd_bf16, i, bm)
        y = _combine(y, mod, i, moe_rows, dest)

    y_s, y_p = y
    return (y_p.reshape(n_prompt, seq, d), y_s, jnp.stack(ret_f, axis=1), jnp.stack(ret_b, axis=1),
            jnp.stack(k_new, axis=1), jnp.stack(v_new, axis=1))
```

```python
import functools

import jax
import jax.numpy as jnp
from jax import lax
from jax.experimental import pallas as pl
from jax.experimental.pallas import tpu as pltpu

F32 = jnp.float32
BF16 = jnp.bfloat16
I32 = jnp.int32
U32 = jnp.uint32

EPS = 1e-6
GRID_W = 64
RET_HEADS = 4
ATT_HEADS = 8
ATT_KV_HEADS = 2
ROPE_THETA = 10000.0
N_EXPERTS = 32
N_GROUPS = 8
EXPERTS_PER_GROUP = N_EXPERTS // N_GROUPS
TOP_K = 2
_PAIR_A = (0, 0, 0, 1, 1, 3)
_PAIR_B = (1, 2, 3, 3, 2, 2)
N_PAIRS = len(_PAIR_A)
MOE_ROW_BLOCK = 256

V7X_VMEM_BYTES = 64 * 1024 * 1024
VMEM_LIMIT_BYTES = 56 * 1024 * 1024


def _params(sem):
    return pltpu.CompilerParams(dimension_semantics=sem, vmem_limit_bytes=VMEM_LIMIT_BYTES)


def _dot(a, b):
    return jnp.dot(a, b, preferred_element_type=F32)


def _dot_nt(a, b):
    return lax.dot_general(a, b, (((1,), (1,)), ((), ())), preferred_element_type=F32)


def _dot_tn(a, b):
    return lax.dot_general(a, b, (((0,), (0,)), ((), ())), preferred_element_type=F32)


def _pick(n, prefs):
    for p in prefs:
        if n % p == 0:
            return p
    return n


def _ada_kernel(c_ref, w_ref, b_ref, o_ref):
    c = c_ref[...]
    s = jax.nn.silu(c).astype(BF16)
    o_ref[...] = _dot(s, w_ref[...].astype(BF16)) + b_ref[...]


def _ada_params(cond, ada_w, ada_b):
    depth, d, n6 = ada_w.shape
    nbp = cond.shape[0]
    tn = _pick(n6, (1536, 1024, 512, 256, 128))
    out = pl.pallas_call(
        _ada_kernel,
        out_shape=jax.ShapeDtypeStruct((depth, nbp, n6), F32),
        grid=(depth, n6 // tn),
        in_specs=[
            pl.BlockSpec((nbp, d), lambda l, j: (0, 0)),
            pl.BlockSpec((None, d, tn), lambda l, j: (l, 0, j)),
            pl.BlockSpec((None, 1, tn), lambda l, j: (l, 0, j)),
        ],
        out_specs=pl.BlockSpec((None, nbp, tn), lambda l, j: (l, 0, j)),
        compiler_params=_params(("arbitrary", "arbitrary")),
        name="ada_params",
    )(cond, ada_w, ada_b.reshape(depth, 1, n6))
    return out.reshape(depth, nbp, 6, 1, d).transpose(0, 2, 1, 3, 4)


def _mod_spec(d, layer, j):
    return pl.BlockSpec((None, None, None, 1, d), lambda b, i: (layer, j, b, 0, 0))


def _pair_specs(ns, ni, tm, w):
    return [pl.BlockSpec((None, tm, w), lambda b, i, *_: (jnp.minimum(b, ns - 1), jnp.where(b < ns, i, ni - 1), 0)),
            pl.BlockSpec((None, tm, w), lambda b, i, *_: (0, jnp.where(b < ns, 0, i), 0))]


def _pair_load(s_ref, p_ref, ns):
    return jnp.where(pl.program_id(0) >= ns, p_ref[...], s_ref[...])


def _pair_store(s_ref, p_ref, ns, val):
    @pl.when(pl.program_id(0) < ns)
    def _():
        s_ref[...] = val

    @pl.when(pl.program_id(0) >= ns)
    def _():
        p_ref[...] = val


def _norm_mod(x, nw, sh, sc):
    ms = jnp.mean(x * x, axis=-1, keepdims=True)
    h = x * lax.rsqrt(ms + EPS) * nw
    return h * (1.0 + sc) + sh


def _ret_inproj_kernel(xs_ref, xp_ref, nw_ref, sh_ref, sc_ref, w_ref, gn_ref, o_ref, *, ns, dk, tn, k_scale):
    x = _pair_load(xs_ref, xp_ref, ns)
    hb = _norm_mod(x, nw_ref[...], sh_ref[...], sc_ref[...]).astype(BF16)
    dv = 2 * dk
    hq, hv = RET_HEADS * dk, RET_HEADS * dv
    segments = ((0, dk, 0), (hq, dk, dk), (2 * hq, dv, 2 * dk), (2 * hq + hv, dv, 2 * dk + dv),
                (2 * hq + 2 * hv, dv, 2 * dk + 2 * dv))
    n = w_ref.shape[1]
    for j in range(n // tn):
        lo = j * tn
        seg = max(s for s in range(len(segments)) if segments[s][0] <= lo)
        start, width, off = segments[seg]
        acc = _dot(hb, w_ref[:, lo:lo + tn])
        if seg == 1:
            acc = acc * k_scale
        elif seg >= 3:
            acc = jax.nn.silu(acc) * gn_ref[:, lo - start:lo - start + tn]
        acc = acc.astype(BF16)
        head0 = (lo - start) // width
        for hh in range(tn // width):
            o_ref[head0 + hh, :, off:off + width] = acc[:, hh * width:(hh + 1) * width]


def _ret_inproj(y, mod, layer, norm_w, w_in_bf16, gn_w, dk):
    y_s, y_p = y
    ns, l, d = y_s.shape
    nb = ns + 1
    n = w_in_bf16.shape[1]
    hq = RET_HEADS * dk
    hv = 2 * hq
    tm = _pick(l, (512, 256, 128))
    tn = _pick(hq, (1024, 512, 256, 128))
    wph = n // RET_HEADS
    kern = functools.partial(_ret_inproj_kernel, ns=ns, dk=dk, tn=tn, k_scale=float(dk) ** -0.5)
    return pl.pallas_call(
        kern,
        out_shape=jax.ShapeDtypeStruct((nb, RET_HEADS, l, wph), BF16),
        grid=(nb, l // tm),
        in_specs=_pair_specs(ns, l // tm, tm, d) + [
            pl.BlockSpec((1, d), lambda b, i: (0, 0)),
            _mod_spec(d, layer, 0),
            _mod_spec(d, layer, 1),
            pl.BlockSpec((d, n), lambda b, i: (0, 0), pipeline_mode=pl.Buffered(1)),
            pl.BlockSpec((1, hv), lambda b, i: (0, 0)),
        ],
        out_specs=pl.BlockSpec((None, RET_HEADS, tm, wph), lambda b, i: (b, 0, i, 0)),
        compiler_params=_params(("arbitrary", "arbitrary")),
        name="ret_inproj",
    )(y_s, y_p, norm_w.reshape(1, d), mod, mod, w_in_bf16, gn_w.reshape(1, hv))


def _log_sigmoid(x):
    return jnp.minimum(x, 0.0) - jnp.log1p(jnp.exp(-jnp.abs(x)))


def _ret_kernel(*refs, ls, c, dk, has_init, emit_final):
    it = iter(refs)
    x_ref, df_ref, db_ref = (next(it) for _ in range(3))
    if has_init:
        s0f_ref, s0b_ref = next(it), next(it)
    y_ref = next(it)
    if emit_final:
        sf_out, sb_out = next(it), next(it)
    s_scr, sb_scr = next(it), next(it)

    dv = 2 * dk
    q_ref = x_ref.at[:, 0:dk]
    k_ref = x_ref.at[:, dk:2 * dk]
    v_ref = x_ref.at[:, 2 * dk:2 * dk + dv]
    gf_ref = x_ref.at[:, 2 * dk + dv:2 * dk + 2 * dv]
    gb_ref = x_ref.at[:, 2 * dk + 2 * dv:2 * dk + 3 * dv]
    nchunks = ls // c
    lg_f = _log_sigmoid(df_ref[...])[:, 0:1]
    lg_b = _log_sigmoid(db_ref[...])[:, 0:1]

    ri = lax.broadcasted_iota(I32, (c, c), 0)
    ci = lax.broadcasted_iota(I32, (c, c), 1)
    dif = (ri - ci).astype(F32)
    decay_f = jnp.where(dif >= 0, jnp.exp(jnp.maximum(dif, 0.0) * lg_f), 0.0).astype(BF16)
    decay_b = jnp.where(dif <= 0, jnp.exp(jnp.maximum(-dif, 0.0) * lg_b), 0.0).astype(BF16)
    pos = lax.broadcasted_iota(I32, (c, dk), 0).astype(F32)
    qdec_f = jnp.exp((pos + 1.0) * lg_f).astype(BF16)
    kdec_f = jnp.exp((c - 1.0 - pos) * lg_f).astype(BF16)
    qdec_b = jnp.exp((c - pos) * lg_b).astype(BF16)
    kdec_b = jnp.exp(pos * lg_b).astype(BF16)
    cdec_f = jnp.exp(c * lg_f)
    cdec_b = jnp.exp(c * lg_b)

    if has_init:
        s_scr[...] = s0b_ref[...]
    else:
        s_scr[...] = jnp.zeros_like(s_scr)

    def bwd_body(t, carry):
        n = nchunks - 1 - t
        r0 = pl.multiple_of(n * c, c)
        s = s_scr[...]
        sb_scr[n] = s.astype(BF16)
        kc = k_ref[pl.ds(r0, c), :]
        vc = v_ref[pl.ds(r0, c), :]
        s_scr[...] = s * cdec_b + _dot_tn(kc * kdec_b, vc)
        return carry

    lax.fori_loop(0, nchunks, bwd_body, 0, unroll=4 if nchunks % 4 == 0 else (2 if nchunks % 2 == 0 else 1))
    if emit_final:
        sb_out[...] = s_scr[...]

    if has_init:
        s_scr[...] = s0f_ref[...]
    else:
        s_scr[...] = jnp.zeros_like(s_scr)

    def head_norm(o):
        mu = jnp.mean(o, axis=-1, keepdims=True)
        dlt = o - mu
        var = jnp.mean(dlt * dlt, axis=-1, keepdims=True)
        return dlt * lax.rsqrt(var + EPS)

    def fwd_body(n, carry):
        r0 = pl.multiple_of(n * c, c)
        rows = pl.ds(r0, c)
        qc = q_ref[rows, :]
        kc = k_ref[rows, :]
        vc = v_ref[rows, :]
        sc = _dot_nt(qc, kc).astype(BF16)
        s = s_scr[...]
        o_f = _dot(sc * decay_f, vc) + _dot(qc * qdec_f, s.astype(BF16))
        o_b = _dot(sc * decay_b, vc) + _dot(qc * qdec_b, sb_scr[n])
        s_scr[...] = s * cdec_f + _dot_tn(kc * kdec_f, vc)
        y_ref[rows, :] = (head_norm(o_f).astype(BF16) * gf_ref[rows, :]
                          + head_norm(o_b).astype(BF16) * gb_ref[rows, :])
        return carry

    lax.fori_loop(0, nchunks, fwd_body, 0, unroll=4 if nchunks % 4 == 0 else (2 if nchunks % 2 == 0 else 1))
    if emit_final:
        sf_out[...] = s_scr[...]


def _retention(qkvg, nseq, row0, ls, dk, decay_f, decay_b, s0f=None, s0b=None, emit_final=False):
    h = RET_HEADS
    dv = 2 * dk
    c = _pick(ls, (256, 128))
    has_init = s0f is not None
    spr = qkvg.shape[2] // ls
    in_specs = [
        pl.BlockSpec((None, None, ls, qkvg.shape[3]), lambda s, hh: (row0 + s // spr, hh, s % spr, 0)),
        pl.BlockSpec((None, 1, 128), lambda s, hh: (hh, 0, 0)),
        pl.BlockSpec((None, 1, 128), lambda s, hh: (hh, 0, 0)),
    ]
    args = [qkvg,
            jnp.broadcast_to(decay_f.reshape(h, 1, 1), (h, 1, 128)),
            jnp.broadcast_to(decay_b.reshape(h, 1, 1), (h, 1, 128))]
    if has_init:
        in_specs += [pl.BlockSpec((None, None, dk, dv), lambda s, hh: (s, hh, 0, 0))] * 2
        args += [s0f, s0b]
    out_shape = [jax.ShapeDtypeStruct((nseq, ls, h * dv), BF16)]
    out_specs = [pl.BlockSpec((None, ls, dv), lambda s, hh: (s, 0, hh))]
    if emit_final:
        out_shape += [jax.ShapeDtypeStruct((nseq, h, dk, dv), F32)] * 2
        out_specs += [pl.BlockSpec((None, None, dk, dv), lambda s, hh: (s, hh, 0, 0))] * 2
    kern = functools.partial(_ret_kernel, ls=ls, c=c, dk=dk, has_init=has_init, emit_final=emit_final)
    return pl.pallas_call(
        kern,
        out_shape=out_shape,
        grid=(nseq, h),
        in_specs=in_specs,
        out_specs=out_specs,
        scratch_shapes=[pltpu.VMEM((dk, dv), F32), pltpu.VMEM((ls // c, dk, dv), BF16)],
        compiler_params=_params(("arbitrary", "arbitrary")),
        name="retention_init" if has_init else "retention_zero",
    )(*args)


def _qkv_kernel(xs_ref, xp_ref, nw_ref, sh_ref, sc_ref, w_ref, qn_ref, kn_ref, cos_ref, sa_ref, sb_ref,
                o_ref, kp_ref, vp_ref, *, ns, dh):
    nq = ATT_HEADS * dh
    nk = ATT_KV_HEADS * dh
    is_prompt = pl.program_id(0) >= ns
    tm = xs_ref.shape[0]
    nsub = 2 if tm % 256 == 0 else 1
    ts = tm // nsub
    cache_rows = []
    for sub in range(nsub):
        rows = slice(sub * ts, (sub + 1) * ts)
        x = jnp.where(is_prompt, xp_ref[rows, :], xs_ref[rows, :])
        hb = _norm_mod(x, nw_ref[...], sh_ref[...], sc_ref[...]).astype(BF16)
        cos, sa, sb = cos_ref[rows, :], sa_ref[rows, :], sb_ref[rows, :]

        def norm_rope(a, w, cos=cos, sa=sa, sb=sb):
            a = a * lax.rsqrt(jnp.mean(a * a, axis=-1, keepdims=True) + EPS) * w
            roped = a * cos + pltpu.roll(a, dh - dh // 4, 1) * sa + pltpu.roll(a, dh // 4, 1) * sb
            return a, roped

        q = _dot(hb, w_ref[:, :nq])
        for hh in range(ATT_HEADS):
            _, r = norm_rope(q[:, hh * dh:(hh + 1) * dh], qn_ref[...])
            o_ref[rows, hh * dh:(hh + 1) * dh] = r.astype(BF16)
        kvv = _dot(hb, w_ref[:, nq:])
        k_norm = []
        for hh in range(ATT_KV_HEADS):
            a, r = norm_rope(kvv[:, hh * dh:(hh + 1) * dh], kn_ref[...])
            o_ref[rows, nq + hh * dh:nq + (hh + 1) * dh] = r.astype(BF16)
            k_norm.append(a)
        o_ref[rows, nq + nk:] = kvv[:, nk:].astype(BF16)
        cache_rows.append((rows, k_norm, kvv[:, nk:]))

    @pl.when(is_prompt)
    def _():
        for rows, k_norm, v in cache_rows:
            for hh in range(ATT_KV_HEADS):
                kp_ref[rows, hh * dh:(hh + 1) * dh] = k_norm[hh]
            vp_ref[rows, :] = v


def _rope_tables(l, dh, rotate):
    axis_dim = dh // 2
    half = axis_dim // 2
    t = jnp.arange(l)
    row = (t // GRID_W).astype(F32)
    col = (t % GRID_W).astype(F32)
    inv = ROPE_THETA ** (-jnp.arange(0, axis_dim, 2, dtype=F32) / axis_dim)
    d = jnp.arange(dh)
    pos = jnp.where((d // axis_dim)[None, :] == 0, row[:, None], col[:, None])
    ang = pos * inv[d % half][None, :]
    first = ((d % axis_dim) < half)[None, :]
    cos, sin = jnp.cos(ang), jnp.sin(ang)
    if not rotate:
        cos, sin = jnp.ones_like(cos), jnp.zeros_like(sin)
    return cos, jnp.where(first, -sin, 0.0), jnp.where(first, 0.0, sin)


def _qkv_proj(y, mod, layer, norm_w, w_bf16, qn, kn, dh):
    y_s, y_p = y
    ns, l, d = y_s.shape
    nb = ns + 1
    n = w_bf16.shape[1]
    nk = ATT_KV_HEADS * dh
    tm = _pick(l, (512, 256, 128))
    tabs = [jnp.stack([a, b]) for a, b in zip(_rope_tables(l, dh, True), _rope_tables(l, dh, False))]
    tab_spec = pl.BlockSpec((None, tm, dh), lambda b, i: (jnp.where(b < ns, 0, 1), i, 0))
    prompt_spec = pl.BlockSpec((tm, nk), lambda b, i: (jnp.where(b < ns, 0, i), 0))
    return pl.pallas_call(
        functools.partial(_qkv_kernel, ns=ns, dh=dh),
        out_shape=[jax.ShapeDtypeStruct((nb, l, n), BF16), jax.ShapeDtypeStruct((l, nk), F32),
                   jax.ShapeDtypeStruct((l, nk), F32)],
        grid=(nb, l // tm),
        in_specs=_pair_specs(ns, l // tm, tm, d) + [
            pl.BlockSpec((1, d), lambda b, i: (0, 0)),
            _mod_spec(d, layer, 0),
            _mod_spec(d, layer, 1),
            pl.BlockSpec((d, n), lambda b, i: (0, 0), pipeline_mode=pl.Buffered(1)),
            pl.BlockSpec((1, dh), lambda b, i: (0, 0)),
            pl.BlockSpec((1, dh), lambda b, i: (0, 0)),
            tab_spec, tab_spec, tab_spec,
        ],
        out_specs=[pl.BlockSpec((None, tm, n), lambda b, i: (b, i, 0)), prompt_spec, prompt_spec],
        compiler_params=_params(("arbitrary", "arbitrary")),
        name="qkv_proj",
    )(y_s, y_p, norm_w.reshape(1, d), mod, mod, w_bf16, qn.reshape(1, dh), kn.reshape(1, dh), *tabs)


def _attn_kernel(*refs, dh, group, has_cache, scale):
    if has_cache:
        q_ref, k_ref, v_ref, kc_ref, vc_ref, o_ref, kall, vext = refs
        past = kc_ref.shape[0]
    else:
        q_ref, k_ref, v_ref, o_ref, kall, vext = refs
        past = 0

    @pl.when(pl.program_id(2) == 0)
    def _():
        if has_cache:
            kall[:past, :] = kc_ref[...].astype(BF16)
            vext[:past, :dh] = vc_ref[...].astype(BF16)
        kall[past:, :] = k_ref[...]
        vext[past:, :dh] = v_ref[...]
        vext[:, dh:] = jnp.ones((vext.shape[0], dh), BF16)

    c = scale * 1.4426950408889634
    tq = q_ref.shape[0]
    q = jnp.concatenate([q_ref[:, g * dh:(g + 1) * dh] for g in range(group)], axis=0)
    lk = kall.shape[0]
    ck = _pick(lk, (1152, 512, 256, 128))
    m = None
    for j in range(lk // ck):
        s = _dot_nt(q, kall[j * ck:(j + 1) * ck, :])
        mj = jnp.max(s, axis=-1, keepdims=True)
        vj = vext[j * ck:(j + 1) * ck, :]
        if m is None:
            m = mj
            o = _dot(jnp.exp2((s - m) * c).astype(BF16), vj)
        else:
            m_new = jnp.maximum(m, mj)
            o = o * jnp.exp2((m - m_new) * c) + _dot(jnp.exp2((s - m_new) * c).astype(BF16), vj)
            m = m_new
    for g in range(group):
        og = o[g * tq:(g + 1) * tq]
        o_ref[:, g * dh:(g + 1) * dh] = (og[:, :dh] / og[:, dh:dh + 1]).astype(BF16)


def _attention(qkv, nseq, seq0, ls, dh, cache_k=None, cache_v=None):
    group = ATT_HEADS // ATT_KV_HEADS
    has_cache = cache_k is not None
    tq = _pick(ls, (256, 128))
    kcol = ATT_HEADS
    vcol = ATT_HEADS + ATT_KV_HEADS
    in_specs = [
        pl.BlockSpec((None, tq, group * dh), lambda s, kh, i: (s + seq0, i, kh)),
        pl.BlockSpec((None, ls, dh), lambda s, kh, i: (s + seq0, 0, kcol + kh)),
        pl.BlockSpec((None, ls, dh), lambda s, kh, i: (s + seq0, 0, vcol + kh)),
    ]
    args = [qkv, qkv, qkv]
    past = 0
    if has_cache:
        past = cache_k.shape[1]
        in_specs += [pl.BlockSpec((None, past, dh), lambda s, kh, i: (s, 0, kh))] * 2
        args += [cache_k, cache_v]
    kern = functools.partial(_attn_kernel, dh=dh, group=group, has_cache=has_cache, scale=float(dh) ** -0.5)
    return pl.pallas_call(
        kern,
        out_shape=jax.ShapeDtypeStruct((nseq, ls, ATT_HEADS * dh), BF16),
        grid=(nseq, ATT_KV_HEADS, ls // tq),
        in_specs=in_specs,
        out_specs=pl.BlockSpec((None, tq, group * dh), lambda s, kh, i: (s, i, kh)),
        scratch_shapes=[pltpu.VMEM((past + ls, dh), BF16), pltpu.VMEM((past + ls, 2 * dh), BF16)],
        compiler_params=_params(("arbitrary", "arbitrary", "arbitrary")),
        name="attention_cache" if has_cache else "attention_self",
    )(*args)


def _pack_rows(h):
    half = h.shape[1] // 2
    u = lax.bitcast_convert_type(h.astype(BF16).astype(F32), U32)
    return (u[:, :half] >> 16) | (u[:, half:] & jnp.uint32(0xFFFF0000))


def _unpack_rows(p):
    lo = lax.bitcast_convert_type(p << 16, F32)
    hi = lax.bitcast_convert_type(p & jnp.uint32(0xFFFF0000), F32)
    return jnp.concatenate([lo, hi], axis=1).astype(BF16)


ROW_EXTRA = 128


def _outproj_kernel(os_ref, op_ref, w_ref, ys_ref, yp_ref, g1_ref, nw_ref, sh_ref, sc_ref, rwh_ref, rwl_ref,
                    yos_ref, yop_ref, h_ref, lg_ref, *, ns):
    o = _pair_load(os_ref, op_ref, ns)
    y = _pair_load(ys_ref, yp_ref, ns) + g1_ref[...] * _dot(o, w_ref[...])
    _pair_store(yos_ref, yop_ref, ns, y)
    h = _norm_mod(y, nw_ref[...], sh_ref[...], sc_ref[...])
    h_ref[...] = _pack_rows(h)
    h_hi = h.astype(BF16)
    h_lo = (h - h_hi.astype(F32)).astype(BF16)
    rwh = rwh_ref[...]
    lg_ref[...] = _dot_nt(rwh, h_hi) + _dot_nt(rwl_ref[...], h_hi) + _dot_nt(rwh, h_lo)


def _outproj(o_s, o_p, w_bf16, y, mod, layer, norm_w, rw_hi, rw_lo):
    y_s, y_p = y
    ns, l, d = y_s.shape
    nb = ns + 1
    kdim = o_s.shape[2]
    ne = rw_hi.shape[0]
    tm = _pick(l, (512, 256, 128))
    ni = l // tm
    dp = jax.eval_shape(_pack_rows, jax.ShapeDtypeStruct((8, d), F32)).shape[1]
    outs = pl.pallas_call(
        functools.partial(_outproj_kernel, ns=ns),
        out_shape=[jax.ShapeDtypeStruct(y_s.shape, F32), jax.ShapeDtypeStruct(y_p.shape, F32),
                   jax.ShapeDtypeStruct((nb, l, dp), U32), jax.ShapeDtypeStruct((nb, ne, l), F32)],
        grid=(nb, ni),
        in_specs=_pair_specs(ns, ni, tm, kdim) + [
            pl.BlockSpec((kdim, d), lambda b, i: (0, 0), pipeline_mode=pl.Buffered(1)),
        ] + _pair_specs(ns, ni, tm, d) + [
            _mod_spec(d, layer, 2),
            pl.BlockSpec((1, d), lambda b, i: (0, 0)),
            _mod_spec(d, layer, 3),
            _mod_spec(d, layer, 4),
            pl.BlockSpec((ne, d), lambda b, i: (0, 0)),
            pl.BlockSpec((ne, d), lambda b, i: (0, 0)),
        ],
        out_specs=_pair_specs(ns, ni, tm, d) + [
            pl.BlockSpec((None, tm, dp), lambda b, i: (b, i, 0)),
            pl.BlockSpec((None, ne, tm), lambda b, i: (b, 0, i))],
        compiler_params=_params(("arbitrary", "arbitrary")),
        name="outproj_router",
    )(o_s, o_p, w_bf16, y_s, y_p, mod, norm_w.reshape(1, d), mod, mod, rw_hi, rw_lo)
    return (outs[0], outs[1]), outs[2], outs[3]


def _router_kernel(lg_ref, b_ref, bkt_ref, rank_ref, wt_ref, cnt_ref, carry_ref):
    ng = N_GROUPS
    sc = jax.nn.sigmoid(lg_ref[...])
    sel = sc + b_ref[...]
    sj = [sel[j * ng:(j + 1) * ng] for j in range(EXPERTS_PER_GROUP)]
    pj = [sc[j * ng:(j + 1) * ng] for j in range(EXPERTS_PER_GROUP)]
    hi1, lo1 = jnp.maximum(sj[0], sj[1]), jnp.minimum(sj[0], sj[1])
    hi2, lo2 = jnp.maximum(sj[2], sj[3]), jnp.minimum(sj[2], sj[3])
    top1 = jnp.maximum(hi1, hi2)
    top2 = jnp.maximum(jnp.minimum(hi1, hi2), jnp.maximum(lo1, lo2))
    gs = top1 + top2
    gmax = jnp.max(gs, axis=0, keepdims=True)
    gi = lax.broadcasted_iota(I32, gs.shape, 0)
    best = jnp.min(jnp.where(gs == gmax, gi, ng), axis=0, keepdims=True)
    onehot = gi == best
    neg = -jnp.inf
    vj = [jnp.max(jnp.where(onehot, a, neg), axis=0, keepdims=True) for a in sj]
    wj = [jnp.max(jnp.where(onehot, a, neg), axis=0, keepdims=True) for a in pj]

    def argmax4(vals):
        m, i = vals[0], jnp.zeros(vals[0].shape, I32)
        for j in range(1, EXPERTS_PER_GROUP):
            gt = vals[j] > m
            m = jnp.where(gt, vals[j], m)
            i = jnp.where(gt, j, i)
        return i

    i1 = argmax4(vj)
    i2 = argmax4([jnp.where(i1 == j, neg, vj[j]) for j in range(EXPERTS_PER_GROUP)])

    def take(vals, i):
        out = vals[0]
        for j in range(1, EXPERTS_PER_GROUP):
            out = jnp.where(i == j, vals[j], out)
        return out

    w1, w2 = take(wj, i1), take(wj, i2)
    tot = w1 + w2
    w1, w2 = w1 / tot, w2 / tot

    lo, hi = jnp.minimum(i1, i2), jnp.maximum(i1, i2)
    pair = jnp.where(lo == 0, hi - 1, jnp.where(lo == 1, jnp.where(hi == 3, 3, 4), 5))
    slot_a = jnp.where(pair < 3, 0, jnp.where(pair < 5, 1, 3))
    bucket = best * N_PAIRS + pair
    a_first = i1 == slot_a
    w_a = jnp.where(a_first, w1, w2)
    w_b = jnp.where(a_first, w2, w1)

    tl = bucket.shape[1]
    nbk = carry_ref.shape[0]

    @pl.when(jnp.logical_and(pl.program_id(0) == 0, pl.program_id(1) == 0))
    def _():
        carry_ref[...] = jnp.zeros_like(carry_ref)

    onehot_b = lax.broadcasted_iota(I32, (nbk, tl), 0) == bucket
    upper = (lax.broadcasted_iota(I32, (tl, tl), 0) <= lax.broadcasted_iota(I32, (tl, tl), 1))
    prefix = _dot(onehot_b.astype(BF16), upper.astype(BF16))
    carry = carry_ref[...]
    rank = jnp.sum(jnp.where(onehot_b, prefix - 1.0 + carry, 0.0), axis=0, keepdims=True)
    carry = carry + prefix[:, tl - 1:tl]
    carry_ref[...] = carry
    bkt_ref[...] = bucket
    rank_ref[...] = rank.astype(I32)
    cnt_ref[...] = jnp.broadcast_to(carry, cnt_ref.shape)
    rows = lax.broadcasted_iota(I32, (wt_ref.shape[1], tl), 0)
    wt_ref[...] = jnp.where(rows == 0, w_a, jnp.where(rows == 1, w_b, 0.0)).T


def _router(logits, bias_perm):
    nb, ne, l = logits.shape
    tl = _pick(l, (512, 256, 128))
    nbk = N_GROUPS * N_PAIRS
    return pl.pallas_call(
        _router_kernel,
        out_shape=[jax.ShapeDtypeStruct((nb, 1, l), I32), jax.ShapeDtypeStruct((nb, 1, l), I32),
                   jax.ShapeDtypeStruct((nb, l, ROW_EXTRA), F32), jax.ShapeDtypeStruct((nbk, 128), F32)],
        grid=(nb, l // tl),
        in_specs=[pl.BlockSpec((None, ne, tl), lambda b, i: (b, 0, i)),
                  pl.BlockSpec((ne, 1), lambda b, i: (0, 0))],
        out_specs=[pl.BlockSpec((None, 1, tl), lambda b, i: (b, 0, i)),
                   pl.BlockSpec((None, 1, tl), lambda b, i: (b, 0, i)),
                   pl.BlockSpec((None, tl, ROW_EXTRA), lambda b, i: (b, i, 0)),
                   pl.BlockSpec((nbk, 128), lambda b, i: (0, 0))],
        scratch_shapes=[pltpu.VMEM((nbk, 1), F32)],
        compiler_params=_params(("arbitrary", "arbitrary")),
        name="router_rank",
    )(logits, bias_perm.reshape(ne, 1))


def _moe_kernel(ea_ref, eb_ref, nu_ref, x_ref, wga_ref, wua_ref, wda_ref, wgb_ref, wub_ref, wdb_ref, o_ref):
    del ea_ref, eb_ref
    i = pl.program_id(0)

    dq = o_ref.shape[1]

    @pl.when(i < nu_ref[0])
    def _():
        x = _unpack_rows(x_ref[:, :dq])
        w = lax.bitcast_convert_type(x_ref[:, dq:], F32)

        def expert(wg, wu, wd):
            hid = jax.nn.silu(_dot(x, wg[...])) * _dot(x, wu[...])
            return _dot(hid.astype(BF16), wd[...])

        o_ref[...] = _pack_rows(w[:, 0:1] * expert(wga_ref, wua_ref, wda_ref)
                                + w[:, 1:2] * expert(wgb_ref, wub_ref, wdb_ref))

    @pl.when(i >= nu_ref[0])
    def _():
        o_ref[...] = jnp.zeros_like(o_ref)


def _moe_experts(xs, blk_a, blk_b, n_used, wg, wu, wd, layer, bm):
    p, dp = xs.shape
    _, _, d, de = wg.shape
    nblk = p // bm
    dq = dp - ROW_EXTRA
    w_in = lambda sel: pl.BlockSpec((None, None, d, de), lambda i, ea, eb, nu: (layer, (ea, eb)[sel][i], 0, 0))
    w_out = lambda sel: pl.BlockSpec((None, None, de, d), lambda i, ea, eb, nu: (layer, (ea, eb)[sel][i], 0, 0))
    grid_spec = pltpu.PrefetchScalarGridSpec(
        num_scalar_prefetch=3,
        grid=(nblk,),
        in_specs=[pl.BlockSpec((bm, dp), lambda i, ea, eb, nu: (i, 0)),
                  w_in(0), w_in(0), w_out(0), w_in(1), w_in(1), w_out(1)],
        out_specs=pl.BlockSpec((bm, dq), lambda i, ea, eb, nu: (i, 0)),
    )
    return pl.pallas_call(
        _moe_kernel,
        out_shape=jax.ShapeDtypeStruct((p, dq), U32),
        grid_spec=grid_spec,
        compiler_params=_params(("arbitrary",)),
        name="moe_experts",
    )(blk_a, blk_b, n_used, xs, wg, wu, wd, wg, wu, wd)


def _row_plan(bucket, rank, counts, bm):
    t = bucket.size
    nbk = N_GROUPS * N_PAIRS
    cnt = counts[:, 0].astype(I32)
    padded = (cnt + bm - 1) // bm * bm
    pends = jnp.cumsum(padded)
    pstarts = pends - padded
    onehot = bucket.reshape(t, 1) == jnp.arange(nbk, dtype=I32)[None, :]
    dest = jnp.sum(jnp.where(onehot, pstarts[None, :], 0), axis=1) + rank.reshape(t)
    p = (t + bm - 1) // bm * bm + nbk * bm
    nblk = p // bm
    blk_bucket = jnp.clip(jnp.searchsorted(pends, jnp.arange(nblk, dtype=I32) * bm, side='right'), 0, nbk - 1)
    grp, pair = blk_bucket // N_PAIRS, blk_bucket % N_PAIRS
    blk_a = grp * EXPERTS_PER_GROUP + jnp.asarray(_PAIR_A, I32)[pair]
    blk_b = grp * EXPERTS_PER_GROUP + jnp.asarray(_PAIR_B, I32)[pair]
    n_used = (pends[-1:] // bm).astype(I32)
    return dest.astype(I32), blk_a.astype(I32), blk_b.astype(I32), n_used, p


def _scatter_kernel(dest_ref, src_ref, wt_ref, init_hbm, out_hbm, rowbuf, sem, *, rows):
    del init_hbm
    base = pl.program_id(0) * rows
    dp = src_ref.shape[1]
    rowbuf[:, :dp] = src_ref[...]
    rowbuf[:, dp:] = lax.bitcast_convert_type(wt_ref[...], U32)
    for r in range(rows):
        pltpu.make_async_copy(rowbuf.at[r], out_hbm.at[dest_ref[base + r]], sem).start(priority=r % 2)
    pltpu.make_async_copy(rowbuf, out_hbm.at[pl.ds(0, rows)], sem).wait()


def _scatter_rows(h_rows, wt_rows, dest, p):
    t, dp = h_rows.shape
    rows = _pick(t, (512, 256, 128))
    grid_spec = pltpu.PrefetchScalarGridSpec(
        num_scalar_prefetch=1,
        grid=(t // rows,),
        in_specs=[pl.BlockSpec((rows, dp), lambda i, dst: (i, 0)),
                  pl.BlockSpec((rows, ROW_EXTRA), lambda i, dst: (i, 0)),
                  pl.BlockSpec(memory_space=pl.ANY)],
        out_specs=pl.BlockSpec(memory_space=pl.ANY),
        scratch_shapes=[pltpu.VMEM((rows, dp + ROW_EXTRA), U32), pltpu.SemaphoreType.DMA],
    )
    return pl.pallas_call(
        functools.partial(_scatter_kernel, rows=rows),
        out_shape=jax.ShapeDtypeStruct((p, dp + ROW_EXTRA), U32),
        grid_spec=grid_spec,
        input_output_aliases={3: 0},
        compiler_params=_params(("arbitrary",)),
        name="moe_scatter_rows",
    )(dest, h_rows, wt_rows, jnp.zeros((p, dp + ROW_EXTRA), U32))


def _combine_kernel(dest_ref, ys_ref, yp_ref, g_ref, m_hbm, os_ref, op_ref, mbuf0, mbuf1, sem,
                    *, ns, th, nsteps):
    step = pl.program_id(0) * pl.num_programs(1) + pl.program_id(1)
    bufs = (mbuf0, mbuf1)

    def gather(first_row, k):
        for r in range(th):
            pltpu.make_async_copy(m_hbm.at[dest_ref[first_row + r]], bufs[k].at[r],
                                  sem.at[k]).start(priority=r % 2)

    def wait(k):
        pltpu.make_async_copy(m_hbm.at[pl.ds(0, th)], bufs[k], sem.at[k]).wait()

    @pl.when(step == 0)
    def _():
        gather(0, 0)

    is_prompt = pl.program_id(0) >= ns
    g = g_ref[...]
    row0 = step * (2 * th)
    for k in range(2):
        rows = slice(k * th, (k + 1) * th)
        wait(k)
        nxt = jnp.minimum(row0 + (k + 1) * th, (nsteps * 2 - 1) * th)
        gather(nxt, 1 - k)
        y = jnp.where(is_prompt, yp_ref[rows, :], ys_ref[rows, :])
        val = y + g * _unpack_rows(bufs[k][...]).astype(F32)

        @pl.when(jnp.logical_not(is_prompt))
        def _():
            os_ref[rows, :] = val

        @pl.when(is_prompt)
        def _():
            op_ref[rows, :] = val

    @pl.when(step == nsteps - 1)
    def _():
        wait(0)


def _combine(y, mod, layer, moe_rows, dest):
    y_s, y_p = y
    ns, l, d = y_s.shape
    nb = ns + 1
    th = _pick(l // 2, (512, 256, 128, 64))
    tm = 2 * th
    ni = l // tm
    grid_spec = pltpu.PrefetchScalarGridSpec(
        num_scalar_prefetch=1,
        grid=(nb, ni),
        in_specs=_pair_specs(ns, ni, tm, d) + [
            pl.BlockSpec((None, None, None, 1, d), lambda b, i, dst: (layer, 5, b, 0, 0)),
            pl.BlockSpec(memory_space=pl.ANY),
        ],
        out_specs=_pair_specs(ns, ni, tm, d),
        scratch_shapes=[pltpu.VMEM((th, moe_rows.shape[1]), moe_rows.dtype),
                        pltpu.VMEM((th, moe_rows.shape[1]), moe_rows.dtype),
                        pltpu.SemaphoreType.DMA((2,))],
    )
    outs = pl.pallas_call(
        functools.partial(_combine_kernel, ns=ns, th=th, nsteps=nb * ni),
        out_shape=[jax.ShapeDtypeStruct(y_s.shape, F32), jax.ShapeDtypeStruct(y_p.shape, F32)],
        grid_spec=grid_spec,
        compiler_params=_params(("arbitrary", "arbitrary")),
        name="moe_combine",
    )(dest, y_s, y_p, mod, moe_rows)
    return (outs[0], outs[1])


def kernel(x_prompt, x_sample, c, state_ret_fwd, state_ret_bwd, cache_attn_k, cache_attn_v, c_ctx,
           ada_w, ada_b, norm1_w, norm2_w, ret_w_in, ret_w_out, ret_gn_w, ret_decay_fwd, ret_decay_bwd,
           attn_w_qkv, attn_w_o, attn_q_norm, attn_k_norm, router_w, router_b,
           moe_w_gate, moe_w_up, moe_w_down):
    n_prompt, seq, d = x_prompt.shape
    ns, l, _ = x_sample.shape
    assert n_prompt * seq == l, "prompt tokens must fill exactly one sample-length row"
    depth = ada_w.shape[0]
    nb = ns + 1
    dk = d // RET_HEADS
    dv = 2 * dk
    dh = d // ATT_HEADS
    bm = MOE_ROW_BLOCK

    y = (x_sample, x_prompt.reshape(1, l, d))
    nbp = (nb + 7) // 8 * 8
    cond = jnp.zeros((nbp, d), F32).at[:ns].set(c).at[ns].set(c_ctx)
    mod = _ada_params(cond, ada_w, ada_b)

    perm = (jnp.arange(N_EXPERTS) % N_GROUPS) * EXPERTS_PER_GROUP + jnp.arange(N_EXPERTS) // N_GROUPS
    rw_t = router_w.T[perm]
    rw_hi = rw_t.astype(BF16)
    rw_lo = (rw_t - rw_hi.astype(F32)).astype(BF16)
    rb_perm = router_b[perm]
    wg_bf16, wu_bf16, wd_bf16 = moe_w_gate.astype(BF16), moe_w_up.astype(BF16), moe_w_down.astype(BF16)

    ret_f, ret_b, k_new, v_new = [], [], [], []
    for i in range(depth):
        if i % 2 == 0:
            r = i // 2
            qkvg = _ret_inproj(y, mod, i, norm1_w[i], ret_w_in[r].astype(BF16), ret_gn_w[r], dk)
            o_s = _retention(qkvg, ns, 0, l, dk, ret_decay_fwd[r], ret_decay_bwd[r],
                             s0f=state_ret_fwd[:, r], s0b=state_ret_bwd[:, r])
            o_p, s_f, s_b = _retention(qkvg, n_prompt, ns, seq, dk,
                                       ret_decay_fwd[r], ret_decay_bwd[r], emit_final=True)
            o_s = o_s[0] if isinstance(o_s, (list, tuple)) else o_s
            ret_f.append(s_f)
            ret_b.append(s_b)
            w_o = ret_w_out[r].astype(BF16)
        else:
            a = i // 2
            qkv, k_p, v_p = _qkv_proj(y, mod, i, norm1_w[i], attn_w_qkv[a].astype(BF16),
                                      attn_q_norm[a], attn_k_norm[a], dh)
            past = cache_attn_k.shape[2]
            o_s = _attention(qkv, ns, 0, l, dh,
                             cache_k=cache_attn_k[:, a].reshape(ns, past, ATT_KV_HEADS * dh),
                             cache_v=cache_attn_v[:, a].reshape(ns, past, ATT_KV_HEADS * dh))
            o_p = _attention(qkv.reshape(nb * n_prompt, seq, qkv.shape[2]), n_prompt, ns * n_prompt, seq, dh)
            k_new.append(k_p.reshape(n_prompt, seq, ATT_KV_HEADS, dh))
            v_new.append(v_p.reshape(n_prompt, seq, ATT_KV_HEADS, dh))
            w_o = attn_w_o[a].astype(BF16)
        o_p = o_p.reshape(1, l, o_p.shape[-1])
        y, h2, logits = _outproj(o_s, o_p, w_o, y, mod, i, norm2_w[i], rw_hi, rw_lo)
        bucket, rank, wt, counts = _router(logits, rb_perm)
        dest, blk_a, blk_b, n_used, p = _row_plan(bucket, rank, counts, bm)
        xs = _scatter_rows(h2.reshape(nb * l, h2.shape[2]), wt.reshape(nb * l, wt.shape[2]), dest, p)
        moe_rows = _moe_experts(xs, blk_a, blk_b, n_used, wg_bf16, wu_bf16, wd_bf16, i, bm)
        y = _combine(y, mod, i, moe_rows, dest)

    y_s, y_p = y
    return (y_p.reshape(n_prompt, seq, d), y_s, jnp.stack(ret_f, axis=1), jnp.stack(ret_b, axis=1),
            jnp.stack(k_new, axis=1), jnp.stack(v_new, axis=1))
```
